```python
import jax
import jax.numpy as jnp
from jax import lax
import numpy as np

D_MODEL = 4096
BATCH = 4
SEQ = 2048
DEPTH = 1

MLA_HEADS = 16
MLA_NOPE_DIM = 128
MLA_ROPE_DIM = 64
MLA_V_DIM = 128
MLA_Q_RANK = 768
MLA_KV_RANK = 512
SWA_HEADS = 16
SWA_KV_HEADS = 4
SWA_HEAD_DIM = 128
SWA_GROUP = SWA_HEADS // SWA_KV_HEADS
WINDOW = 128
BLOCK = 128
ROPE_THETA = 10000.0
N_GROUPS = 8
EXPERTS_PER_GROUP = 8
TOP_K_IN_GROUP = 2
EXPERT_FF = 1024
PLE_DIM = 256
LN_EPS = 1e-5
RMS_EPS = 1e-6
DEEPNORM_ALPHA = (2.0 * DEPTH) ** 0.25
DEEPNORM_BETA = (8.0 * DEPTH) ** -0.25

IN_SIZES = (MLA_Q_RANK, MLA_KV_RANK, MLA_ROPE_DIM,
            SWA_HEADS * SWA_HEAD_DIM, SWA_KV_HEADS * SWA_HEAD_DIM, SWA_KV_HEADS * SWA_HEAD_DIM,
            D_MODEL, D_MODEL)
IN_WIDTH = sum(IN_SIZES)
IN_OFFSETS = tuple(int(v) for v in np.cumsum(IN_SIZES)[:-1])

kernel_name = 'hybrid_mla_swa_hmoe_encoder'


def layer_norm(x, w, b):
    xf = x.astype(jnp.float32)
    mu = jnp.mean(xf, axis=-1, keepdims=True)
    var = jnp.mean(jnp.square(xf - mu), axis=-1, keepdims=True)
    y = (xf - mu) * lax.rsqrt(var + LN_EPS) * w.astype(jnp.float32) + b.astype(jnp.float32)
    return y.astype(x.dtype)


def rms_norm(x, w):
    xf = x.astype(jnp.float32)
    y = xf * lax.rsqrt(jnp.mean(jnp.square(xf), axis=-1, keepdims=True) + RMS_EPS) * w.astype(jnp.float32)
    return y.astype(x.dtype)


def rope_tables(positions, dim):
    inv_freq = ROPE_THETA ** (-jnp.arange(0, dim, 2, dtype=jnp.float32) / dim)
    ang = positions.astype(jnp.float32)[..., None] * inv_freq
    return jnp.cos(ang), jnp.sin(ang)


def apply_rope(x, cos, sin):
    extra = x.ndim - cos.ndim
    c = cos.reshape(cos.shape[:2] + (1,) * extra + cos.shape[2:])
    s = sin.reshape(sin.shape[:2] + (1,) * extra + sin.shape[2:])
    x1, x2 = jnp.split(x.astype(jnp.float32), 2, axis=-1)
    return jnp.concatenate([x1 * c - x2 * s, x2 * c + x1 * s], axis=-1).astype(x.dtype)


def mla_attention(q_nope, q_rope, k_nope, k_rope, v):
    b, s, h, _ = q_nope.shape
    nb = s // BLOCK
    scale = (MLA_NOPE_DIM + MLA_ROPE_DIM) ** -0.5

    def to_blocks(t):
        return t.reshape((b, nb, BLOCK) + t.shape[2:]).swapaxes(0, 1)

    def one_block(qs):
        qn, qr = qs
        sc = jnp.einsum('bqhd,bkhd->bhqk', qn, k_nope, preferred_element_type=jnp.float32)
        sc = sc + jnp.einsum('bqhr,bkr->bhqk', qr, k_rope, preferred_element_type=jnp.float32)
        prob = jax.nn.softmax(sc * scale, axis=-1)
        return jnp.einsum('bhqk,bkhd->bqhd', prob.astype(v.dtype), v)

    o = lax.map(one_block, (to_blocks(q_nope), to_blocks(q_rope)))
    return o.swapaxes(0, 1).reshape(b, s, h * v.shape[-1])


def windowed_gqa_sink(q, k, v, sink):
    b, s = q.shape[:2]
    nb = s // BLOCK
    hd = q.shape[-1]
    qb = q.reshape(b, nb, BLOCK, SWA_KV_HEADS, SWA_GROUP, hd)

    def neighbourhood(t):
        tp = jnp.pad(t, ((0, 0), (BLOCK, BLOCK), (0, 0), (0, 0))).reshape(b, nb + 2, BLOCK, SWA_KV_HEADS, hd)
        return jnp.concatenate([tp[:, :-2], tp[:, 1:-1], tp[:, 2:]], axis=2)

    kw = neighbourhood(k)
    vw = neighbourhood(v)
    sc = jnp.einsum('bnqkgd,bnpkd->bnkgqp', qb, kw, preferred_element_type=jnp.float32) * (hd ** -0.5)
    blk = jnp.arange(nb)[:, None] * BLOCK
    q_pos = blk + jnp.arange(BLOCK)[None, :]
    k_pos = blk - BLOCK + jnp.arange(3 * BLOCK)[None, :]
    valid = ((jnp.abs(q_pos[:, :, None] - k_pos[:, None, :]) <= WINDOW)
             & (k_pos >= 0)[:, None, :] & (k_pos < s)[:, None, :])
    sc = jnp.where(valid[None, :, None, None], sc, -jnp.inf)
    sink_l = sink.astype(jnp.float32).reshape(1, 1, SWA_KV_HEADS, SWA_GROUP, 1, 1)
    m = jnp.maximum(jnp.max(sc, axis=-1, keepdims=True), sink_l)
    e = jnp.exp(sc - m)
    prob = e / (jnp.sum(e, axis=-1, keepdims=True) + jnp.exp(sink_l - m))
    o = jnp.einsum('bnkgqp,bnpkd->bnqkgd', prob.astype(v.dtype), vw)
    return o.reshape(b, s, SWA_HEADS * hd)


def hierarchical_moe(x, w_group, b_group, w_router, b_expert, w_gate, w_up, w_down):
    b, s, d = x.shape
    xt = x.reshape(b * s, d)
    t = xt.shape[0]
    group_logits = jnp.einsum('td,dg->tg', xt, w_group, preferred_element_type=jnp.float32) + b_group.astype(jnp.float32)
    group_prob = jax.nn.softmax(group_logits, axis=-1)
    g_w, g_idx = lax.top_k(group_prob, 1)
    g_onehot = jax.nn.one_hot(g_idx[:, 0], N_GROUPS, dtype=jnp.float32)
    expert_logits = (jnp.einsum('td,de->te', xt, w_router, preferred_element_type=jnp.float32)
                     + b_expert.astype(jnp.float32)).reshape(t, N_GROUPS, EXPERTS_PER_GROUP)
    sel_logits = jnp.sum(g_onehot[:, :, None] * expert_logits, axis=1)
    top_l, top_i = lax.top_k(sel_logits, TOP_K_IN_GROUP)
    top_w = jax.nn.softmax(top_l, axis=-1) * g_w
    within = jnp.sum(jax.nn.one_hot(top_i, EXPERTS_PER_GROUP, dtype=jnp.float32) * top_w[..., None], axis=1)
    combine = (g_onehot[:, :, None] * within[:, None, :]).astype(x.dtype)
    out = jnp.zeros_like(xt)
    for g in range(N_GROUPS):
        hg = jnp.einsum('td,edf->tef', xt, w_gate[g])
        hu = jnp.einsum('td,edf->tef', xt, w_up[g])
        act = jax.nn.silu(hg) * hu * combine[:, g, :, None]
        out = out + jnp.einsum('tef,efd->td', act, w_down[g])
    return out.reshape(b, s, d)


def setup_inputs(seed: int = 0) -> dict:
    key = jax.random.key(seed)
    ks = jax.random.split(key, 32)
    f32 = jnp.float32
    L, D = DEPTH, D_MODEL

    def dense(k, shape, fan_in, scale=1.0):
        return jax.random.normal(k, shape, f32) * (scale * fan_in ** -0.5)

    def gain(k, shape):
        return 1.0 + 0.02 * jax.random.normal(k, shape, f32)

    def small(k, shape, s):
        return s * jax.random.normal(k, shape, f32)

    return {
        'x': jax.random.normal(ks[0], (BATCH, SEQ, D), f32),
        'p': jax.random.normal(ks[1], (L, BATCH, SEQ, PLE_DIM), f32),
        'positions': jnp.tile(jnp.arange(SEQ, dtype=jnp.int32)[None, :], (BATCH, 1)),
        'w_in': dense(ks[2], (L, D, IN_WIDTH), D),
        'q_norm': gain(ks[3], (L, MLA_Q_RANK)),
        'kv_norm': gain(ks[4], (L, MLA_KV_RANK)),
        'w_q_up': dense(ks[5], (L, MLA_Q_RANK, MLA_HEADS * (MLA_NOPE_DIM + MLA_ROPE_DIM)), MLA_Q_RANK),
        'w_kv_up': dense(ks[6], (L, MLA_KV_RANK, MLA_HEADS * (MLA_NOPE_DIM + MLA_V_DIM)), MLA_KV_RANK),
        'sink': small(ks[7], (L, SWA_HEADS), 1.0),
        'w_branch_a': dense(ks[8], (L, MLA_HEADS * MLA_V_DIM, D), MLA_HEADS * MLA_V_DIM),
        'w_branch_b': dense(ks[9], (L, SWA_HEADS * SWA_HEAD_DIM, D), SWA_HEADS * SWA_HEAD_DIM),
        'w_out': dense(ks[10], (L, D, D), D, DEEPNORM_BETA),
        'ln1_w': gain(ks[11], (L, D)),
        'ln1_b': small(ks[12], (L, D), 0.02),
        'w_group': dense(ks[13], (L, D, N_GROUPS), D),
        'b_group': small(ks[14], (L, N_GROUPS), 0.01),
        'w_expert_router': dense(ks[15], (L, D, N_GROUPS * EXPERTS_PER_GROUP), D),
        'b_expert': small(ks[16], (L, N_GROUPS * EXPERTS_PER_GROUP), 0.01),
        'w_gate': dense(ks[17], (L, N_GROUPS, EXPERTS_PER_GROUP, D, EXPERT_FF), D),
        'w_up': dense(ks[18], (L, N_GROUPS, EXPERTS_PER_GROUP, D, EXPERT_FF), D),
        'w_down': dense(ks[19], (L, N_GROUPS, EXPERTS_PER_GROUP, EXPERT_FF, D), EXPERT_FF, DEEPNORM_BETA),
        'w_ple_up': dense(ks[20], (L, PLE_DIM, D), PLE_DIM, DEEPNORM_BETA),
        'w_ple_gate': dense(ks[21], (L, D, D), D),
        'ln2_w': gain(ks[22], (L, D)),
        'ln2_b': small(ks[23], (L, D), 0.02),
    }


def reference(x, p, positions, w_in, q_norm, kv_norm, w_q_up, w_kv_up, sink, w_branch_a, w_branch_b,
              w_out, ln1_w, ln1_b, w_group, b_group, w_expert_router, b_expert, w_gate, w_up, w_down,
              w_ple_up, w_ple_gate, ln2_w, ln2_b):
    b, s, _ = x.shape
    cos_r, sin_r = rope_tables(positions, MLA_ROPE_DIM)
    cos_f, sin_f = rope_tables(positions, SWA_HEAD_DIM)
    for i in range(DEPTH):
        proj = jnp.einsum('bsd,de->bse', x, w_in[i])
        c_q, c_kv, k_r, q_b, k_b, v_b, g_a, g_b = jnp.split(proj, IN_OFFSETS, axis=-1)
        q_a = (rms_norm(c_q, q_norm[i]) @ w_q_up[i]).reshape(b, s, MLA_HEADS, MLA_NOPE_DIM + MLA_ROPE_DIM)
        q_nope, q_rope = q_a[..., :MLA_NOPE_DIM], apply_rope(q_a[..., MLA_NOPE_DIM:], cos_r, sin_r)
        kv_a = (rms_norm(c_kv, kv_norm[i]) @ w_kv_up[i]).reshape(b, s, MLA_HEADS, MLA_NOPE_DIM + MLA_V_DIM)
        k_nope, v_a = kv_a[..., :MLA_NOPE_DIM], kv_a[..., MLA_NOPE_DIM:]
        k_rope = apply_rope(k_r, cos_r, sin_r)
        o_a = mla_attention(q_nope, q_rope, k_nope, k_rope, v_a)
        qs = apply_rope(q_b.reshape(b, s, SWA_HEADS, SWA_HEAD_DIM), cos_f, sin_f)
        kswa = apply_rope(k_b.reshape(b, s, SWA_KV_HEADS, SWA_HEAD_DIM), cos_f, sin_f)
        vswa = v_b.reshape(b, s, SWA_KV_HEADS, SWA_HEAD_DIM)
        o_b = windowed_gqa_sink(qs, kswa, vswa, sink[i])
        merged = jax.nn.sigmoid(g_a) * (o_a @ w_branch_a[i]) + jax.nn.sigmoid(g_b) * (o_b @ w_branch_b[i])
        mixed = merged @ w_out[i]
        x1 = layer_norm(DEEPNORM_ALPHA * x + mixed, ln1_w[i], ln1_b[i])
        moe = hierarchical_moe(x1, w_group[i], b_group[i], w_expert_router[i], b_expert[i],
                               w_gate[i], w_up[i], w_down[i])
        ple = jax.nn.sigmoid(x1 @ w_ple_gate[i]) * (p[i].astype(x1.dtype) @ w_ple_up[i])
        x = layer_norm(DEEPNORM_ALPHA * x1 + moe + ple, ln2_w[i], ln2_b[i])
    return x
```

```python
import functools

import numpy as np
import jax
import jax.numpy as jnp
from jax import lax
from jax.experimental import pallas as pl
from jax.experimental.pallas import tpu as pltpu

F32 = jnp.float32
BF16 = jnp.bfloat16

D_MODEL = 4096
MLA_HEADS = 16
MLA_NOPE = 128
MLA_ROPE = 64
MLA_V = 128
Q_RANK = 768
KV_RANK = 512
SWA_HEADS = 16
SWA_KV_HEADS = 4
SWA_GROUP = SWA_HEADS // SWA_KV_HEADS
SWA_DIM = 128
WINDOW = 128
ROPE_THETA = 10000.0
N_GROUPS = 8
EXPERTS_PER_GROUP = 8
N_EXPERTS = N_GROUPS * EXPERTS_PER_GROUP
EXPERT_FF = 1024
PLE_DIM = 256
LN_EPS = 1e-5
RMS_EPS = 1e-6
DEPTH = 1
ALPHA = (2.0 * DEPTH) ** 0.25

OFF_KR = Q_RANK + KV_RANK
OFF_QB = OFF_KR + MLA_ROPE
OFF_GA = OFF_QB + (SWA_HEADS + 2 * SWA_KV_HEADS) * SWA_DIM
OFF_GB = OFF_GA + D_MODEL

LANES = 128
V7X_VMEM_BYTES = 64 * 1024 * 1024
MIB = 1024 * 1024

MOE_TILE = 384
MOE_FF_CHUNK = 128
ROUTE_LANES = LANES
EXPERT_LANE0 = N_GROUPS


def _params(semantics, vmem_mib):
    return pltpu.CompilerParams(dimension_semantics=semantics,
                                vmem_limit_bytes=vmem_mib * MIB)


def _sigmoid(v):
    return 1.0 / (1.0 + jnp.exp(-v))


def _dot(a, b):
    return jnp.dot(a, b, preferred_element_type=F32)


def _tables_kernel(pos_ref, inv_a_ref, inv_b_ref, sign_b_ref, cos_a, sin_a, cos_b, sin_b):
    pos = pos_ref[...].astype(F32)
    ang_a = pos * inv_a_ref[...]
    ang_b = pos * inv_b_ref[...]
    cos_a[...] = jnp.cos(ang_a)
    sin_a[...] = jnp.sin(ang_a)
    cos_b[...] = jnp.cos(ang_b)
    sin_b[...] = jnp.sin(ang_b) * sign_b_ref[...]


def _rope_tables(positions):
    t = positions.shape[0]
    tm = 1024
    inv64 = ROPE_THETA ** (-np.arange(0, MLA_ROPE, 2, dtype=np.float32) / MLA_ROPE)
    inv128 = ROPE_THETA ** (-np.arange(0, SWA_DIM, 2, dtype=np.float32) / SWA_DIM)
    inv_a = jnp.asarray(np.tile(inv64, 4)[None, :], F32)
    inv_b = jnp.asarray(np.tile(inv128, 2)[None, :], F32)
    sign_b = jnp.asarray(np.concatenate([-np.ones(64), np.ones(64)])[None, :], F32)
    row = pl.BlockSpec((tm, LANES), lambda i: (i, 0))
    const = pl.BlockSpec((1, LANES), lambda i: (0, 0))
    out = jax.ShapeDtypeStruct((t, LANES), F32)
    return pl.pallas_call(
        _tables_kernel,
        grid=(t // tm,),
        in_specs=[pl.BlockSpec((tm, 1), lambda i: (i, 0)), const, const, const],
        out_specs=[row, row, row, row],
        out_shape=[out, out, out, out],
        compiler_params=_params(("parallel",), 32),
        name="rope_tables",
    )(positions, inv_a, inv_b, sign_b)


def _latent_kernel(x_ref, w_ref, qn_ref, kvn_ref, cos_ref, sin_ref,
                   xb_ref, cq_ref, ckv_ref, kr_ref):
    xb = x_ref[...].astype(BF16)
    xb_ref[...] = xb
    acc = _dot(xb, w_ref[...])
    cq = acc[:, :Q_RANK]
    ckv = acc[:, Q_RANK:OFF_KR]
    cq = cq * lax.rsqrt(jnp.mean(cq * cq, axis=-1, keepdims=True) + RMS_EPS) * qn_ref[...]
    ckv = ckv * lax.rsqrt(jnp.mean(ckv * ckv, axis=-1, keepdims=True) + RMS_EPS) * kvn_ref[...]
    cq_ref[...] = cq.astype(BF16)
    ckv_ref[...] = ckv.astype(BF16)
    kr = acc[:, OFF_KR:OFF_KR + LANES] * cos_ref[...] + acc[:, OFF_KR + LANES:] * sin_ref[...]
    kr_ref[...] = kr.astype(BF16)


def _latent_proj(x, w_c, q_norm, kv_norm, cos_a, sin_a):
    t = x.shape[0]
    tm = 256
    wc_cols = w_c.shape[1]
    row = lambda n: pl.BlockSpec((tm, n), lambda i: (i, 0))
    const = lambda r, n: pl.BlockSpec((r, n), lambda i: (0, 0))
    return pl.pallas_call(
        _latent_kernel,
        grid=(t // tm,),
        in_specs=[row(D_MODEL), const(D_MODEL, wc_cols), const(1, Q_RANK), const(1, KV_RANK),
                  row(LANES), row(LANES)],
        out_specs=[row(D_MODEL), row(Q_RANK), row(KV_RANK), row(LANES)],
        out_shape=[jax.ShapeDtypeStruct((t, D_MODEL), BF16),
                   jax.ShapeDtypeStruct((t, Q_RANK), BF16),
                   jax.ShapeDtypeStruct((t, KV_RANK), BF16),
                   jax.ShapeDtypeStruct((t, LANES), BF16)],
        compiler_params=_params(("parallel",), 52),
        name="latent_proj",
    )(x, w_c, q_norm, kv_norm, cos_a, sin_a)


def _qup_kernel(cq_ref, wa_ref, wb_ref, cos_ref, sin_ref, q_ref, *, scale):
    cq = cq_ref[...]
    a = _dot(cq, wa_ref[...])
    b = _dot(cq, wb_ref[...])
    q_ref[:, :LANES] = (a[:, :LANES] * scale).astype(BF16)
    q_ref[:, LANES:] = ((a[:, LANES:] * cos_ref[...] + b * sin_ref[...]) * scale).astype(BF16)


def _q_up(cq, w_qa, w_qb, cos_a, sin_a):
    t = cq.shape[0]
    tm = 1024
    scale = float((MLA_NOPE + MLA_ROPE) ** -0.5)
    return pl.pallas_call(
        functools.partial(_qup_kernel, scale=scale),
        grid=(t // tm, MLA_HEADS),
        in_specs=[pl.BlockSpec((tm, Q_RANK), lambda i, j: (i, 0)),
                  pl.BlockSpec((Q_RANK, 2 * LANES), lambda i, j: (0, j)),
                  pl.BlockSpec((Q_RANK, LANES), lambda i, j: (0, j)),
                  pl.BlockSpec((tm, LANES), lambda i, j: (i, 0)),
                  pl.BlockSpec((tm, LANES), lambda i, j: (i, 0))],
        out_specs=pl.BlockSpec((tm, 2 * LANES), lambda i, j: (i, j)),
        out_shape=jax.ShapeDtypeStruct((t, MLA_HEADS * 2 * LANES), BF16),
        compiler_params=_params(("parallel", "parallel"), 32),
        name="q_up",
    )(cq, w_qa, w_qb, cos_a, sin_a)


def _kvup_kernel(ckv_ref, w_ref, kr_ref, k_ref, v_ref):
    acc = _dot(ckv_ref[...], w_ref[...])
    k_ref[:, :LANES] = acc[:, :LANES].astype(BF16)
    k_ref[:, LANES:] = kr_ref[...]
    v_ref[...] = acc[:, LANES:].astype(BF16)


def _kv_up(ckv, w_kv, kr):
    t = ckv.shape[0]
    tm = 1024
    return pl.pallas_call(
        _kvup_kernel,
        grid=(t // tm, MLA_HEADS),
        in_specs=[pl.BlockSpec((tm, KV_RANK), lambda i, j: (i, 0)),
                  pl.BlockSpec((KV_RANK, 2 * LANES), lambda i, j: (0, j)),
                  pl.BlockSpec((tm, LANES), lambda i, j: (i, 0))],
        out_specs=[pl.BlockSpec((tm, 2 * LANES), lambda i, j: (i, j)),
                   pl.BlockSpec((tm, LANES), lambda i, j: (i, j))],
        out_shape=[jax.ShapeDtypeStruct((t, MLA_HEADS * 2 * LANES), BF16),
                   jax.ShapeDtypeStruct((t, MLA_HEADS * MLA_V), BF16)],
        compiler_params=_params(("parallel", "parallel"), 32),
        name="kv_up",
    )(ckv, w_kv, kr)


SWA_TN = 512
SWA_Q_TILES = SWA_HEADS * SWA_DIM // SWA_TN
SWA_ROPE_TILES = (SWA_HEADS + SWA_KV_HEADS) * SWA_DIM // SWA_TN


def _swaproj_kernel(xb_ref, w_ref, cos_ref, sin_ref, o_ref, *, scale):
    j = pl.program_id(1)
    acc = _dot(xb_ref[...], w_ref[...])

    @pl.when(j < SWA_ROPE_TILES)
    def _():
        sc = jnp.where(j < SWA_Q_TILES, scale, 1.0).astype(F32)
        cos = cos_ref[...] * sc
        sin = sin_ref[...] * sc
        for c in range(SWA_TN // LANES):
            a = acc[:, c * LANES:(c + 1) * LANES]
            r = a * cos + pltpu.roll(a, SWA_DIM // 2, 1) * sin
            o_ref[:, c * LANES:(c + 1) * LANES] = r.astype(BF16)

    @pl.when(j >= SWA_ROPE_TILES)
    def _():
        o_ref[...] = acc.astype(BF16)


def _swa_proj(xb, w_b, cos_b, sin_b):
    t = xb.shape[0]
    tm = 1024
    n = w_b.shape[1]
    return pl.pallas_call(
        functools.partial(_swaproj_kernel, scale=float(SWA_DIM ** -0.5)),
        grid=(t // tm, n // SWA_TN),
        in_specs=[pl.BlockSpec((tm, D_MODEL), lambda i, j: (i, 0)),
                  pl.BlockSpec((D_MODEL, SWA_TN), lambda i, j: (0, j)),
                  pl.BlockSpec((tm, LANES), lambda i, j: (i, 0)),
                  pl.BlockSpec((tm, LANES), lambda i, j: (i, 0))],
        out_specs=pl.BlockSpec((tm, SWA_TN), lambda i, j: (i, j)),
        out_shape=jax.ShapeDtypeStruct((t, n), BF16),
        compiler_params=_params(("parallel", "parallel"), 48),
        name="swa_proj",
    )(xb, w_b, cos_b, sin_b)


def _mla_kernel(q_ref, k_ref, v_ref, o_ref):
    s = lax.dot_general(q_ref[...], k_ref[...], (((1,), (1,)), ((), ())),
                        preferred_element_type=F32)
    m = jnp.max(s, axis=-1, keepdims=True)
    p = jnp.exp(s - m)
    l = jnp.sum(p, axis=-1, keepdims=True)
    o = _dot(p.astype(BF16), v_ref[...])
    o_ref[...] = (o / l).astype(BF16)


def _mla_attention(qf, kf, v, batch, seq):
    t = qf.shape[0]
    tq = 512
    nq = seq // tq
    return pl.pallas_call(
        _mla_kernel,
        grid=(batch, MLA_HEADS, nq),
        in_specs=[pl.BlockSpec((tq, 2 * LANES), lambda b, h, i: (b * nq + i, h)),
                  pl.BlockSpec((seq, 2 * LANES), lambda b, h, i: (b, h)),
                  pl.BlockSpec((seq, MLA_V), lambda b, h, i: (b, h))],
        out_specs=pl.BlockSpec((tq, MLA_V), lambda b, h, i: (b * nq + i, h)),
        out_shape=jax.ShapeDtypeStruct((t, MLA_HEADS * MLA_V), BF16),
        compiler_params=_params(("parallel", "parallel", "parallel"), 48),
        name="mla_attention",
    )(qf, kf, v)


SWA_BLOCK = 128
SWA_KEYS = 3 * SWA_BLOCK


def _swa_kernel(q_ref, k_ref, v_ref, sink_ref, o_ref, *, seq):
    n = pl.program_id(2)
    ks = jnp.clip(n * SWA_BLOCK - SWA_BLOCK, 0, seq - SWA_KEYS)
    ks = pl.multiple_of(ks, SWA_BLOCK)
    kw = k_ref[pl.ds(ks, SWA_KEYS), :]
    vw = v_ref[pl.ds(ks, SWA_KEYS), :]
    q = q_ref[...]
    q4 = jnp.concatenate([q[:, g * SWA_DIM:(g + 1) * SWA_DIM] for g in range(SWA_GROUP)], axis=0)
    s = lax.dot_general(q4, kw, (((1,), (1,)), ((), ())), preferred_element_type=F32)
    rows = lax.broadcasted_iota(jnp.int32, s.shape, 0)
    cols = lax.broadcasted_iota(jnp.int32, s.shape, 1)
    q_pos = n * SWA_BLOCK + (rows & (SWA_BLOCK - 1))
    k_pos = ks + cols
    s = jnp.where(jnp.abs(q_pos - k_pos) <= WINDOW, s, -jnp.inf)
    sink = sink_ref[:, :1]
    m = jnp.maximum(jnp.max(s, axis=-1, keepdims=True), sink)
    e = jnp.exp(s - m)
    denom = jnp.sum(e, axis=-1, keepdims=True) + jnp.exp(sink - m)
    o = _dot(e.astype(BF16), vw) / denom
    for g in range(SWA_GROUP):
        o_ref[:, g * SWA_DIM:(g + 1) * SWA_DIM] = o[g * SWA_BLOCK:(g + 1) * SWA_BLOCK].astype(BF16)


def _swa_attention(qkv, sink_rows, batch, seq):
    t = qkv.shape[0]
    nb = seq // SWA_BLOCK
    gw = SWA_GROUP * SWA_DIM
    k_col0 = SWA_HEADS
    v_col0 = SWA_HEADS + SWA_KV_HEADS
    return pl.pallas_call(
        functools.partial(_swa_kernel, seq=seq),
        grid=(batch, SWA_KV_HEADS, nb),
        in_specs=[pl.BlockSpec((SWA_BLOCK, gw), lambda b, h, n: (b * nb + n, h)),
                  pl.BlockSpec((seq, SWA_DIM), lambda b, h, n: (b, k_col0 + h)),
                  pl.BlockSpec((seq, SWA_DIM), lambda b, h, n: (b, v_col0 + h)),
                  pl.BlockSpec((None, SWA_GROUP * SWA_BLOCK, LANES), lambda b, h, n: (h, 0, 0))],
        out_specs=pl.BlockSpec((SWA_BLOCK, gw), lambda b, h, n: (b * nb + n, h)),
        out_shape=jax.ShapeDtypeStruct((t, SWA_HEADS * SWA_DIM), BF16),
        compiler_params=_params(("parallel", "parallel", "parallel"), 32),
        name="swa_attention",
    )(qkv, qkv, qkv, sink_rows)


def _merge_kernel(xb_ref, oa_ref, ob_ref, wga_ref, wgb_ref, wa_ref, wb_ref, o_ref):
    xb = xb_ref[...]
    ga = _sigmoid(_dot(xb, wga_ref[...]))
    ya = _dot(oa_ref[...], wa_ref[...])
    acc = ga * ya
    gb = _sigmoid(_dot(xb, wgb_ref[...]))
    yb = _dot(ob_ref[...], wb_ref[...])
    o_ref[...] = (acc + gb * yb).astype(BF16)


def _branch_merge(xb, o_a, o_b, w_ga, w_gb, w_a, w_b):
    t = xb.shape[0]
    tm, tn = 512, 256
    ka = o_a.shape[1]
    row = lambda k: pl.BlockSpec((tm, k), lambda i, j: (i, 0))
    col = lambda k: pl.BlockSpec((k, tn), lambda i, j: (0, j))
    return pl.pallas_call(
        _merge_kernel,
        grid=(t // tm, D_MODEL // tn),
        in_specs=[row(D_MODEL), row(ka), row(ka), col(D_MODEL), col(D_MODEL), col(ka), col(ka)],
        out_specs=pl.BlockSpec((tm, tn), lambda i, j: (i, j)),
        out_shape=jax.ShapeDtypeStruct((t, D_MODEL), BF16),
        compiler_params=_params(("parallel", "parallel"), 48),
        name="branch_merge",
    )(xb, o_a, o_b, w_ga, w_gb, w_a, w_b)


def _outproj_kernel(m_ref, w_ref, x_ref, o_ref):
    o_ref[...] = ALPHA * x_ref[...] + _dot(m_ref[...], w_ref[...])


def _out_proj(merged, w_out, x):
    t = x.shape[0]
    tm, tn = 1024, 512
    return pl.pallas_call(
        _outproj_kernel,
        grid=(t // tm, D_MODEL // tn),
        in_specs=[pl.BlockSpec((tm, D_MODEL), lambda i, j: (i, 0)),
                  pl.BlockSpec((D_MODEL, tn), lambda i, j: (0, j)),
                  pl.BlockSpec((tm, tn), lambda i, j: (i, j))],
        out_specs=pl.BlockSpec((tm, tn), lambda i, j: (i, j)),
        out_shape=jax.ShapeDtypeStruct((t, D_MODEL), F32),
        compiler_params=_params(("parallel", "parallel"), 48),
        name="out_proj",
    )(merged, w_out, x)


def _layer_norm(v, w, b):
    mu = jnp.mean(v, axis=-1, keepdims=True)
    c = v - mu
    var = jnp.mean(c * c, axis=-1, keepdims=True)
    return c * lax.rsqrt(var + LN_EPS) * w + b


def _split_bf16(v):
    hi = v.astype(BF16)
    lo = (v - hi.astype(F32)).astype(BF16)
    return hi, lo


def _ln_route_kernel(pre_ref, lnw_ref, lnb_ref, wr_ref, br_ref,
                     x1_ref, x1b_ref, route_ref, cnt_ref, carry):
    i = pl.program_id(0)

    @pl.when(i == 0)
    def _():
        carry[...] = jnp.zeros_like(carry)

    x1 = _layer_norm(pre_ref[...], lnw_ref[...], lnb_ref[...])
    x1_ref[...] = x1
    x1b_ref[...] = x1.astype(BF16)

    xh, xl = _split_bf16(x1)
    wh, wl = _split_bf16(wr_ref[...])
    logits = _dot(xh, wh) + (_dot(xl, wh) + _dot(xh, wl)) + br_ref[...]

    tm = logits.shape[0]
    lane = lax.broadcasted_iota(jnp.int32, logits.shape, 1)
    neg = -jnp.inf
    gl = jnp.where(lane < N_GROUPS, logits, neg)
    gmax = jnp.max(gl, axis=-1, keepdims=True)
    gsum = jnp.sum(jnp.exp(gl - gmax), axis=-1, keepdims=True)
    g_w = 1.0 / gsum
    g_idx = jnp.min(jnp.where(gl == gmax, lane, ROUTE_LANES), axis=-1, keepdims=True)
    d = lane - (EXPERT_LANE0 + g_idx * EXPERTS_PER_GROUP)
    el = jnp.where((d >= 0) & (d < EXPERTS_PER_GROUP), logits, neg)
    m1 = jnp.max(el, axis=-1, keepdims=True)
    i1 = jnp.min(jnp.where(el == m1, lane, ROUTE_LANES), axis=-1, keepdims=True)
    el2 = jnp.where(lane == i1, neg, el)
    m2 = jnp.max(el2, axis=-1, keepdims=True)
    i2 = jnp.min(jnp.where(el2 == m2, lane, ROUTE_LANES), axis=-1, keepdims=True)
    e21 = jnp.exp(m2 - m1)
    w1 = g_w / (1.0 + e21)
    w2 = g_w * e21 / (1.0 + e21)

    oh1 = (lane == i1).astype(F32)
    oh2 = (lane == i2).astype(F32)
    oh = oh1 + oh2
    r_i = lax.broadcasted_iota(jnp.int32, (tm, tm), 0)
    c_i = lax.broadcasted_iota(jnp.int32, (tm, tm), 1)
    tri = jnp.where(c_i < r_i, 1.0, 0.0).astype(BF16)
    before = _dot(tri, oh.astype(BF16)) + carry[...]
    rank1 = jnp.sum(oh1 * before, axis=-1, keepdims=True)
    rank2 = jnp.sum(oh2 * before, axis=-1, keepdims=True)
    carry[...] = carry[...] + jnp.sum(oh, axis=0, keepdims=True)
    cnt_ref[...] = carry[...]

    slab = jnp.where(lane == 0, (i1 - EXPERT_LANE0).astype(F32), 0.0)
    slab = jnp.where(lane == 1, (i2 - EXPERT_LANE0).astype(F32), slab)
    slab = jnp.where(lane == 2, w1, slab)
    slab = jnp.where(lane == 3, w2, slab)
    slab = jnp.where(lane == 4, rank1, slab)
    slab = jnp.where(lane == 5, rank2, slab)
    route_ref[...] = slab


def _ln_route(pre1, ln_w, ln_b, w_r, b_r):
    t = pre1.shape[0]
    tm = 256
    row = lambda n: pl.BlockSpec((tm, n), lambda i: (i, 0))
    const = lambda r, n: pl.BlockSpec((r, n), lambda i: (0, 0))
    return pl.pallas_call(
        _ln_route_kernel,
        grid=(t // tm,),
        in_specs=[row(D_MODEL), const(1, D_MODEL), const(1, D_MODEL),
                  const(D_MODEL, ROUTE_LANES), const(1, ROUTE_LANES)],
        out_specs=[row(D_MODEL), row(D_MODEL), row(ROUTE_LANES), const(1, ROUTE_LANES)],
        out_shape=[jax.ShapeDtypeStruct((t, D_MODEL), F32),
                   jax.ShapeDtypeStruct((t, D_MODEL), BF16),
                   jax.ShapeDtypeStruct((t, ROUTE_LANES), F32),
                   jax.ShapeDtypeStruct((1, ROUTE_LANES), F32)],
        scratch_shapes=[pltpu.VMEM((1, ROUTE_LANES), F32)],
        compiler_params=_params(("arbitrary",), 48),
        name="ln1_route",
    )(pre1, ln_w, ln_b, w_r, b_r)


def _gather_kernel(src_ref, nv_ref, x_hbm, o_ref, sem):
    i = pl.program_id(0)

    @pl.when(i < nv_ref[0])
    def _():
        base = i * MOE_TILE

        def issue(r, c):
            pltpu.make_async_copy(x_hbm.at[pl.ds(src_ref[base + r], 1)],
                                  o_ref.at[pl.ds(r, 1)], sem).start()
            return c

        lax.fori_loop(0, MOE_TILE, issue, 0)
        pltpu.make_async_copy(x_hbm.at[pl.ds(0, MOE_TILE)], o_ref, sem).wait()


def _dispatch_gather(src, nv, x1, n_tiles):
    d = x1.shape[1]
    grid_spec = pltpu.PrefetchScalarGridSpec(
        num_scalar_prefetch=2,
        grid=(n_tiles,),
        in_specs=[pl.BlockSpec(memory_space=pl.ANY)],
        out_specs=pl.BlockSpec((MOE_TILE, d), lambda i, src, nv: (jnp.minimum(i, nv[0] - 1), 0)),
        scratch_shapes=[pltpu.SemaphoreType.DMA(())],
    )
    return pl.pallas_call(
        _gather_kernel,
        grid_spec=grid_spec,
        out_shape=jax.ShapeDtypeStruct((n_tiles * MOE_TILE, d), x1.dtype),
        compiler_params=_params(("arbitrary",), 32),
        name="dispatch_gather",
    )(src, nv, x1)


def _moe_kernel(te_ref, nv_ref, xs_ref, wg_ref, wu_ref, wd_ref, ys_ref, xb_scr):
    i = pl.program_id(0)
    j = pl.program_id(1)

    @pl.when(i < nv_ref[0])
    def _():
        @pl.when(j == 0)
        def _():
            xb_scr[...] = xs_ref[...].astype(BF16)

        xb = xb_scr[...]
        hg = _dot(xb, wg_ref[...].astype(BF16))
        hu = _dot(xb, wu_ref[...].astype(BF16))
        act = (hg * _sigmoid(hg) * hu).astype(BF16)
        y = _dot(act, wd_ref[...].astype(BF16))

        @pl.when(j == 0)
        def _():
            ys_ref[...] = y

        @pl.when(j > 0)
        def _():
            ys_ref[...] += y


def _moe_experts(te, nv, xs, w_gate, w_up, w_down, n_tiles):
    d = xs.shape[1]
    nf = EXPERT_FF // MOE_FF_CHUNK

    def row_map(i, j, te, nv):
        return (jnp.minimum(i, nv[0] - 1), 0)

    def up_map(i, j, te, nv):
        return (te[i], 0, jnp.where(i < nv[0], j, nf - 1))

    def down_map(i, j, te, nv):
        return (te[i], jnp.where(i < nv[0], j, nf - 1), 0)

    grid_spec = pltpu.PrefetchScalarGridSpec(
        num_scalar_prefetch=2,
        grid=(n_tiles, nf),
        in_specs=[pl.BlockSpec((MOE_TILE, d), row_map),
                  pl.BlockSpec((None, d, MOE_FF_CHUNK), up_map),
                  pl.BlockSpec((None, d, MOE_FF_CHUNK), up_map),
                  pl.BlockSpec((None, MOE_FF_CHUNK, d), down_map)],
        out_specs=pl.BlockSpec((MOE_TILE, d), row_map),
        scratch_shapes=[pltpu.VMEM((MOE_TILE, d), BF16)],
    )
    return pl.pallas_call(
        _moe_kernel,
        grid_spec=grid_spec,
        out_shape=jax.ShapeDtypeStruct((n_tiles * MOE_TILE, d), F32),
        compiler_params=_params(("arbitrary", "arbitrary"), 52),
        name="moe_experts",
    )(te, nv, xs, w_gate, w_up, w_down)


def _ple_kernel(x1b_ref, wpg_ref, p_ref, wpu_ref, x1_ref, o_ref):
    gate = _sigmoid(_dot(x1b_ref[...], wpg_ref[...]))
    up = _dot(p_ref[...].astype(BF16), wpu_ref[...])
    o_ref[...] = ALPHA * x1_ref[...] + gate * up


def _ple(x1b, w_pg, p, w_pu, x1):
    t = x1.shape[0]
    tm, tn = 1024, 512
    return pl.pallas_call(
        _ple_kernel,
        grid=(t // tm, D_MODEL // tn),
        in_specs=[pl.BlockSpec((tm, D_MODEL), lambda i, j: (i, 0)),
                  pl.BlockSpec((D_MODEL, tn), lambda i, j: (0, j)),
                  pl.BlockSpec((tm, PLE_DIM), lambda i, j: (i, 0)),
                  pl.BlockSpec((PLE_DIM, tn), lambda i, j: (0, j)),
                  pl.BlockSpec((tm, tn), lambda i, j: (i, j))],
        out_specs=pl.BlockSpec((tm, tn), lambda i, j: (i, j)),
        out_shape=jax.ShapeDtypeStruct((t, D_MODEL), F32),
        compiler_params=_params(("parallel", "parallel"), 48),
        name="ple",
    )(x1b, w_pg, p, w_pu, x1)


def _final_kernel(pos1_ref, pos2_ref, pre_ref, route_ref, lnw_ref, lnb_ref, ys_hbm,
                  o_ref, g1, g2, sem):
    i = pl.program_id(0)
    tm = pre_ref.shape[0]
    base = i * tm

    def issue(r, c):
        pltpu.make_async_copy(ys_hbm.at[pl.ds(pos1_ref[base + r], 1)],
                              g1.at[pl.ds(r, 1)], sem).start()
        pltpu.make_async_copy(ys_hbm.at[pl.ds(pos2_ref[base + r], 1)],
                              g2.at[pl.ds(r, 1)], sem).start()
        return c

    lax.fori_loop(0, tm, issue, 0)
    pltpu.make_async_copy(ys_hbm.at[pl.ds(0, tm)], g1, sem).wait()
    pltpu.make_async_copy(ys_hbm.at[pl.ds(0, tm)], g2, sem).wait()
    route = route_ref[...]
    w1 = route[:, 2:3]
    w2 = route[:, 3:4]
    v = pre_ref[...] + (w1 * g1[...] + w2 * g2[...])
    o_ref[...] = _layer_norm(v, lnw_ref[...], lnb_ref[...])


def _final(pos1, pos2, pre2, route, ln_w, ln_b, ys):
    t, d = pre2.shape
    tm = 256
    grid_spec = pltpu.PrefetchScalarGridSpec(
        num_scalar_prefetch=2,
        grid=(t // tm,),
        in_specs=[pl.BlockSpec((tm, d), lambda i, p1, p2: (i, 0)),
                  pl.BlockSpec((tm, ROUTE_LANES), lambda i, p1, p2: (i, 0)),
                  pl.BlockSpec((1, d), lambda i, p1, p2: (0, 0)),
                  pl.BlockSpec((1, d), lambda i, p1, p2: (0, 0)),
                  pl.BlockSpec(memory_space=pl.ANY)],
        out_specs=pl.BlockSpec((tm, d), lambda i, p1, p2: (i, 0)),
        scratch_shapes=[pltpu.VMEM((tm, d), F32), pltpu.VMEM((tm, d), F32),
                        pltpu.SemaphoreType.DMA(())],
    )
    return pl.pallas_call(
        _final_kernel,
        grid_spec=grid_spec,
        out_shape=jax.ShapeDtypeStruct((t, d), F32),
        compiler_params=_params(("arbitrary",), 48),
        name="combine_ln2",
    )(pos1, pos2, pre2, route, ln_w, ln_b, ys)


def _rot_half_cols(w, half):
    return jnp.concatenate([-w[..., half:], w[..., :half]], axis=-1)


def _prep_latent_weight(w_in):
    d = w_in.shape[0]
    w_kr = w_in[:, OFF_KR:OFF_QB]
    z = jnp.zeros((d, LANES - MLA_ROPE), w_in.dtype)
    return jnp.concatenate([w_in[:, :OFF_KR], w_kr, z, _rot_half_cols(w_kr, MLA_ROPE // 2), z],
                           axis=1).astype(BF16)


def _prep_q_weights(w_q_up):
    r = w_q_up.shape[0]
    wq = w_q_up.reshape(r, MLA_HEADS, MLA_NOPE + MLA_ROPE)
    nope, rope = wq[:, :, :MLA_NOPE], wq[:, :, MLA_NOPE:]
    z = jnp.zeros((r, MLA_HEADS, LANES - MLA_ROPE), w_q_up.dtype)
    w_a = jnp.concatenate([nope, rope, z], axis=-1).reshape(r, MLA_HEADS * 2 * LANES)
    w_b = jnp.concatenate([_rot_half_cols(rope, MLA_ROPE // 2), z], axis=-1).reshape(r, MLA_HEADS * LANES)
    return w_a.astype(BF16), w_b.astype(BF16)


def _layer(x, p, tabs, batch, seq, w_in, q_norm, kv_norm, w_q_up, w_kv_up, sink, w_branch_a,
           w_branch_b, w_out, ln1_w, ln1_b, w_group, b_group, w_expert_router, b_expert,
           w_gate, w_up, w_down, w_ple_up, w_ple_gate, ln2_w, ln2_b):
    t = x.shape[0]
    cos_a, sin_a, cos_b, sin_b = tabs

    w_c = _prep_latent_weight(w_in)
    xb, cq, ckv, kr = _latent_proj(x, w_c, q_norm[None, :], kv_norm[None, :], cos_a, sin_a)
    w_qa, w_qb = _prep_q_weights(w_q_up)
    qf = _q_up(cq, w_qa, w_qb, cos_a, sin_a)
    kf, v_a = _kv_up(ckv, w_kv_up.astype(BF16), kr)
    o_a = _mla_attention(qf, kf, v_a, batch, seq)

    qkv_b = _swa_proj(xb, w_in[:, OFF_QB:OFF_GA].astype(BF16), cos_b, sin_b)
    sink_rows = jnp.broadcast_to(
        jnp.repeat(sink.astype(F32).reshape(SWA_KV_HEADS, SWA_GROUP), SWA_BLOCK, axis=1)[:, :, None],
        (SWA_KV_HEADS, SWA_GROUP * SWA_BLOCK, LANES))
    o_b = _swa_attention(qkv_b, sink_rows, batch, seq)

    merged = _branch_merge(xb, o_a, o_b, w_in[:, OFF_GA:OFF_GB].astype(BF16),
                           w_in[:, OFF_GB:].astype(BF16), w_branch_a.astype(BF16),
                           w_branch_b.astype(BF16))
    pre1 = _out_proj(merged, w_out.astype(BF16), x)

    pad = ROUTE_LANES - N_GROUPS - N_EXPERTS
    w_r = jnp.concatenate([w_group, w_expert_router, jnp.zeros((D_MODEL, pad), F32)], axis=1)
    b_r = jnp.concatenate([b_group, b_expert, jnp.zeros((pad,), F32)])[None, :]
    x1, x1b, route, counts = _ln_route(pre1, ln1_w[None, :], ln1_b[None, :], w_r, b_r)

    n_tiles = (2 * t) // MOE_TILE + N_EXPERTS
    e1 = route[:, 0].astype(jnp.int32)
    e2 = route[:, 1].astype(jnp.int32)
    r1 = route[:, 4].astype(jnp.int32)
    r2 = route[:, 5].astype(jnp.int32)
    cnt = counts[0, EXPERT_LANE0:EXPERT_LANE0 + N_EXPERTS].astype(jnp.int32)
    tiles_e = (cnt + MOE_TILE - 1) // MOE_TILE
    tile_end = jnp.cumsum(tiles_e)
    tile_start = tile_end - tiles_e
    nv = tile_end[-1:]
    pos1 = tile_start[e1] * MOE_TILE + r1
    pos2 = tile_start[e2] * MOE_TILE + r2
    tile_ids = jnp.minimum(jnp.arange(n_tiles, dtype=jnp.int32), nv[0] - 1)
    te = jnp.searchsorted(tile_end, tile_ids, side="right").astype(jnp.int32)
    tok = jnp.arange(t, dtype=jnp.int32)
    src = jnp.zeros((n_tiles * MOE_TILE,), jnp.int32).at[jnp.concatenate([pos1, pos2])].set(
        jnp.concatenate([tok, tok]))

    xs = _dispatch_gather(src, nv, x1, n_tiles)
    ys = _moe_experts(te, nv, xs,
                      w_gate.reshape(N_EXPERTS, D_MODEL, EXPERT_FF),
                      w_up.reshape(N_EXPERTS, D_MODEL, EXPERT_FF),
                      w_down.reshape(N_EXPERTS, EXPERT_FF, D_MODEL), n_tiles)

    pre2 = _ple(x1b, w_ple_gate.astype(BF16), p, w_ple_up.astype(BF16), x1)
    return _final(pos1, pos2, pre2, route, ln2_w[None, :], ln2_b[None, :], ys)


def kernel(x, p, positions, w_in, q_norm, kv_norm, w_q_up, w_kv_up, sink, w_branch_a, w_branch_b,
           w_out, ln1_w, ln1_b, w_group, b_group, w_expert_router, b_expert, w_gate, w_up, w_down,
           w_ple_up, w_ple_gate, ln2_w, ln2_b):
    batch, seq, d = x.shape
    t = batch * seq
    h = x.reshape(t, d)
    tabs = _rope_tables(positions.reshape(t, 1))
    for i in range(w_in.shape[0]):
        h = _layer(h, p[i].reshape(t, PLE_DIM), tabs, batch, seq, w_in[i], q_norm[i], kv_norm[i],
                   w_q_up[i], w_kv_up[i], sink[i], w_branch_a[i], w_branch_b[i], w_out[i],
                   ln1_w[i], ln1_b[i], w_group[i], b_group[i], w_expert_router[i], b_expert[i],
                   w_gate[i], w_up[i], w_down[i], w_ple_up[i], w_ple_gate[i], ln2_w[i], ln2_b[i])
    return h.reshape(batch, seq, d)
```

```python
import functools
import math

import numpy as np
import jax
import jax.numpy as jnp
from jax import lax
from jax.experimental import pallas as pl
from jax.experimental.pallas import tpu as pltpu

F32 = jnp.float32
BF16 = jnp.bfloat16

D_MODEL = 4096
MLA_HEADS = 16
MLA_NOPE = 128
MLA_ROPE = 64
MLA_V = 128
Q_RANK = 768
KV_RANK = 512
SWA_HEADS = 16
SWA_KV_HEADS = 4
SWA_GROUP = SWA_HEADS // SWA_KV_HEADS
SWA_DIM = 128
WINDOW = 128
ROPE_THETA = 10000.0
N_GROUPS = 8
EXPERTS_PER_GROUP = 8
N_EXPERTS = N_GROUPS * EXPERTS_PER_GROUP
EXPERT_FF = 1024
PLE_DIM = 256
LN_EPS = 1e-5
RMS_EPS = 1e-6
DEPTH = 1
ALPHA = (2.0 * DEPTH) ** 0.25
LOG2E = math.log2(math.e)

OFF_KR = Q_RANK + KV_RANK
OFF_QB = OFF_KR + MLA_ROPE
SWA_COLS = (SWA_HEADS + 2 * SWA_KV_HEADS) * SWA_DIM
IN_WIDTH = OFF_QB + SWA_COLS + 2 * D_MODEL

LANES = 128
MIB = 1024 * 1024

ROW_CHUNKS = D_MODEL // LANES

MOE_TILE = 336
MOE_KC = 512
MOE_NA = D_MODEL // MOE_KC
MOE_NC = 1024
MOE_NB = D_MODEL // MOE_NC
MOE_STEPS = MOE_NA + MOE_NB
MOE_ROWS_PER_STEP = MOE_TILE // MOE_STEPS
ROUTE_LANES = LANES
EXPERT_LANE0 = N_GROUPS
ROUTE_FIELDS = 8


def _params(semantics, vmem_mib):
    return pltpu.CompilerParams(dimension_semantics=semantics,
                                vmem_limit_bytes=vmem_mib * MIB)


def _sigmoid(v):
    return 1.0 / (1.0 + jnp.exp(-v))


def _dot(a, b):
    return jnp.dot(a, b, preferred_element_type=F32)


def _rot_half_64(v):
    lane = lax.broadcasted_iota(jnp.int32, v.shape, 1)
    half = MLA_ROPE // 2
    fwd = pltpu.roll(v, half, 1)
    bwd = pltpu.roll(v, LANES - half, 1)
    return jnp.where(lane < half, -bwd, jnp.where(lane < MLA_ROPE, fwd, 0.0))


def _tables_kernel(pos_ref, inv_a_ref, inv_b_ref, sign_b_ref, cos_a, sin_a, cos_b, sin_b):
    pos = pos_ref[...].astype(F32)
    ang_a = pos * inv_a_ref[...]
    ang_b = pos * inv_b_ref[...]
    cos_a[...] = jnp.cos(ang_a)
    sin_a[...] = jnp.sin(ang_a)
    cos_b[...] = jnp.cos(ang_b)
    sin_b[...] = jnp.sin(ang_b) * sign_b_ref[...]


def _rope_tables(positions):
    t = positions.shape[0]
    tm = 1024
    inv64 = ROPE_THETA ** (-np.arange(0, MLA_ROPE, 2, dtype=np.float32) / MLA_ROPE)
    inv128 = ROPE_THETA ** (-np.arange(0, SWA_DIM, 2, dtype=np.float32) / SWA_DIM)
    inv_a = jnp.asarray(np.tile(inv64, 4)[None, :], F32)
    inv_b = jnp.asarray(np.tile(inv128, 2)[None, :], F32)
    sign_b = jnp.asarray(np.concatenate([-np.ones(64), np.ones(64)])[None, :], F32)
    row = pl.BlockSpec((tm, LANES), lambda i: (i, 0))
    const = pl.BlockSpec((1, LANES), lambda i: (0, 0))
    out = jax.ShapeDtypeStruct((t, LANES), F32)
    return pl.pallas_call(
        _tables_kernel,
        grid=(t // tm,),
        in_specs=[pl.BlockSpec((tm, 1), lambda i: (i, 0)), const, const, const],
        out_specs=[row, row, row, row],
        out_shape=[out, out, out, out],
        compiler_params=_params(("parallel",), 32),
        name="rope_tables",
    )(positions, inv_a, inv_b, sign_b)


LAT_COLS = OFF_KR + LANES
SHIFT_COLS = IN_WIDTH - OFF_QB
SHIFT_TN = 256


def _cast_kernel(w_ref, o_ref):
    o_ref[...] = w_ref[...].astype(BF16)


def _cast_shift_kernel(a_ref, b_ref, o_ref):
    full = jnp.concatenate([a_ref[...], b_ref[...]], axis=1)
    half = LANES // 2
    o_ref[...] = full[:, half:half + SHIFT_TN].astype(BF16)


def _prep_input_weight(w_in, layer):
    d = w_in.shape[1]
    tk = 512
    w_lat = pl.pallas_call(
        _cast_kernel,
        grid=(d // tk,),
        in_specs=[pl.BlockSpec((None, tk, LAT_COLS), lambda i: (layer, i, 0))],
        out_specs=pl.BlockSpec((tk, LAT_COLS), lambda i: (i, 0)),
        out_shape=jax.ShapeDtypeStruct((d, LAT_COLS), BF16),
        compiler_params=_params(("parallel",), 32),
        name="cast_latent_w",
    )(w_in)
    tk = 2048
    a0 = OFF_KR // SHIFT_TN
    b0 = (OFF_KR + SHIFT_TN) // LANES
    w_sh = pl.pallas_call(
        _cast_shift_kernel,
        grid=(d // tk, SHIFT_COLS // SHIFT_TN),
        in_specs=[pl.BlockSpec((None, tk, SHIFT_TN), lambda i, j: (layer, i, a0 + j)),
                  pl.BlockSpec((None, tk, LANES), lambda i, j: (layer, i, b0 + 2 * j))],
        out_specs=pl.BlockSpec((tk, SHIFT_TN), lambda i, j: (i, j)),
        out_shape=jax.ShapeDtypeStruct((d, SHIFT_COLS), BF16),
        compiler_params=_params(("parallel", "parallel"), 32),
        name="cast_shift_w",
    )(w_in, w_in)
    return w_lat, w_sh


def _latent_kernel(x_ref, w_ref, qn_ref, kvn_ref, cos_ref, sin_ref,
                   xb_ref, cq_ref, ckv_ref, kr_ref):
    xb = x_ref[...].astype(BF16)
    xb_ref[...] = xb
    acc = _dot(xb, w_ref[...])
    cq = acc[:, :Q_RANK]
    ckv = acc[:, Q_RANK:OFF_KR]
    cq = cq * lax.rsqrt(jnp.mean(cq * cq, axis=-1, keepdims=True) + RMS_EPS) * qn_ref[...]
    ckv = ckv * lax.rsqrt(jnp.mean(ckv * ckv, axis=-1, keepdims=True) + RMS_EPS) * kvn_ref[...]
    cq_ref[...] = cq.astype(BF16)
    ckv_ref[...] = ckv.astype(BF16)
    k = acc[:, OFF_KR:]
    lane = lax.broadcasted_iota(jnp.int32, k.shape, 1)
    kr = jnp.where(lane < MLA_ROPE, k * cos_ref[...], 0.0) + _rot_half_64(k) * sin_ref[...]
    kr_ref[...] = kr.astype(BF16)


def _latent_proj(x, w_lat, q_norm, kv_norm, cos_a, sin_a):
    t = x.shape[0]
    tm = 256
    row = lambda n: pl.BlockSpec((tm, n), lambda i: (i, 0))
    const = lambda r, n: pl.BlockSpec((r, n), lambda i: (0, 0))
    return pl.pallas_call(
        _latent_kernel,
        grid=(t // tm,),
        in_specs=[row(D_MODEL), const(D_MODEL, LAT_COLS), const(1, Q_RANK), const(1, KV_RANK),
                  row(LANES), row(LANES)],
        out_specs=[row(D_MODEL), row(Q_RANK), row(KV_RANK), row(LANES)],
        out_shape=[jax.ShapeDtypeStruct((t, D_MODEL), BF16),
                   jax.ShapeDtypeStruct((t, Q_RANK), BF16),
                   jax.ShapeDtypeStruct((t, KV_RANK), BF16),
                   jax.ShapeDtypeStruct((t, LANES), BF16)],
        compiler_params=_params(("parallel",), 52),
        name="latent_proj",
    )(x, w_lat, q_norm, kv_norm, cos_a, sin_a)


def _qup_kernel(cq_ref, wa_ref, cos_ref, sin_ref, q_ref, *, scale):
    a = _dot(cq_ref[...], wa_ref[...])
    r = a[:, LANES:]
    q_ref[:, :LANES] = (a[:, :LANES] * scale).astype(BF16)
    q_ref[:, LANES:] = ((r * cos_ref[...] + _rot_half_64(r) * sin_ref[...]) * scale).astype(BF16)


def _q_up(cq, w_qa, cos_a, sin_a):
    t = cq.shape[0]
    tm = 1024
    scale = float((MLA_NOPE + MLA_ROPE) ** -0.5 * LOG2E)
    return pl.pallas_call(
        functools.partial(_qup_kernel, scale=scale),
        grid=(t // tm, MLA_HEADS),
        in_specs=[pl.BlockSpec((tm, Q_RANK), lambda i, j: (i, 0)),
                  pl.BlockSpec((Q_RANK, 2 * LANES), lambda i, j: (0, j)),
                  pl.BlockSpec((tm, LANES), lambda i, j: (i, 0)),
                  pl.BlockSpec((tm, LANES), lambda i, j: (i, 0))],
        out_specs=pl.BlockSpec((tm, 2 * LANES), lambda i, j: (i, j)),
        out_shape=jax.ShapeDtypeStruct((t, MLA_HEADS * 2 * LANES), BF16),
        compiler_params=_params(("parallel", "parallel"), 32),
        name="q_up",
    )(cq, w_qa, cos_a, sin_a)


def _kvup_kernel(ckv_ref, w_ref, kr_ref, k_ref, v_ref):
    acc = _dot(ckv_ref[...], w_ref[...].astype(BF16))
    k_ref[:, :LANES] = acc[:, :LANES].astype(BF16)
    k_ref[:, LANES:] = kr_ref[...]
    v_ref[...] = acc[:, LANES:].astype(BF16)


def _kv_up(ckv, w_kv_up, layer, kr):
    t = ckv.shape[0]
    tm = 1024
    return pl.pallas_call(
        _kvup_kernel,
        grid=(t // tm, MLA_HEADS),
        in_specs=[pl.BlockSpec((tm, KV_RANK), lambda i, j: (i, 0)),
                  pl.BlockSpec((None, KV_RANK, 2 * LANES), lambda i, j: (layer, 0, j)),
                  pl.BlockSpec((tm, LANES), lambda i, j: (i, 0))],
        out_specs=[pl.BlockSpec((tm, 2 * LANES), lambda i, j: (i, j)),
                   pl.BlockSpec((tm, LANES), lambda i, j: (i, j))],
        out_shape=[jax.ShapeDtypeStruct((t, MLA_HEADS * 2 * LANES), BF16),
                   jax.ShapeDtypeStruct((t, MLA_HEADS * MLA_V), BF16)],
        compiler_params=_params(("parallel", "parallel"), 32),
        name="kv_up",
    )(ckv, w_kv_up, kr)


SWA_TN = 512
SWA_Q_TILES = SWA_HEADS * SWA_DIM // SWA_TN
SWA_ROPE_TILES = (SWA_HEADS + SWA_KV_HEADS) * SWA_DIM // SWA_TN


def _swaproj_kernel(xb_ref, w_ref, cos_ref, sin_ref, o_ref, *, scale):
    j = pl.program_id(1)
    acc = _dot(xb_ref[...], w_ref[...])

    @pl.when(j < SWA_ROPE_TILES)
    def _():
        sc = jnp.where(j < SWA_Q_TILES, scale, 1.0).astype(F32)
        cos = cos_ref[...] * sc
        sin = sin_ref[...] * sc
        for c in range(SWA_TN // LANES):
            a = acc[:, c * LANES:(c + 1) * LANES]
            r = a * cos + pltpu.roll(a, SWA_DIM // 2, 1) * sin
            o_ref[:, c * LANES:(c + 1) * LANES] = r.astype(BF16)

    @pl.when(j >= SWA_ROPE_TILES)
    def _():
        o_ref[...] = acc.astype(BF16)


def _swa_proj(xb, w_sh, cos_b, sin_b):
    t = xb.shape[0]
    tm = 1024
    return pl.pallas_call(
        functools.partial(_swaproj_kernel, scale=float(SWA_DIM ** -0.5 * LOG2E)),
        grid=(t // tm, SWA_COLS // SWA_TN),
        in_specs=[pl.BlockSpec((tm, D_MODEL), lambda i, j: (i, 0)),
                  pl.BlockSpec((D_MODEL, SWA_TN), lambda i, j: (0, j)),
                  pl.BlockSpec((tm, LANES), lambda i, j: (i, 0)),
                  pl.BlockSpec((tm, LANES), lambda i, j: (i, 0))],
        out_specs=pl.BlockSpec((tm, SWA_TN), lambda i, j: (i, j)),
        out_shape=jax.ShapeDtypeStruct((t, SWA_COLS), BF16),
        compiler_params=_params(("parallel", "parallel"), 48),
        name="swa_proj",
    )(xb, w_sh, cos_b, sin_b)


MLA_HEADS_PER_STEP = 2


def _mla_kernel(q_ref, k_ref, v_ref, o_ref):
    for h in range(MLA_HEADS_PER_STEP):
        q = q_ref[:, h * 2 * LANES:(h + 1) * 2 * LANES]
        k = k_ref[:, h * 2 * LANES:(h + 1) * 2 * LANES]
        s = lax.dot_general(q, k, (((1,), (1,)), ((), ())), preferred_element_type=F32)
        m = jnp.max(s, axis=-1, keepdims=True)
        p = jnp.exp2(s - m)
        l = jnp.sum(p, axis=-1, keepdims=True)
        o = _dot(p.astype(BF16), v_ref[:, h * MLA_V:(h + 1) * MLA_V])
        o_ref[:, h * MLA_V:(h + 1) * MLA_V] = (o / l).astype(BF16)


def _mla_attention(qf, kf, v, batch, seq):
    t = qf.shape[0]
    tq = 512
    nq = seq // tq
    hp = MLA_HEADS_PER_STEP
    return pl.pallas_call(
        _mla_kernel,
        grid=(batch, MLA_HEADS // hp, nq),
        in_specs=[pl.BlockSpec((tq, hp * 2 * LANES), lambda b, h, i: (b * nq + i, h)),
                  pl.BlockSpec((seq, hp * 2 * LANES), lambda b, h, i: (b, h)),
                  pl.BlockSpec((seq, hp * MLA_V), lambda b, h, i: (b, h))],
        out_specs=pl.BlockSpec((tq, hp * MLA_V), lambda b, h, i: (b * nq + i, h)),
        out_shape=jax.ShapeDtypeStruct((t, MLA_HEADS * MLA_V), BF16),
        compiler_params=_params(("parallel", "parallel", "parallel"), 52),
        name="mla_attention",
    )(qf, kf, v)


SWA_BLOCK = 128
SWA_KEYS = 3 * SWA_BLOCK
SWA_BLOCKS_PER_STEP = 4


def _swa_kernel(q_ref, k_ref, v_ref, sink_ref, o_ref, *, seq):
    step = pl.program_id(2)
    sink = sink_ref[:, :1] * LOG2E
    shape = (SWA_GROUP * SWA_BLOCK, SWA_KEYS)
    rows = lax.broadcasted_iota(jnp.int32, shape, 0)
    cols = lax.broadcasted_iota(jnp.int32, shape, 1)
    rel = (rows & (SWA_BLOCK - 1)) - cols
    for qb in range(SWA_BLOCKS_PER_STEP):
        n = step * SWA_BLOCKS_PER_STEP + qb
        ks = jnp.clip(n * SWA_BLOCK - SWA_BLOCK, 0, seq - SWA_KEYS)
        ks = pl.multiple_of(ks, SWA_BLOCK)
        kw = k_ref[pl.ds(ks, SWA_KEYS), :]
        vw = v_ref[pl.ds(ks, SWA_KEYS), :]
        q = q_ref[qb * SWA_BLOCK:(qb + 1) * SWA_BLOCK, :]
        q4 = jnp.concatenate([q[:, g * SWA_DIM:(g + 1) * SWA_DIM] for g in range(SWA_GROUP)],
                             axis=0)
        s = lax.dot_general(q4, kw, (((1,), (1,)), ((), ())), preferred_element_type=F32)
        s = jnp.where(jnp.abs(rel + (n * SWA_BLOCK - ks)) <= WINDOW, s, -jnp.inf)
        m = jnp.maximum(jnp.max(s, axis=-1, keepdims=True), sink)
        e = jnp.exp2(s - m)
        denom = jnp.sum(e, axis=-1, keepdims=True) + jnp.exp2(sink - m)
        o = _dot(e.astype(BF16), vw) / denom
        for g in range(SWA_GROUP):
            o_ref[qb * SWA_BLOCK:(qb + 1) * SWA_BLOCK, g * SWA_DIM:(g + 1) * SWA_DIM] = (
                o[g * SWA_BLOCK:(g + 1) * SWA_BLOCK].astype(BF16))


def _swa_attention(qkv, sink_rows, batch, seq):
    t = qkv.shape[0]
    tq = SWA_BLOCK * SWA_BLOCKS_PER_STEP
    nq = seq // tq
    gw = SWA_GROUP * SWA_DIM
    k_col0 = SWA_HEADS
    v_col0 = SWA_HEADS + SWA_KV_HEADS
    return pl.pallas_call(
        functools.partial(_swa_kernel, seq=seq),
        grid=(batch, SWA_KV_HEADS, nq),
        in_specs=[pl.BlockSpec((tq, gw), lambda b, h, n: (b * nq + n, h)),
                  pl.BlockSpec((seq, SWA_DIM), lambda b, h, n: (b, k_col0 + h)),
                  pl.BlockSpec((seq, SWA_DIM), lambda b, h, n: (b, v_col0 + h)),
                  pl.BlockSpec((None, SWA_GROUP * SWA_BLOCK, LANES), lambda b, h, n: (h, 0, 0))],
        out_specs=pl.BlockSpec((tq, gw), lambda b, h, n: (b * nq + n, h)),
        out_shape=jax.ShapeDtypeStruct((t, SWA_HEADS * SWA_DIM), BF16),
        compiler_params=_params(("parallel", "parallel", "parallel"), 32),
        name="swa_attention",
    )(qkv, qkv, qkv, sink_rows)


MERGE_TN = 256
GATE_A_BLOCK0 = SWA_COLS // MERGE_TN
GATE_B_BLOCK0 = (SWA_COLS + D_MODEL) // MERGE_TN


def _merge_kernel(xb_ref, oa_ref, ob_ref, wga_ref, wgb_ref, wa_ref, wb_ref, o_ref):
    xb = xb_ref[...]
    ga = _sigmoid(_dot(xb, wga_ref[...]))
    ya = _dot(oa_ref[...], wa_ref[...].astype(BF16))
    acc = ga * ya
    gb = _sigmoid(_dot(xb, wgb_ref[...]))
    yb = _dot(ob_ref[...], wb_ref[...].astype(BF16))
    o_ref[...] = (acc + gb * yb).astype(BF16)


def _branch_merge(xb, o_a, o_b, w_sh, w_a, w_b, layer):
    t = xb.shape[0]
    tm, tn = 512, MERGE_TN
    ka = o_a.shape[1]
    row = lambda k: pl.BlockSpec((tm, k), lambda i, j: (i, 0))
    return pl.pallas_call(
        _merge_kernel,
        grid=(t // tm, D_MODEL // tn),
        in_specs=[row(D_MODEL), row(ka), row(ka),
                  pl.BlockSpec((D_MODEL, tn), lambda i, j: (0, GATE_A_BLOCK0 + j)),
                  pl.BlockSpec((D_MODEL, tn), lambda i, j: (0, GATE_B_BLOCK0 + j)),
                  pl.BlockSpec((None, ka, tn), lambda i, j: (layer, 0, j)),
                  pl.BlockSpec((None, ka, tn), lambda i, j: (layer, 0, j))],
        out_specs=pl.BlockSpec((tm, tn), lambda i, j: (i, j)),
        out_shape=jax.ShapeDtypeStruct((t, D_MODEL), BF16),
        compiler_params=_params(("parallel", "parallel"), 48),
        name="branch_merge",
    )(xb, o_a, o_b, w_sh, w_sh, w_a, w_b)


def _outproj_kernel(m_ref, w_ref, x_ref, o_ref):
    o_ref[...] = ALPHA * x_ref[...] + _dot(m_ref[...], w_ref[...].astype(BF16))


def _out_proj(merged, w_out, layer, x):
    t = x.shape[0]
    tm, tn = 1024, 512
    return pl.pallas_call(
        _outproj_kernel,
        grid=(t // tm, D_MODEL // tn),
        in_specs=[pl.BlockSpec((tm, D_MODEL), lambda i, j: (i, 0)),
                  pl.BlockSpec((None, D_MODEL, tn), lambda i, j: (layer, 0, j)),
                  pl.BlockSpec((tm, tn), lambda i, j: (i, j))],
        out_specs=pl.BlockSpec((tm, tn), lambda i, j: (i, j)),
        out_shape=jax.ShapeDtypeStruct((t, D_MODEL), F32),
        compiler_params=_params(("parallel", "parallel"), 52),
        name="out_proj",
    )(merged, w_out, x)


def _layer_norm(v, w, b):
    mu = jnp.mean(v, axis=-1, keepdims=True)
    c = v - mu
    var = jnp.mean(c * c, axis=-1, keepdims=True)
    return c * lax.rsqrt(var + LN_EPS) * w + b


def _split_bf16(v):
    hi = v.astype(BF16)
    lo = (v - hi.astype(F32)).astype(BF16)
    return hi, lo


def _ln_route_kernel(pre_ref, lnw_ref, lnb_ref, wr_ref, br_ref,
                     x1_ref, x1b_ref, x1lin_ref, route_ref, route_t_ref, cnt_ref, carry):
    i = pl.program_id(0)

    @pl.when(i == 0)
    def _():
        carry[...] = jnp.zeros_like(carry)

    x1 = _layer_norm(pre_ref[...], lnw_ref[...], lnb_ref[...])
    tm = x1.shape[0]
    x1_ref[...] = x1
    x1b_ref[...] = x1.astype(BF16)
    for s in range(ROW_CHUNKS):
        x1lin_ref[pl.ds(s, tm, stride=ROW_CHUNKS), :] = x1[:, s * LANES:(s + 1) * LANES]

    xh, xl = _split_bf16(x1)
    wh, wl = _split_bf16(wr_ref[...])
    logits = _dot(xh, wh) + (_dot(xl, wh) + _dot(xh, wl)) + br_ref[...]

    lane = lax.broadcasted_iota(jnp.int32, logits.shape, 1)
    neg = -jnp.inf
    gl = jnp.where(lane < N_GROUPS, logits, neg)
    gmax = jnp.max(gl, axis=-1, keepdims=True)
    gsum = jnp.sum(jnp.exp(gl - gmax), axis=-1, keepdims=True)
    g_w = 1.0 / gsum
    g_idx = jnp.min(jnp.where(gl == gmax, lane, ROUTE_LANES), axis=-1, keepdims=True)
    d = lane - (EXPERT_LANE0 + g_idx * EXPERTS_PER_GROUP)
    el = jnp.where((d >= 0) & (d < EXPERTS_PER_GROUP), logits, neg)
    m1 = jnp.max(el, axis=-1, keepdims=True)
    i1 = jnp.min(jnp.where(el == m1, lane, ROUTE_LANES), axis=-1, keepdims=True)
    el2 = jnp.where(lane == i1, neg, el)
    m2 = jnp.max(el2, axis=-1, keepdims=True)
    i2 = jnp.min(jnp.where(el2 == m2, lane, ROUTE_LANES), axis=-1, keepdims=True)
    e21 = jnp.exp(m2 - m1)
    w1 = g_w / (1.0 + e21)
    w2 = g_w * e21 / (1.0 + e21)

    oh1 = (lane == i1).astype(F32)
    oh2 = (lane == i2).astype(F32)
    oh = oh1 + oh2
    r_i = lax.broadcasted_iota(jnp.int32, (tm, tm), 0)
    c_i = lax.broadcasted_iota(jnp.int32, (tm, tm), 1)
    tri = jnp.where(c_i < r_i, 1.0, 0.0).astype(BF16)
    before = _dot(tri, oh.astype(BF16)) + carry[...]
    rank1 = jnp.sum(oh1 * before, axis=-1, keepdims=True)
    rank2 = jnp.sum(oh2 * before, axis=-1, keepdims=True)
    carry[...] = carry[...] + jnp.sum(oh, axis=0, keepdims=True)
    cnt_ref[...] = carry[...]

    slab = jnp.where(lane == 0, (i1 - EXPERT_LANE0).astype(F32), 0.0)
    slab = jnp.where(lane == 1, (i2 - EXPERT_LANE0).astype(F32), slab)
    slab = jnp.where(lane == 2, w1, slab)
    slab = jnp.where(lane == 3, w2, slab)
    slab = jnp.where(lane == 4, rank1, slab)
    slab = jnp.where(lane == 5, rank2, slab)
    route_ref[...] = slab
    route_t_ref[...] = jnp.transpose(slab)[:ROUTE_FIELDS, :]


def _ln_route(pre1, ln_w, ln_b, w_r, b_r):
    t = pre1.shape[0]
    tm = 256
    row = lambda n: pl.BlockSpec((tm, n), lambda i: (i, 0))
    const = lambda r, n: pl.BlockSpec((r, n), lambda i: (0, 0))
    return pl.pallas_call(
        _ln_route_kernel,
        grid=(t // tm,),
        in_specs=[row(D_MODEL), const(1, D_MODEL), const(1, D_MODEL),
                  const(D_MODEL, ROUTE_LANES), const(1, ROUTE_LANES)],
        out_specs=[row(D_MODEL), row(D_MODEL),
                   pl.BlockSpec((tm * ROW_CHUNKS, LANES), lambda i: (i, 0)),
                   row(ROUTE_LANES),
                   pl.BlockSpec((ROUTE_FIELDS, tm), lambda i: (0, i)),
                   const(1, ROUTE_LANES)],
        out_shape=[jax.ShapeDtypeStruct((t, D_MODEL), F32),
                   jax.ShapeDtypeStruct((t, D_MODEL), BF16),
                   jax.ShapeDtypeStruct((t * ROW_CHUNKS, LANES), F32),
                   jax.ShapeDtypeStruct((t, ROUTE_LANES), F32),
                   jax.ShapeDtypeStruct((ROUTE_FIELDS, t), F32),
                   jax.ShapeDtypeStruct((1, ROUTE_LANES), F32)],
        scratch_shapes=[pltpu.VMEM((1, ROUTE_LANES), F32)],
        compiler_params=_params(("arbitrary",), 52),
        name="ln1_route",
    )(pre1, ln_w, ln_b, w_r, b_r)


def _plan_kernel(e1_ref, e2_ref, r1_ref, r2_ref, cnt_ref,
                 pos1_ref, pos2_ref, src_ref, te_ref, nv_ref, tstart, *, n_tokens, n_tiles):
    def per_expert(e, first_tile):
        tstart[e] = first_tile
        c = cnt_ref[e]
        nt = lax.div(c + (MOE_TILE - 1), MOE_TILE)

        def set_te(k, carry):
            te_ref[first_tile + k] = e
            return carry

        lax.fori_loop(0, nt, set_te, 0)

        def pad_src(k, carry):
            src_ref[first_tile * MOE_TILE + k] = 0
            return carry

        lax.fori_loop(c, nt * MOE_TILE, pad_src, 0)
        return first_tile + nt

    nv = lax.fori_loop(0, N_EXPERTS, per_expert, 0)
    nv_ref[0] = nv
    last = te_ref[nv - 1]

    def tail_te(k, carry):
        te_ref[k] = last
        return carry

    lax.fori_loop(nv, n_tiles, tail_te, 0)

    def per_token(t, carry):
        p1 = tstart[e1_ref[t]] * MOE_TILE + r1_ref[t]
        p2 = tstart[e2_ref[t]] * MOE_TILE + r2_ref[t]
        pos1_ref[t] = p1
        pos2_ref[t] = p2
        src_ref[p1] = t
        src_ref[p2] = t
        return carry

    lax.fori_loop(0, n_tokens, per_token, 0)


def _dispatch_plan(e1, e2, r1, r2, cnt, n_tiles):
    t = e1.shape[0]
    smem = pl.BlockSpec(memory_space=pltpu.SMEM)
    grid_spec = pltpu.PrefetchScalarGridSpec(
        num_scalar_prefetch=5,
        grid=(1,),
        in_specs=[],
        out_specs=[smem, smem, smem, smem, smem],
        scratch_shapes=[pltpu.SMEM((N_EXPERTS,), jnp.int32)],
    )
    i32 = lambda n: jax.ShapeDtypeStruct((n,), jnp.int32)
    return pl.pallas_call(
        functools.partial(_plan_kernel, n_tokens=t, n_tiles=n_tiles),
        grid_spec=grid_spec,
        out_shape=[i32(t), i32(t), i32(n_tiles * MOE_TILE), i32(n_tiles), i32(1)],
        compiler_params=_params(("arbitrary",), 32),
        name="dispatch_plan",
    )(e1, e2, r1, r2, cnt)


def _moe_kernel(te_ref, nv_ref, src_ref, x_hbm, wg_ref, wu_ref, wd_ref, ys_ref,
                stage, xb, hg, hu, act, sem):
    i = pl.program_id(0)
    j = pl.program_id(1)
    nv = nv_ref[0]
    slot = lax.rem(i, 2)
    tile_rows = MOE_TILE * ROW_CHUNKS

    def row_copy(tile, sl, r):
        tok = src_ref[tile * MOE_TILE + r]
        return pltpu.make_async_copy(x_hbm.at[pl.ds(tok * ROW_CHUNKS, ROW_CHUNKS)],
                                     stage.at[sl, pl.ds(r * ROW_CHUNKS, ROW_CHUNKS)], sem.at[sl])

    def issue_rows(tile, sl, r0, n):
        def body(r, carry):
            row_copy(tile, sl, r).start()
            return carry

        lax.fori_loop(r0, r0 + n, body, 0)

    @pl.when(i < nv)
    def _():
        @pl.when(j == 0)
        def _():
            @pl.when(i == 0)
            def _():
                issue_rows(0, 0, 0, MOE_TILE)

            pltpu.make_async_copy(x_hbm.at[pl.ds(0, tile_rows)], stage.at[slot], sem.at[slot]).wait()
            per = MOE_KC // LANES
            for s in range(ROW_CHUNKS):
                xb[s // per, :, (s % per) * LANES:(s % per + 1) * LANES] = (
                    stage[slot, pl.ds(s, MOE_TILE, stride=ROW_CHUNKS), :].astype(BF16))

        @pl.when(i + 1 < nv)
        def _():
            issue_rows(i + 1, 1 - slot, j * MOE_ROWS_PER_STEP, MOE_ROWS_PER_STEP)

        @pl.when(j < MOE_NA)
        def _():
            xk = xb[jnp.minimum(j, MOE_NA - 1)]
            pg = _dot(xk, wg_ref[...].astype(BF16))
            pu = _dot(xk, wu_ref[...].astype(BF16))

            @pl.when(j == 0)
            def _():
                hg[...] = pg
                hu[...] = pu

            @pl.when(j > 0)
            def _():
                hg[...] += pg
                hu[...] += pu

        @pl.when(j == MOE_NA)
        def _():
            g = hg[...]
            act[...] = (g * _sigmoid(g) * hu[...]).astype(BF16)

        @pl.when(j >= MOE_NA)
        def _():
            y = _dot(act[...], wd_ref[...].astype(BF16))
            per = MOE_NC // LANES
            ys2 = ys_ref.reshape(MOE_TILE * per, LANES)
            for k in range(per):
                ys2[pl.ds(k, MOE_TILE, stride=per), :] = y[:, k * LANES:(k + 1) * LANES]


def _moe_experts(te, nv, src, x1lin, w_gate, w_up, w_down, layer, n_tiles):
    per = MOE_NC // LANES

    def up_map(i, j, te, nv, src):
        return (layer, te[i], jnp.where(i < nv[0], jnp.minimum(j, MOE_NA - 1), MOE_NA - 1), 0)

    def down_map(i, j, te, nv, src):
        return (layer, te[i], 0, jnp.where(i < nv[0], jnp.maximum(j - MOE_NA, 0), MOE_NB - 1))

    def out_map(i, j, te, nv, src):
        valid = i < nv[0]
        return (jnp.where(valid, i, nv[0] - 1),
                jnp.where(valid, jnp.maximum(j - MOE_NA, 0), MOE_NB - 1), 0)

    wshape_up = (None, None, MOE_KC, EXPERT_FF)
    grid_spec = pltpu.PrefetchScalarGridSpec(
        num_scalar_prefetch=3,
        grid=(n_tiles, MOE_STEPS),
        in_specs=[pl.BlockSpec(memory_space=pl.ANY),
                  pl.BlockSpec(wshape_up, up_map),
                  pl.BlockSpec(wshape_up, up_map),
                  pl.BlockSpec((None, None, EXPERT_FF, MOE_NC), down_map)],
        out_specs=pl.BlockSpec((MOE_TILE, per, LANES), out_map),
        scratch_shapes=[pltpu.VMEM((2, MOE_TILE * ROW_CHUNKS, LANES), F32),
                        pltpu.VMEM((MOE_NA, MOE_TILE, MOE_KC), BF16),
                        pltpu.VMEM((MOE_TILE, EXPERT_FF), F32),
                        pltpu.VMEM((MOE_TILE, EXPERT_FF), F32),
                        pltpu.VMEM((MOE_TILE, EXPERT_FF), BF16),
                        pltpu.SemaphoreType.DMA((2,))],
    )
    ys = pl.pallas_call(
        _moe_kernel,
        grid_spec=grid_spec,
        out_shape=jax.ShapeDtypeStruct((n_tiles * MOE_TILE, ROW_CHUNKS, LANES), F32),
        compiler_params=_params(("arbitrary", "arbitrary"), 56),
        name="moe_experts",
    )(te, nv, src, x1lin,
      w_gate.reshape(DEPTH, N_EXPERTS, D_MODEL, EXPERT_FF),
      w_up.reshape(DEPTH, N_EXPERTS, D_MODEL, EXPERT_FF),
      w_down.reshape(DEPTH, N_EXPERTS, EXPERT_FF, D_MODEL))
    return ys


def _ple_kernel(x1b_ref, wpg_ref, p_ref, wpu_ref, x1_ref, o_ref):
    gate = _sigmoid(_dot(x1b_ref[...], wpg_ref[...].astype(BF16)))
    up = _dot(p_ref[...].astype(BF16), wpu_ref[...].astype(BF16))
    o_ref[...] = ALPHA * x1_ref[...] + gate * up


def _ple(x1b, w_pg, p, w_pu, layer, x1):
    t = x1.shape[0]
    tm, tn = 1024, 512
    return pl.pallas_call(
        _ple_kernel,
        grid=(t // tm, D_MODEL // tn),
        in_specs=[pl.BlockSpec((tm, D_MODEL), lambda i, j: (i, 0)),
                  pl.BlockSpec((None, D_MODEL, tn), lambda i, j: (layer, 0, j)),
                  pl.BlockSpec((None, tm, PLE_DIM), lambda i, j: (layer, i, 0)),
                  pl.BlockSpec((None, PLE_DIM, tn), lambda i, j: (layer, 0, j)),
                  pl.BlockSpec((tm, tn), lambda i, j: (i, j))],
        out_specs=pl.BlockSpec((tm, tn), lambda i, j: (i, j)),
        out_shape=jax.ShapeDtypeStruct((t, D_MODEL), F32),
        compiler_params=_params(("parallel", "parallel"), 52),
        name="ple",
    )(x1b, w_pg, p, w_pu, x1)


def _final_kernel(pos1_ref, pos2_ref, pre_ref, route_ref, lnw_ref, lnb_ref, ys_hbm,
                  o_ref, stage, vbuf, sem):
    i = pl.program_id(0)
    n = pl.num_programs(0)
    tm = pre_ref.shape[0]
    slot = lax.rem(i, 2)

    def issue(tile, sl):
        base = tile * tm

        def body(r, carry):
            dst = pl.ds(r * ROW_CHUNKS, ROW_CHUNKS)
            pltpu.make_async_copy(ys_hbm.at[pl.ds(pos1_ref[base + r] * ROW_CHUNKS, ROW_CHUNKS)],
                                  stage.at[sl, 0, dst], sem.at[sl]).start()
            pltpu.make_async_copy(ys_hbm.at[pl.ds(pos2_ref[base + r] * ROW_CHUNKS, ROW_CHUNKS)],
                                  stage.at[sl, 1, dst], sem.at[sl]).start()
            return carry

        lax.fori_loop(0, tm, body, 0)

    @pl.when(i == 0)
    def _():
        issue(0, 0)

    for k in range(2):
        pltpu.make_async_copy(ys_hbm.at[pl.ds(0, tm * ROW_CHUNKS)], stage.at[slot, k],
                              sem.at[slot]).wait()

    @pl.when(i + 1 < n)
    def _():
        issue(i + 1, 1 - slot)

    route = route_ref[...]
    w1 = route[:, 2:3]
    w2 = route[:, 3:4]
    for s in range(ROW_CHUNKS):
        rows = pl.ds(s, tm, stride=ROW_CHUNKS)
        vbuf[:, s * LANES:(s + 1) * LANES] = w1 * stage[slot, 0, rows, :] + w2 * stage[slot, 1, rows, :]
    o_ref[...] = _layer_norm(pre_ref[...] + vbuf[...], lnw_ref[...], lnb_ref[...])


def _final(pos1, pos2, pre2, route, ln_w, ln_b, ys_lin):
    t, d = pre2.shape
    tm = 256
    grid_spec = pltpu.PrefetchScalarGridSpec(
        num_scalar_prefetch=2,
        grid=(t // tm,),
        in_specs=[pl.BlockSpec((tm, d), lambda i, p1, p2: (i, 0)),
                  pl.BlockSpec((tm, ROUTE_LANES), lambda i, p1, p2: (i, 0)),
                  pl.BlockSpec((1, d), lambda i, p1, p2: (0, 0)),
                  pl.BlockSpec((1, d), lambda i, p1, p2: (0, 0)),
                  pl.BlockSpec(memory_space=pl.ANY)],
        out_specs=pl.BlockSpec((tm, d), lambda i, p1, p2: (i, 0)),
        scratch_shapes=[pltpu.VMEM((2, 2, tm * ROW_CHUNKS, LANES), F32),
                        pltpu.VMEM((tm, d), F32),
                        pltpu.SemaphoreType.DMA((2,))],
    )
    return pl.pallas_call(
        _final_kernel,
        grid_spec=grid_spec,
        out_shape=jax.ShapeDtypeStruct((t, d), F32),
        compiler_params=_params(("arbitrary",), 56),
        name="combine_ln2",
    )(pos1, pos2, pre2, route, ln_w, ln_b, ys_lin)


def _prep_q_weight(w_q_up):
    r = w_q_up.shape[0]
    wq = w_q_up.reshape(r, MLA_HEADS, MLA_NOPE + MLA_ROPE)
    z = jnp.zeros((r, MLA_HEADS, LANES - MLA_ROPE), w_q_up.dtype)
    return jnp.concatenate([wq, z], axis=-1).reshape(r, MLA_HEADS * 2 * LANES).astype(BF16)


def _layer(layer, x, p, tabs, batch, seq, w_in, q_norm, kv_norm, w_q_up, w_kv_up, sink, w_branch_a,
           w_branch_b, w_out, ln1_w, ln1_b, w_group, b_group, w_expert_router, b_expert,
           w_gate, w_up, w_down, w_ple_up, w_ple_gate, ln2_w, ln2_b):
    t = x.shape[0]
    cos_a, sin_a, cos_b, sin_b = tabs

    w_lat, w_sh = _prep_input_weight(w_in, layer)
    xb, cq, ckv, kr = _latent_proj(x, w_lat, q_norm[layer][None, :], kv_norm[layer][None, :],
                                   cos_a, sin_a)
    qf = _q_up(cq, _prep_q_weight(w_q_up[layer]), cos_a, sin_a)
    kf, v_a = _kv_up(ckv, w_kv_up, layer, kr)
    o_a = _mla_attention(qf, kf, v_a, batch, seq)

    qkv_b = _swa_proj(xb, w_sh, cos_b, sin_b)
    sink_rows = jnp.broadcast_to(
        jnp.repeat(sink[layer].astype(F32).reshape(SWA_KV_HEADS, SWA_GROUP), SWA_BLOCK, axis=1)[:, :, None],
        (SWA_KV_HEADS, SWA_GROUP * SWA_BLOCK, LANES))
    o_b = _swa_attention(qkv_b, sink_rows, batch, seq)

    merged = _branch_merge(xb, o_a, o_b, w_sh, w_branch_a, w_branch_b, layer)
    pre1 = _out_proj(merged, w_out, layer, x)

    pad = ROUTE_LANES - N_GROUPS - N_EXPERTS
    w_r = jnp.concatenate([w_group[layer], w_expert_router[layer], jnp.zeros((D_MODEL, pad), F32)],
                          axis=1)
    b_r = jnp.concatenate([b_group[layer], b_expert[layer], jnp.zeros((pad,), F32)])[None, :]
    x1, x1b, x1lin, route, route_t, counts = _ln_route(pre1, ln1_w[layer][None, :],
                                                       ln1_b[layer][None, :], w_r, b_r)

    n_tiles = (2 * t) // MOE_TILE + N_EXPERTS
    ri = route_t.astype(jnp.int32)
    cnt = counts[0, EXPERT_LANE0:EXPERT_LANE0 + N_EXPERTS].astype(jnp.int32)
    pos1, pos2, src, te, nv = _dispatch_plan(ri[0], ri[1], ri[4], ri[5], cnt, n_tiles)

    ys = _moe_experts(te, nv, src, x1lin, w_gate, w_up, w_down, layer, n_tiles)
    ys_lin = ys.reshape(n_tiles * MOE_TILE * ROW_CHUNKS, LANES)

    pre2 = _ple(x1b, w_ple_gate, p, w_ple_up, layer, x1)
    return _final(pos1, pos2, pre2, route, ln2_w[layer][None, :], ln2_b[layer][None, :], ys_lin)


def kernel(x, p, positions, w_in, q_norm, kv_norm, w_q_up, w_kv_up, sink, w_branch_a, w_branch_b,
           w_out, ln1_w, ln1_b, w_group, b_group, w_expert_router, b_expert, w_gate, w_up, w_down,
           w_ple_up, w_ple_gate, ln2_w, ln2_b):
    batch, seq, d = x.shape
    t = batch * seq
    h = x.reshape(t, d)
    tabs = _rope_tables(positions.reshape(t, 1))
    p2 = p.reshape(p.shape[0], t, PLE_DIM)
    for layer in range(w_in.shape[0]):
        h = _layer(layer, h, p2, tabs, batch, seq, w_in, q_norm, kv_norm, w_q_up, w_kv_up, sink,
                   w_branch_a, w_branch_b, w_out, ln1_w, ln1_b, w_group, b_group, w_expert_router,
                   b_expert, w_gate, w_up, w_down, w_ple_up, w_ple_gate, ln2_w, ln2_b)
    return h.reshape(batch, seq, d)
```

```python
import functools
import math

import numpy as np
import jax
import jax.numpy as jnp
from jax import lax
from jax.experimental import pallas as pl
from jax.experimental.pallas import tpu as pltpu

F32 = jnp.float32
BF16 = jnp.bfloat16

D_MODEL = 4096
MLA_HEADS = 16
MLA_NOPE = 128
MLA_ROPE = 64
MLA_V = 128
Q_RANK = 768
KV_RANK = 512
SWA_HEADS = 16
SWA_KV_HEADS = 4
SWA_GROUP = SWA_HEADS // SWA_KV_HEADS
SWA_DIM = 128
WINDOW = 128
ROPE_THETA = 10000.0
N_GROUPS = 8
EXPERTS_PER_GROUP = 8
N_EXPERTS = N_GROUPS * EXPERTS_PER_GROUP
EXPERT_FF = 1024
PLE_DIM = 256
LN_EPS = 1e-5
RMS_EPS = 1e-6
DEPTH = 1
ALPHA = (2.0 * DEPTH) ** 0.25
LOG2E = math.log2(math.e)

OFF_KR = Q_RANK + KV_RANK
OFF_QB = OFF_KR + MLA_ROPE
SWA_COLS = (SWA_HEADS + 2 * SWA_KV_HEADS) * SWA_DIM
IN_WIDTH = OFF_QB + SWA_COLS + 2 * D_MODEL

LANES = 128
MIB = 1024 * 1024

ROW_CHUNKS = D_MODEL // LANES

MOE_TILE = 336
MOE_KC = 512
MOE_NA = D_MODEL // MOE_KC
MOE_NC = 1024
MOE_NB = D_MODEL // MOE_NC
MOE_STEPS = MOE_NA + MOE_NB
MOE_ROWS_PER_STEP = MOE_TILE // MOE_STEPS
ROUTE_LANES = LANES
EXPERT_LANE0 = N_GROUPS
ROUTE_FIELDS = 8


def _params(semantics, vmem_mib):
    return pltpu.CompilerParams(dimension_semantics=semantics,
                                vmem_limit_bytes=vmem_mib * MIB)


def _sigmoid(v):
    return 1.0 / (1.0 + jnp.exp(-v))


def _dot(a, b):
    return jnp.dot(a, b, preferred_element_type=F32)


SUBLANES = 8


def _transpose8(vs):
    sub = lax.broadcasted_iota(jnp.int32, (SUBLANES, LANES), 0)
    vs = list(vs)
    for d in (4, 2, 1):
        low = (sub & d) == 0
        for i in range(SUBLANES):
            if i & d:
                continue
            a, b = vs[i], vs[i + d]
            vs[i] = jnp.where(low, a, pltpu.roll(b, d, 0))
            vs[i + d] = jnp.where(low, pltpu.roll(a, SUBLANES - d, 0), b)
    return vs


def _rot_half_64(v):
    lane = lax.broadcasted_iota(jnp.int32, v.shape, 1)
    half = MLA_ROPE // 2
    fwd = pltpu.roll(v, half, 1)
    bwd = pltpu.roll(v, LANES - half, 1)
    return jnp.where(lane < half, -bwd, jnp.where(lane < MLA_ROPE, fwd, 0.0))


def _tables_kernel(pos_ref, inv_a_ref, inv_b_ref, sign_b_ref, cos_a, sin_a, cos_b, sin_b):
    pos = pos_ref[...].astype(F32)
    ang_a = pos * inv_a_ref[...]
    ang_b = pos * inv_b_ref[...]
    cos_a[...] = jnp.cos(ang_a)
    sin_a[...] = jnp.sin(ang_a)
    cos_b[...] = jnp.cos(ang_b)
    sin_b[...] = jnp.sin(ang_b) * sign_b_ref[...]


def _rope_tables(positions):
    t = positions.shape[0]
    tm = 1024
    inv64 = ROPE_THETA ** (-np.arange(0, MLA_ROPE, 2, dtype=np.float32) / MLA_ROPE)
    inv128 = ROPE_THETA ** (-np.arange(0, SWA_DIM, 2, dtype=np.float32) / SWA_DIM)
    inv_a = jnp.asarray(np.tile(inv64, 4)[None, :], F32)
    inv_b = jnp.asarray(np.tile(inv128, 2)[None, :], F32)
    sign_b = jnp.asarray(np.concatenate([-np.ones(64), np.ones(64)])[None, :], F32)
    row = pl.BlockSpec((tm, LANES), lambda i: (i, 0))
    const = pl.BlockSpec((1, LANES), lambda i: (0, 0))
    out = jax.ShapeDtypeStruct((t, LANES), F32)
    return pl.pallas_call(
        _tables_kernel,
        grid=(t // tm,),
        in_specs=[pl.BlockSpec((tm, 1), lambda i: (i, 0)), const, const, const],
        out_specs=[row, row, row, row],
        out_shape=[out, out, out, out],
        compiler_params=_params(("parallel",), 32),
        name="rope_tables",
    )(positions, inv_a, inv_b, sign_b)


LAT_COLS = OFF_KR + LANES
SHIFT_COLS = IN_WIDTH - OFF_QB
SHIFT_TN = 256


def _cast_kernel(w_ref, o_ref):
    o_ref[...] = w_ref[...].astype(BF16)


def _cast_shift_kernel(a_ref, b_ref, o_ref):
    full = jnp.concatenate([a_ref[...], b_ref[...]], axis=1)
    half = LANES // 2
    o_ref[...] = full[:, half:half + SHIFT_TN].astype(BF16)


def _prep_input_weight(w_in, layer):
    d = w_in.shape[1]
    tk = 512
    w_lat = pl.pallas_call(
        _cast_kernel,
        grid=(d // tk,),
        in_specs=[pl.BlockSpec((None, tk, LAT_COLS), lambda i: (layer, i, 0))],
        out_specs=pl.BlockSpec((tk, LAT_COLS), lambda i: (i, 0)),
        out_shape=jax.ShapeDtypeStruct((d, LAT_COLS), BF16),
        compiler_params=_params(("parallel",), 32),
        name="cast_latent_w",
    )(w_in)
    tk = 2048
    a0 = OFF_KR // SHIFT_TN
    b0 = (OFF_KR + SHIFT_TN) // LANES
    w_sh = pl.pallas_call(
        _cast_shift_kernel,
        grid=(d // tk, SHIFT_COLS // SHIFT_TN),
        in_specs=[pl.BlockSpec((None, tk, SHIFT_TN), lambda i, j: (layer, i, a0 + j)),
                  pl.BlockSpec((None, tk, LANES), lambda i, j: (layer, i, b0 + 2 * j))],
        out_specs=pl.BlockSpec((tk, SHIFT_TN), lambda i, j: (i, j)),
        out_shape=jax.ShapeDtypeStruct((d, SHIFT_COLS), BF16),
        compiler_params=_params(("parallel", "parallel"), 32),
        name="cast_shift_w",
    )(w_in, w_in)
    return w_lat, w_sh


def _latent_kernel(x_ref, w_ref, qn_ref, kvn_ref, cos_ref, sin_ref,
                   xb_ref, cq_ref, ckv_ref, kr_ref):
    xb = x_ref[...].astype(BF16)
    xb_ref[...] = xb
    acc = _dot(xb, w_ref[...])
    cq = acc[:, :Q_RANK]
    ckv = acc[:, Q_RANK:OFF_KR]
    cq = cq * lax.rsqrt(jnp.mean(cq * cq, axis=-1, keepdims=True) + RMS_EPS) * qn_ref[...]
    ckv = ckv * lax.rsqrt(jnp.mean(ckv * ckv, axis=-1, keepdims=True) + RMS_EPS) * kvn_ref[...]
    cq_ref[...] = cq.astype(BF16)
    ckv_ref[...] = ckv.astype(BF16)
    k = acc[:, OFF_KR:]
    lane = lax.broadcasted_iota(jnp.int32, k.shape, 1)
    kr = jnp.where(lane < MLA_ROPE, k * cos_ref[...], 0.0) + _rot_half_64(k) * sin_ref[...]
    kr_ref[...] = kr.astype(BF16)


def _latent_proj(x, w_lat, q_norm, kv_norm, cos_a, sin_a):
    t = x.shape[0]
    tm = 256
    row = lambda n: pl.BlockSpec((tm, n), lambda i: (i, 0))
    const = lambda r, n: pl.BlockSpec((r, n), lambda i: (0, 0))
    return pl.pallas_call(
        _latent_kernel,
        grid=(t // tm,),
        in_specs=[row(D_MODEL), const(D_MODEL, LAT_COLS), const(1, Q_RANK), const(1, KV_RANK),
                  row(LANES), row(LANES)],
        out_specs=[row(D_MODEL), row(Q_RANK), row(KV_RANK), row(LANES)],
        out_shape=[jax.ShapeDtypeStruct((t, D_MODEL), BF16),
                   jax.ShapeDtypeStruct((t, Q_RANK), BF16),
                   jax.ShapeDtypeStruct((t, KV_RANK), BF16),
                   jax.ShapeDtypeStruct((t, LANES), BF16)],
        compiler_params=_params(("parallel",), 52),
        name="latent_proj",
    )(x, w_lat, q_norm, kv_norm, cos_a, sin_a)


def _qup_kernel(cq_ref, wa_ref, cos_ref, sin_ref, q_ref, *, scale):
    a = _dot(cq_ref[...], wa_ref[...])
    r = a[:, LANES:]
    q_ref[:, :LANES] = (a[:, :LANES] * scale).astype(BF16)
    q_ref[:, LANES:] = ((r * cos_ref[...] + _rot_half_64(r) * sin_ref[...]) * scale).astype(BF16)


def _q_up(cq, w_qa, cos_a, sin_a):
    t = cq.shape[0]
    tm = 1024
    scale = float((MLA_NOPE + MLA_ROPE) ** -0.5 * LOG2E)
    return pl.pallas_call(
        functools.partial(_qup_kernel, scale=scale),
        grid=(t // tm, MLA_HEADS),
        in_specs=[pl.BlockSpec((tm, Q_RANK), lambda i, j: (i, 0)),
                  pl.BlockSpec((Q_RANK, 2 * LANES), lambda i, j: (0, j)),
                  pl.BlockSpec((tm, LANES), lambda i, j: (i, 0)),
                  pl.BlockSpec((tm, LANES), lambda i, j: (i, 0))],
        out_specs=pl.BlockSpec((tm, 2 * LANES), lambda i, j: (i, j)),
        out_shape=jax.ShapeDtypeStruct((t, MLA_HEADS * 2 * LANES), BF16),
        compiler_params=_params(("parallel", "parallel"), 32),
        name="q_up",
    )(cq, w_qa, cos_a, sin_a)


def _kvup_kernel(ckv_ref, w_ref, kr_ref, k_ref, v_ref):
    acc = _dot(ckv_ref[...], w_ref[...].astype(BF16))
    k_ref[:, :LANES] = acc[:, :LANES].astype(BF16)
    k_ref[:, LANES:] = kr_ref[...]
    v_ref[...] = acc[:, LANES:].astype(BF16)


def _kv_up(ckv, w_kv_up, layer, kr):
    t = ckv.shape[0]
    tm = 1024
    return pl.pallas_call(
        _kvup_kernel,
        grid=(t // tm, MLA_HEADS),
        in_specs=[pl.BlockSpec((tm, KV_RANK), lambda i, j: (i, 0)),
                  pl.BlockSpec((None, KV_RANK, 2 * LANES), lambda i, j: (layer, 0, j)),
                  pl.BlockSpec((tm, LANES), lambda i, j: (i, 0))],
        out_specs=[pl.BlockSpec((tm, 2 * LANES), lambda i, j: (i, j)),
                   pl.BlockSpec((tm, LANES), lambda i, j: (i, j))],
        out_shape=[jax.ShapeDtypeStruct((t, MLA_HEADS * 2 * LANES), BF16),
                   jax.ShapeDtypeStruct((t, MLA_HEADS * MLA_V), BF16)],
        compiler_params=_params(("parallel", "parallel"), 32),
        name="kv_up",
    )(ckv, w_kv_up, kr)


SWA_TN = 512
SWA_Q_TILES = SWA_HEADS * SWA_DIM // SWA_TN
SWA_ROPE_TILES = (SWA_HEADS + SWA_KV_HEADS) * SWA_DIM // SWA_TN


def _swaproj_kernel(xb_ref, w_ref, cos_ref, sin_ref, o_ref, *, scale):
    j = pl.program_id(1)
    acc = _dot(xb_ref[...], w_ref[...])

    @pl.when(j < SWA_ROPE_TILES)
    def _():
        sc = jnp.where(j < SWA_Q_TILES, scale, 1.0).astype(F32)
        cos = cos_ref[...] * sc
        sin = sin_ref[...] * sc
        for c in range(SWA_TN // LANES):
            a = acc[:, c * LANES:(c + 1) * LANES]
            r = a * cos + pltpu.roll(a, SWA_DIM // 2, 1) * sin
            o_ref[:, c * LANES:(c + 1) * LANES] = r.astype(BF16)

    @pl.when(j >= SWA_ROPE_TILES)
    def _():
        o_ref[...] = acc.astype(BF16)


def _swa_proj(xb, w_sh, cos_b, sin_b):
    t = xb.shape[0]
    tm = 1024
    return pl.pallas_call(
        functools.partial(_swaproj_kernel, scale=float(SWA_DIM ** -0.5 * LOG2E)),
        grid=(t // tm, SWA_COLS // SWA_TN),
        in_specs=[pl.BlockSpec((tm, D_MODEL), lambda i, j: (i, 0)),
                  pl.BlockSpec((D_MODEL, SWA_TN), lambda i, j: (0, j)),
                  pl.BlockSpec((tm, LANES), lambda i, j: (i, 0)),
                  pl.BlockSpec((tm, LANES), lambda i, j: (i, 0))],
        out_specs=pl.BlockSpec((tm, SWA_TN), lambda i, j: (i, j)),
        out_shape=jax.ShapeDtypeStruct((t, SWA_COLS), BF16),
        compiler_params=_params(("parallel", "parallel"), 48),
        name="swa_proj",
    )(xb, w_sh, cos_b, sin_b)


MLA_HEADS_PER_STEP = 2


def _mla_kernel(q_ref, k_ref, v_ref, o_ref):
    for h in range(MLA_HEADS_PER_STEP):
        q = q_ref[:, h * 2 * LANES:(h + 1) * 2 * LANES]
        k = k_ref[:, h * 2 * LANES:(h + 1) * 2 * LANES]
        s = lax.dot_general(q, k, (((1,), (1,)), ((), ())), preferred_element_type=F32)
        m = jnp.max(s, axis=-1, keepdims=True)
        p = jnp.exp2(s - m)
        l = jnp.sum(p, axis=-1, keepdims=True)
        o = _dot(p.astype(BF16), v_ref[:, h * MLA_V:(h + 1) * MLA_V])
        o_ref[:, h * MLA_V:(h + 1) * MLA_V] = (o / l).astype(BF16)


def _mla_attention(qf, kf, v, batch, seq):
    t = qf.shape[0]
    tq = 512
    nq = seq // tq
    hp = MLA_HEADS_PER_STEP
    return pl.pallas_call(
        _mla_kernel,
        grid=(batch, MLA_HEADS // hp, nq),
        in_specs=[pl.BlockSpec((tq, hp * 2 * LANES), lambda b, h, i: (b * nq + i, h)),
                  pl.BlockSpec((seq, hp * 2 * LANES), lambda b, h, i: (b, h)),
                  pl.BlockSpec((seq, hp * MLA_V), lambda b, h, i: (b, h))],
        out_specs=pl.BlockSpec((tq, hp * MLA_V), lambda b, h, i: (b * nq + i, h)),
        out_shape=jax.ShapeDtypeStruct((t, MLA_HEADS * MLA_V), BF16),
        compiler_params=_params(("parallel", "parallel", "parallel"), 52),
        name="mla_attention",
    )(qf, kf, v)


SWA_BLOCK = 128
SWA_KEYS = 3 * SWA_BLOCK
SWA_BLOCKS_PER_STEP = 4


def _swa_kernel(q_ref, k_ref, v_ref, sink_ref, o_ref, *, seq):
    step = pl.program_id(2)
    sink = sink_ref[:, :1] * LOG2E
    shape = (SWA_GROUP * SWA_BLOCK, SWA_KEYS)
    rows = lax.broadcasted_iota(jnp.int32, shape, 0)
    cols = lax.broadcasted_iota(jnp.int32, shape, 1)
    rel = (rows & (SWA_BLOCK - 1)) - cols
    for qb in range(SWA_BLOCKS_PER_STEP):
        n = step * SWA_BLOCKS_PER_STEP + qb
        ks = jnp.clip(n * SWA_BLOCK - SWA_BLOCK, 0, seq - SWA_KEYS)
        ks = pl.multiple_of(ks, SWA_BLOCK)
        kw = k_ref[pl.ds(ks, SWA_KEYS), :]
        vw = v_ref[pl.ds(ks, SWA_KEYS), :]
        q = q_ref[qb * SWA_BLOCK:(qb + 1) * SWA_BLOCK, :]
        q4 = jnp.concatenate([q[:, g * SWA_DIM:(g + 1) * SWA_DIM] for g in range(SWA_GROUP)],
                             axis=0)
        s = lax.dot_general(q4, kw, (((1,), (1,)), ((), ())), preferred_element_type=F32)
        s = jnp.where(jnp.abs(rel + (n * SWA_BLOCK - ks)) <= WINDOW, s, -jnp.inf)
        m = jnp.maximum(jnp.max(s, axis=-1, keepdims=True), sink)
        e = jnp.exp2(s - m)
        denom = jnp.sum(e, axis=-1, keepdims=True) + jnp.exp2(sink - m)
        o = _dot(e.astype(BF16), vw) / denom
        for g in range(SWA_GROUP):
            o_ref[qb * SWA_BLOCK:(qb + 1) * SWA_BLOCK, g * SWA_DIM:(g + 1) * SWA_DIM] = (
                o[g * SWA_BLOCK:(g + 1) * SWA_BLOCK].astype(BF16))


def _swa_attention(qkv, sink_rows, batch, seq):
    t = qkv.shape[0]
    tq = SWA_BLOCK * SWA_BLOCKS_PER_STEP
    nq = seq // tq
    gw = SWA_GROUP * SWA_DIM
    k_col0 = SWA_HEADS
    v_col0 = SWA_HEADS + SWA_KV_HEADS
    return pl.pallas_call(
        functools.partial(_swa_kernel, seq=seq),
        grid=(batch, SWA_KV_HEADS, nq),
        in_specs=[pl.BlockSpec((tq, gw), lambda b, h, n: (b * nq + n, h)),
                  pl.BlockSpec((seq, SWA_DIM), lambda b, h, n: (b, k_col0 + h)),
                  pl.BlockSpec((seq, SWA_DIM), lambda b, h, n: (b, v_col0 + h)),
                  pl.BlockSpec((None, SWA_GROUP * SWA_BLOCK, LANES), lambda b, h, n: (h, 0, 0))],
        out_specs=pl.BlockSpec((tq, gw), lambda b, h, n: (b * nq + n, h)),
        out_shape=jax.ShapeDtypeStruct((t, SWA_HEADS * SWA_DIM), BF16),
        compiler_params=_params(("parallel", "parallel", "parallel"), 32),
        name="swa_attention",
    )(qkv, qkv, qkv, sink_rows)


MERGE_TN = 256
GATE_A_BLOCK0 = SWA_COLS // MERGE_TN
GATE_B_BLOCK0 = (SWA_COLS + D_MODEL) // MERGE_TN


def _merge_kernel(xb_ref, oa_ref, ob_ref, wga_ref, wgb_ref, wa_ref, wb_ref, o_ref):
    xb = xb_ref[...]
    ga = _sigmoid(_dot(xb, wga_ref[...]))
    ya = _dot(oa_ref[...], wa_ref[...].astype(BF16))
    acc = ga * ya
    gb = _sigmoid(_dot(xb, wgb_ref[...]))
    yb = _dot(ob_ref[...], wb_ref[...].astype(BF16))
    o_ref[...] = (acc + gb * yb).astype(BF16)


def _branch_merge(xb, o_a, o_b, w_sh, w_a, w_b, layer):
    t = xb.shape[0]
    tm, tn = 512, MERGE_TN
    ka = o_a.shape[1]
    row = lambda k: pl.BlockSpec((tm, k), lambda i, j: (i, 0))
    return pl.pallas_call(
        _merge_kernel,
        grid=(t // tm, D_MODEL // tn),
        in_specs=[row(D_MODEL), row(ka), row(ka),
                  pl.BlockSpec((D_MODEL, tn), lambda i, j: (0, GATE_A_BLOCK0 + j)),
                  pl.BlockSpec((D_MODEL, tn), lambda i, j: (0, GATE_B_BLOCK0 + j)),
                  pl.BlockSpec((None, ka, tn), lambda i, j: (layer, 0, j)),
                  pl.BlockSpec((None, ka, tn), lambda i, j: (layer, 0, j))],
        out_specs=pl.BlockSpec((tm, tn), lambda i, j: (i, j)),
        out_shape=jax.ShapeDtypeStruct((t, D_MODEL), BF16),
        compiler_params=_params(("parallel", "parallel"), 48),
        name="branch_merge",
    )(xb, o_a, o_b, w_sh, w_sh, w_a, w_b)


def _outproj_kernel(m_ref, w_ref, x_ref, o_ref):
    o_ref[...] = ALPHA * x_ref[...] + _dot(m_ref[...], w_ref[...].astype(BF16))


def _out_proj(merged, w_out, layer, x):
    t = x.shape[0]
    tm, tn = 1024, 512
    return pl.pallas_call(
        _outproj_kernel,
        grid=(t // tm, D_MODEL // tn),
        in_specs=[pl.BlockSpec((tm, D_MODEL), lambda i, j: (i, 0)),
                  pl.BlockSpec((None, D_MODEL, tn), lambda i, j: (layer, 0, j)),
                  pl.BlockSpec((tm, tn), lambda i, j: (i, j))],
        out_specs=pl.BlockSpec((tm, tn), lambda i, j: (i, j)),
        out_shape=jax.ShapeDtypeStruct((t, D_MODEL), F32),
        compiler_params=_params(("parallel", "parallel"), 52),
        name="out_proj",
    )(merged, w_out, x)


def _layer_norm(v, w, b):
    mu = jnp.mean(v, axis=-1, keepdims=True)
    c = v - mu
    var = jnp.mean(c * c, axis=-1, keepdims=True)
    return c * lax.rsqrt(var + LN_EPS) * w + b


def _split_bf16(v):
    hi = v.astype(BF16)
    lo = (v - hi.astype(F32)).astype(BF16)
    return hi, lo


def _ln_route_kernel(pre_ref, lnw_ref, lnb_ref, wr_ref, br_ref,
                     x1_ref, x1b_ref, x1lin_ref, route_ref, route_t_ref, cnt_ref, carry):
    i = pl.program_id(0)

    @pl.when(i == 0)
    def _():
        carry[...] = jnp.zeros_like(carry)

    x1 = _layer_norm(pre_ref[...], lnw_ref[...], lnb_ref[...])
    tm = x1.shape[0]
    x1_ref[...] = x1
    x1b_ref[...] = x1.astype(BF16)

    def to_row_linear(a, carry):
        r0 = pl.multiple_of(a * SUBLANES, SUBLANES)
        for q in range(ROW_CHUNKS // SUBLANES):
            tiles = [x1_ref[pl.ds(r0, SUBLANES), (q * SUBLANES + k) * LANES:(q * SUBLANES + k + 1) * LANES]
                     for k in range(SUBLANES)]
            rows = _transpose8(tiles)
            for b in range(SUBLANES):
                dst = pl.multiple_of((r0 + b) * ROW_CHUNKS + q * SUBLANES, SUBLANES)
                x1lin_ref[pl.ds(dst, SUBLANES), :] = rows[b]
        return carry

    lax.fori_loop(0, tm // SUBLANES, to_row_linear, 0)

    xh, xl = _split_bf16(x1)
    wh, wl = _split_bf16(wr_ref[...])
    logits = _dot(xh, wh) + (_dot(xl, wh) + _dot(xh, wl)) + br_ref[...]

    lane = lax.broadcasted_iota(jnp.int32, logits.shape, 1)
    neg = -jnp.inf
    gl = jnp.where(lane < N_GROUPS, logits, neg)
    gmax = jnp.max(gl, axis=-1, keepdims=True)
    gsum = jnp.sum(jnp.exp(gl - gmax), axis=-1, keepdims=True)
    g_w = 1.0 / gsum
    g_idx = jnp.min(jnp.where(gl == gmax, lane, ROUTE_LANES), axis=-1, keepdims=True)
    d = lane - (EXPERT_LANE0 + g_idx * EXPERTS_PER_GROUP)
    el = jnp.where((d >= 0) & (d < EXPERTS_PER_GROUP), logits, neg)
    m1 = jnp.max(el, axis=-1, keepdims=True)
    i1 = jnp.min(jnp.where(el == m1, lane, ROUTE_LANES), axis=-1, keepdims=True)
    el2 = jnp.where(lane == i1, neg, el)
    m2 = jnp.max(el2, axis=-1, keepdims=True)
    i2 = jnp.min(jnp.where(el2 == m2, lane, ROUTE_LANES), axis=-1, keepdims=True)
    e21 = jnp.exp(m2 - m1)
    w1 = g_w / (1.0 + e21)
    w2 = g_w * e21 / (1.0 + e21)

    oh1 = (lane == i1).astype(F32)
    oh2 = (lane == i2).astype(F32)
    oh = oh1 + oh2
    r_i = lax.broadcasted_iota(jnp.int32, (tm, tm), 0)
    c_i = lax.broadcasted_iota(jnp.int32, (tm, tm), 1)
    tri = jnp.where(c_i < r_i, 1.0, 0.0).astype(BF16)
    before = _dot(tri, oh.astype(BF16)) + carry[...]
    rank1 = jnp.sum(oh1 * before, axis=-1, keepdims=True)
    rank2 = jnp.sum(oh2 * before, axis=-1, keepdims=True)
    carry[...] = carry[...] + jnp.sum(oh, axis=0, keepdims=True)
    cnt_ref[...] = carry[...]

    slab = jnp.where(lane == 0, (i1 - EXPERT_LANE0).astype(F32), 0.0)
    slab = jnp.where(lane == 1, (i2 - EXPERT_LANE0).astype(F32), slab)
    slab = jnp.where(lane == 2, w1, slab)
    slab = jnp.where(lane == 3, w2, slab)
    slab = jnp.where(lane == 4, rank1, slab)
    slab = jnp.where(lane == 5, rank2, slab)
    route_ref[...] = slab
    route_t_ref[...] = jnp.transpose(slab)[:ROUTE_FIELDS, :]


def _ln_route(pre1, ln_w, ln_b, w_r, b_r):
    t = pre1.shape[0]
    tm = 256
    row = lambda n: pl.BlockSpec((tm, n), lambda i: (i, 0))
    const = lambda r, n: pl.BlockSpec((r, n), lambda i: (0, 0))
    return pl.pallas_call(
        _ln_route_kernel,
        grid=(t // tm,),
        in_specs=[row(D_MODEL), const(1, D_MODEL), const(1, D_MODEL),
                  const(D_MODEL, ROUTE_LANES), const(1, ROUTE_LANES)],
        out_specs=[row(D_MODEL), row(D_MODEL),
                   pl.BlockSpec((tm * ROW_CHUNKS, LANES), lambda i: (i, 0)),
                   row(ROUTE_LANES),
                   pl.BlockSpec((ROUTE_FIELDS, tm), lambda i: (0, i)),
                   const(1, ROUTE_LANES)],
        out_shape=[jax.ShapeDtypeStruct((t, D_MODEL), F32),
                   jax.ShapeDtypeStruct((t, D_MODEL), BF16),
                   jax.ShapeDtypeStruct((t * ROW_CHUNKS, LANES), F32),
                   jax.ShapeDtypeStruct((t, ROUTE_LANES), F32),
                   jax.ShapeDtypeStruct((ROUTE_FIELDS, t), F32),
                   jax.ShapeDtypeStruct((1, ROUTE_LANES), F32)],
        scratch_shapes=[pltpu.VMEM((1, ROUTE_LANES), F32)],
        compiler_params=_params(("arbitrary",), 52),
        name="ln1_route",
    )(pre1, ln_w, ln_b, w_r, b_r)


def _plan_kernel(e1_ref, e2_ref, r1_ref, r2_ref, cnt_ref,
                 pos1_ref, pos2_ref, src_ref, te_ref, nv_ref, tstart, *, n_tokens, n_tiles):
    def per_expert(e, first_tile):
        tstart[e] = first_tile
        c = cnt_ref[e]
        nt = lax.div(c + (MOE_TILE - 1), MOE_TILE)

        def set_te(k, carry):
            te_ref[first_tile + k] = e
            return carry

        lax.fori_loop(0, nt, set_te, 0)

        def pad_src(k, carry):
            src_ref[first_tile * MOE_TILE + k] = 0
            return carry

        lax.fori_loop(c, nt * MOE_TILE, pad_src, 0)
        return first_tile + nt

    nv = lax.fori_loop(0, N_EXPERTS, per_expert, 0)
    nv_ref[0] = nv
    last = te_ref[nv - 1]

    def tail_te(k, carry):
        te_ref[k] = last
        return carry

    lax.fori_loop(nv, n_tiles, tail_te, 0)

    def per_token(t, carry):
        p1 = tstart[e1_ref[t]] * MOE_TILE + r1_ref[t]
        p2 = tstart[e2_ref[t]] * MOE_TILE + r2_ref[t]
        pos1_ref[t] = p1
        pos2_ref[t] = p2
        src_ref[p1] = t
        src_ref[p2] = t
        return carry

    lax.fori_loop(0, n_tokens, per_token, 0, unroll=8)


def _dispatch_plan(e1, e2, r1, r2, cnt, n_tiles):
    t = e1.shape[0]
    smem = pl.BlockSpec(memory_space=pltpu.SMEM)
    grid_spec = pltpu.PrefetchScalarGridSpec(
        num_scalar_prefetch=5,
        grid=(1,),
        in_specs=[],
        out_specs=[smem, smem, smem, smem, smem],
        scratch_shapes=[pltpu.SMEM((N_EXPERTS,), jnp.int32)],
    )
    i32 = lambda n: jax.ShapeDtypeStruct((n,), jnp.int32)
    return pl.pallas_call(
        functools.partial(_plan_kernel, n_tokens=t, n_tiles=n_tiles),
        grid_spec=grid_spec,
        out_shape=[i32(t), i32(t), i32(n_tiles * MOE_TILE), i32(n_tiles), i32(1)],
        compiler_params=_params(("arbitrary",), 32),
        name="dispatch_plan",
    )(e1, e2, r1, r2, cnt)


def _moe_kernel(te_ref, nv_ref, src_ref, x_hbm, wg_ref, wu_ref, wd_ref, ys_ref,
                stage, xb, hg, hu, act, sem):
    i = pl.program_id(0)
    j = pl.program_id(1)
    nv = nv_ref[0]
    slot = lax.rem(i, 2)
    tile_rows = MOE_TILE * ROW_CHUNKS

    def row_copy(tile, sl, r):
        tok = src_ref[tile * MOE_TILE + r]
        return pltpu.make_async_copy(x_hbm.at[pl.ds(tok * ROW_CHUNKS, ROW_CHUNKS)],
                                     stage.at[sl, pl.ds(r * ROW_CHUNKS, ROW_CHUNKS)], sem.at[sl])

    def issue_rows(tile, sl, r0, n):
        def body(r, carry):
            row_copy(tile, sl, r).start()
            return carry

        lax.fori_loop(r0, r0 + n, body, 0)

    @pl.when(i < nv)
    def _():
        @pl.when(j == 0)
        def _():
            @pl.when(i == 0)
            def _():
                issue_rows(0, 0, 0, MOE_TILE)

            pltpu.make_async_copy(x_hbm.at[pl.ds(0, tile_rows)], stage.at[slot], sem.at[slot]).wait()
            per = MOE_KC // LANES
            pair = 2 * SUBLANES

            def to_tiled(a, carry):
                t0 = pl.multiple_of(a * pair, pair)
                for q in range(ROW_CHUNKS // SUBLANES):
                    halves = []
                    for h in range(2):
                        rows = [stage[slot, pl.ds(pl.multiple_of(
                            (t0 + h * SUBLANES + b) * ROW_CHUNKS + q * SUBLANES, SUBLANES), SUBLANES), :]
                            for b in range(SUBLANES)]
                        halves.append(_transpose8(rows))
                    for k in range(SUBLANES):
                        s = q * SUBLANES + k
                        blk = jnp.concatenate([halves[0][k], halves[1][k]], axis=0).astype(BF16)
                        xb[s // per, pl.ds(t0, pair), (s % per) * LANES:(s % per + 1) * LANES] = blk
                return carry

            lax.fori_loop(0, MOE_TILE // pair, to_tiled, 0)

        @pl.when(i + 1 < nv)
        def _():
            issue_rows(i + 1, 1 - slot, j * MOE_ROWS_PER_STEP, MOE_ROWS_PER_STEP)

        @pl.when(j < MOE_NA)
        def _():
            xk = xb[jnp.minimum(j, MOE_NA - 1)]
            pg = _dot(xk, wg_ref[...].astype(BF16))
            pu = _dot(xk, wu_ref[...].astype(BF16))

            @pl.when(j == 0)
            def _():
                hg[...] = pg
                hu[...] = pu

            @pl.when(j > 0)
            def _():
                hg[...] += pg
                hu[...] += pu

        @pl.when(j == MOE_NA)
        def _():
            g = hg[...]
            act[...] = (g * _sigmoid(g) * hu[...]).astype(BF16)

        @pl.when(j >= MOE_NA)
        def _():
            y = _dot(act[...], wd_ref[...].astype(BF16))
            per = MOE_NC // LANES
            ys2 = ys_ref.reshape(MOE_TILE * per, LANES)
            for a in range(MOE_TILE // SUBLANES):
                tiles = [y[a * SUBLANES:(a + 1) * SUBLANES, k * LANES:(k + 1) * LANES]
                         for k in range(per)]
                rows = _transpose8(tiles)
                for b in range(SUBLANES):
                    ys2[(a * SUBLANES + b) * per:(a * SUBLANES + b + 1) * per, :] = rows[b]


def _moe_experts(te, nv, src, x1lin, w_gate, w_up, w_down, layer, n_tiles):
    per = MOE_NC // LANES

    def up_map(i, j, te, nv, src):
        return (layer, te[i], jnp.where(i < nv[0], jnp.minimum(j, MOE_NA - 1), MOE_NA - 1), 0)

    def down_map(i, j, te, nv, src):
        return (layer, te[i], 0, jnp.where(i < nv[0], jnp.maximum(j - MOE_NA, 0), MOE_NB - 1))

    def out_map(i, j, te, nv, src):
        valid = i < nv[0]
        return (jnp.where(valid, i, nv[0] - 1),
                jnp.where(valid, jnp.maximum(j - MOE_NA, 0), MOE_NB - 1), 0)

    wshape_up = (None, None, MOE_KC, EXPERT_FF)
    grid_spec = pltpu.PrefetchScalarGridSpec(
        num_scalar_prefetch=3,
        grid=(n_tiles, MOE_STEPS),
        in_specs=[pl.BlockSpec(memory_space=pl.ANY),
                  pl.BlockSpec(wshape_up, up_map),
                  pl.BlockSpec(wshape_up, up_map),
                  pl.BlockSpec((None, None, EXPERT_FF, MOE_NC), down_map)],
        out_specs=pl.BlockSpec((MOE_TILE, per, LANES), out_map),
        scratch_shapes=[pltpu.VMEM((2, MOE_TILE * ROW_CHUNKS, LANES), F32),
                        pltpu.VMEM((MOE_NA, MOE_TILE, MOE_KC), BF16),
                        pltpu.VMEM((MOE_TILE, EXPERT_FF), F32),
                        pltpu.VMEM((MOE_TILE, EXPERT_FF), F32),
                        pltpu.VMEM((MOE_TILE, EXPERT_FF), BF16),
                        pltpu.SemaphoreType.DMA((2,))],
    )
    ys = pl.pallas_call(
        _moe_kernel,
        grid_spec=grid_spec,
        out_shape=jax.ShapeDtypeStruct((n_tiles * MOE_TILE, ROW_CHUNKS, LANES), F32),
        compiler_params=_params(("arbitrary", "arbitrary"), 56),
        name="moe_experts",
    )(te, nv, src, x1lin,
      w_gate.reshape(DEPTH, N_EXPERTS, D_MODEL, EXPERT_FF),
      w_up.reshape(DEPTH, N_EXPERTS, D_MODEL, EXPERT_FF),
      w_down.reshape(DEPTH, N_EXPERTS, EXPERT_FF, D_MODEL))
    return ys


def _ple_kernel(x1b_ref, wpg_ref, p_ref, wpu_ref, x1_ref, o_ref):
    gate = _sigmoid(_dot(x1b_ref[...], wpg_ref[...].astype(BF16)))
    up = _dot(p_ref[...].astype(BF16), wpu_ref[...].astype(BF16))
    o_ref[...] = ALPHA * x1_ref[...] + gate * up


def _ple(x1b, w_pg, p, w_pu, layer, x1):
    t = x1.shape[0]
    tm, tn = 1024, 512
    return pl.pallas_call(
        _ple_kernel,
        grid=(t // tm, D_MODEL // tn),
        in_specs=[pl.BlockSpec((tm, D_MODEL), lambda i, j: (i, 0)),
                  pl.BlockSpec((None, D_MODEL, tn), lambda i, j: (layer, 0, j)),
                  pl.BlockSpec((None, tm, PLE_DIM), lambda i, j: (layer, i, 0)),
                  pl.BlockSpec((None, PLE_DIM, tn), lambda i, j: (layer, 0, j)),
                  pl.BlockSpec((tm, tn), lambda i, j: (i, j))],
        out_specs=pl.BlockSpec((tm, tn), lambda i, j: (i, j)),
        out_shape=jax.ShapeDtypeStruct((t, D_MODEL), F32),
        compiler_params=_params(("parallel", "parallel"), 52),
        name="ple",
    )(x1b, w_pg, p, w_pu, x1)


def _final_kernel(pos1_ref, pos2_ref, pre_ref, route_ref, lnw_ref, lnb_ref, ys_hbm,
                  o_ref, stage, vbuf, sem):
    i = pl.program_id(0)
    n = pl.num_programs(0)
    tm = pre_ref.shape[0]
    slot = lax.rem(i, 2)

    def issue(tile, sl):
        base = tile * tm

        def body(r, carry):
            dst = pl.ds(r * ROW_CHUNKS, ROW_CHUNKS)
            pltpu.make_async_copy(ys_hbm.at[pl.ds(pos1_ref[base + r] * ROW_CHUNKS, ROW_CHUNKS)],
                                  stage.at[sl, 0, dst], sem.at[sl]).start()
            pltpu.make_async_copy(ys_hbm.at[pl.ds(pos2_ref[base + r] * ROW_CHUNKS, ROW_CHUNKS)],
                                  stage.at[sl, 1, dst], sem.at[sl]).start()
            return carry

        lax.fori_loop(0, tm, body, 0)

    @pl.when(i == 0)
    def _():
        issue(0, 0)

    for k in range(2):
        pltpu.make_async_copy(ys_hbm.at[pl.ds(0, tm * ROW_CHUNKS)], stage.at[slot, k],
                              sem.at[slot]).wait()

    @pl.when(i + 1 < n)
    def _():
        issue(i + 1, 1 - slot)

    def combine(a, carry):
        r0 = pl.multiple_of(a * SUBLANES, SUBLANES)
        route = route_ref[pl.ds(r0, SUBLANES), :]
        w1 = jnp.broadcast_to(route[:, 2:3], (SUBLANES, LANES))
        w2 = jnp.broadcast_to(route[:, 3:4], (SUBLANES, LANES))
        for q in range(ROW_CHUNKS // SUBLANES):
            tiles = []
            for slot_half in range(2):
                rows = [stage[slot, slot_half, pl.ds(pl.multiple_of(
                    (r0 + b) * ROW_CHUNKS + q * SUBLANES, SUBLANES), SUBLANES), :]
                    for b in range(SUBLANES)]
                tiles.append(_transpose8(rows))
            for k in range(SUBLANES):
                s = q * SUBLANES + k
                vbuf[pl.ds(r0, SUBLANES), s * LANES:(s + 1) * LANES] = w1 * tiles[0][k] + w2 * tiles[1][k]
        return carry

    lax.fori_loop(0, tm // SUBLANES, combine, 0)
    o_ref[...] = _layer_norm(pre_ref[...] + vbuf[...], lnw_ref[...], lnb_ref[...])


def _final(pos1, pos2, pre2, route, ln_w, ln_b, ys_lin):
    t, d = pre2.shape
    tm = 256
    grid_spec = pltpu.PrefetchScalarGridSpec(
        num_scalar_prefetch=2,
        grid=(t // tm,),
        in_specs=[pl.BlockSpec((tm, d), lambda i, p1, p2: (i, 0)),
                  pl.BlockSpec((tm, ROUTE_LANES), lambda i, p1, p2: (i, 0)),
                  pl.BlockSpec((1, d), lambda i, p1, p2: (0, 0)),
                  pl.BlockSpec((1, d), lambda i, p1, p2: (0, 0)),
                  pl.BlockSpec(memory_space=pl.ANY)],
        out_specs=pl.BlockSpec((tm, d), lambda i, p1, p2: (i, 0)),
        scratch_shapes=[pltpu.VMEM((2, 2, tm * ROW_CHUNKS, LANES), F32),
                        pltpu.VMEM((tm, d), F32),
                        pltpu.SemaphoreType.DMA((2,))],
    )
    return pl.pallas_call(
        _final_kernel,
        grid_spec=grid_spec,
        out_shape=jax.ShapeDtypeStruct((t, d), F32),
        compiler_params=_params(("arbitrary",), 56),
        name="combine_ln2",
    )(pos1, pos2, pre2, route, ln_w, ln_b, ys_lin)


def _prep_q_weight(w_q_up):
    r = w_q_up.shape[0]
    wq = w_q_up.reshape(r, MLA_HEADS, MLA_NOPE + MLA_ROPE)
    z = jnp.zeros((r, MLA_HEADS, LANES - MLA_ROPE), w_q_up.dtype)
    return jnp.concatenate([wq, z], axis=-1).reshape(r, MLA_HEADS * 2 * LANES).astype(BF16)


def _layer(layer, x, p, tabs, batch, seq, w_in, q_norm, kv_norm, w_q_up, w_kv_up, sink, w_branch_a,
           w_branch_b, w_out, ln1_w, ln1_b, w_group, b_group, w_expert_router, b_expert,
           w_gate, w_up, w_down, w_ple_up, w_ple_gate, ln2_w, ln2_b):
    t = x.shape[0]
    cos_a, sin_a, cos_b, sin_b = tabs

    w_lat, w_sh = _prep_input_weight(w_in, layer)
    xb, cq, ckv, kr = _latent_proj(x, w_lat, q_norm[layer][None, :], kv_norm[layer][None, :],
                                   cos_a, sin_a)
    qf = _q_up(cq, _prep_q_weight(w_q_up[layer]), cos_a, sin_a)
    kf, v_a = _kv_up(ckv, w_kv_up, layer, kr)
    o_a = _mla_attention(qf, kf, v_a, batch, seq)

    qkv_b = _swa_proj(xb, w_sh, cos_b, sin_b)
    sink_rows = jnp.broadcast_to(
        jnp.repeat(sink[layer].astype(F32).reshape(SWA_KV_HEADS, SWA_GROUP), SWA_BLOCK, axis=1)[:, :, None],
        (SWA_KV_HEADS, SWA_GROUP * SWA_BLOCK, LANES))
    o_b = _swa_attention(qkv_b, sink_rows, batch, seq)

    merged = _branch_merge(xb, o_a, o_b, w_sh, w_branch_a, w_branch_b, layer)
    pre1 = _out_proj(merged, w_out, layer, x)

    pad = ROUTE_LANES - N_GROUPS - N_EXPERTS
    w_r = jnp.concatenate([w_group[layer], w_expert_router[layer], jnp.zeros((D_MODEL, pad), F32)],
                          axis=1)
    b_r = jnp.concatenate([b_group[layer], b_expert[layer], jnp.zeros((pad,), F32)])[None, :]
    x1, x1b, x1lin, route, route_t, counts = _ln_route(pre1, ln1_w[layer][None, :],
                                                       ln1_b[layer][None, :], w_r, b_r)

    n_tiles = (2 * t) // MOE_TILE + N_EXPERTS
    ri = route_t.astype(jnp.int32)
    cnt = counts[0, EXPERT_LANE0:EXPERT_LANE0 + N_EXPERTS].astype(jnp.int32)
    pos1, pos2, src, te, nv = _dispatch_plan(ri[0], ri[1], ri[4], ri[5], cnt, n_tiles)

    ys = _moe_experts(te, nv, src, x1lin, w_gate, w_up, w_down, layer, n_tiles)
    ys_lin = ys.reshape(n_tiles * MOE_TILE * ROW_CHUNKS, LANES)

    pre2 = _ple(x1b, w_ple_gate, p, w_ple_up, layer, x1)
    return _final(pos1, pos2, pre2, route, ln2_w[layer][None, :], ln2_b[layer][None, :], ys_lin)


def kernel(x, p, positions, w_in, q_norm, kv_norm, w_q_up, w_kv_up, sink, w_branch_a, w_branch_b,
           w_out, ln1_w, ln1_b, w_group, b_group, w_expert_router, b_expert, w_gate, w_up, w_down,
           w_ple_up, w_ple_gate, ln2_w, ln2_b):
    batch, seq, d = x.shape
    t = batch * seq
    h = x.reshape(t, d)
    tabs = _rope_tables(positions.reshape(t, 1))
    p2 = p.reshape(p.shape[0], t, PLE_DIM)
    for layer in range(w_in.shape[0]):
        h = _layer(layer, h, p2, tabs, batch, seq, w_in, q_norm, kv_norm, w_q_up, w_kv_up, sink,
                   w_branch_a, w_branch_b, w_out, ln1_w, ln1_b, w_group, b_group, w_expert_router,
                   b_expert, w_gate, w_up, w_down, w_ple_up, w_ple_gate, ln2_w, ln2_b)
    return h.reshape(batch, seq, d)
```

```python
import functools
import math

import numpy as np
import jax
import jax.numpy as jnp
from jax import lax
from jax.experimental import pallas as pl
from jax.experimental.pallas import tpu as pltpu

F32 = jnp.float32
BF16 = jnp.bfloat16

D_MODEL = 4096
MLA_HEADS = 16
MLA_NOPE = 128
MLA_ROPE = 64
MLA_V = 128
Q_RANK = 768
KV_RANK = 512
SWA_HEADS = 16
SWA_KV_HEADS = 4
SWA_GROUP = SWA_HEADS // SWA_KV_HEADS
SWA_DIM = 128
WINDOW = 128
ROPE_THETA = 10000.0
N_GROUPS = 8
EXPERTS_PER_GROUP = 8
N_EXPERTS = N_GROUPS * EXPERTS_PER_GROUP
EXPERT_FF = 1024
PLE_DIM = 256
LN_EPS = 1e-5
RMS_EPS = 1e-6
DEPTH = 1
ALPHA = (2.0 * DEPTH) ** 0.25
LOG2E = math.log2(math.e)

OFF_KR = Q_RANK + KV_RANK
OFF_QB = OFF_KR + MLA_ROPE
SWA_COLS = (SWA_HEADS + 2 * SWA_KV_HEADS) * SWA_DIM
IN_WIDTH = OFF_QB + SWA_COLS + 2 * D_MODEL

LANES = 128
MIB = 1024 * 1024

ROW_CHUNKS = D_MODEL // LANES

MOE_TILE = 336
MOE_KC = 512
MOE_NA = D_MODEL // MOE_KC
MOE_NC = 1024
MOE_NB = D_MODEL // MOE_NC
MOE_STEPS = MOE_NA + MOE_NB
MOE_ROWS_PER_STEP = MOE_TILE // MOE_STEPS
MOE_RING = 3
ROUTE_LANES = LANES
EXPERT_LANE0 = N_GROUPS
ROUTE_FIELDS = 8


def _params(semantics, vmem_mib):
    return pltpu.CompilerParams(dimension_semantics=semantics,
                                vmem_limit_bytes=vmem_mib * MIB)


def _sigmoid(v):
    return 1.0 / (1.0 + jnp.exp(-v))


def _dot(a, b):
    return jnp.dot(a, b, preferred_element_type=F32)


SUBLANES = 8


def _transpose8(vs):
    sub = lax.broadcasted_iota(jnp.int32, (SUBLANES, LANES), 0)
    vs = list(vs)
    for d in (4, 2, 1):
        low = (sub & d) == 0
        for i in range(SUBLANES):
            if i & d:
                continue
            a, b = vs[i], vs[i + d]
            vs[i] = jnp.where(low, a, pltpu.roll(b, d, 0))
            vs[i + d] = jnp.where(low, pltpu.roll(a, SUBLANES - d, 0), b)
    return vs


def _rot_half_64(v):
    lane = lax.broadcasted_iota(jnp.int32, v.shape, 1)
    half = MLA_ROPE // 2
    fwd = pltpu.roll(v, half, 1)
    bwd = pltpu.roll(v, LANES - half, 1)
    return jnp.where(lane < half, -bwd, jnp.where(lane < MLA_ROPE, fwd, 0.0))


def _tables_kernel(pos_ref, inv_a_ref, inv_b_ref, sign_b_ref, cos_a, sin_a, cos_b, sin_b):
    pos = pos_ref[...].astype(F32)
    ang_a = pos * inv_a_ref[...]
    ang_b = pos * inv_b_ref[...]
    cos_a[...] = jnp.cos(ang_a)
    sin_a[...] = jnp.sin(ang_a)
    cos_b[...] = jnp.cos(ang_b)
    sin_b[...] = jnp.sin(ang_b) * sign_b_ref[...]


def _rope_tables(positions):
    t = positions.shape[0]
    tm = 1024
    inv64 = ROPE_THETA ** (-np.arange(0, MLA_ROPE, 2, dtype=np.float32) / MLA_ROPE)
    inv128 = ROPE_THETA ** (-np.arange(0, SWA_DIM, 2, dtype=np.float32) / SWA_DIM)
    inv_a = jnp.asarray(np.tile(inv64, 4)[None, :], F32)
    inv_b = jnp.asarray(np.tile(inv128, 2)[None, :], F32)
    sign_b = jnp.asarray(np.concatenate([-np.ones(64), np.ones(64)])[None, :], F32)
    row = pl.BlockSpec((tm, LANES), lambda i: (i, 0))
    const = pl.BlockSpec((1, LANES), lambda i: (0, 0))
    out = jax.ShapeDtypeStruct((t, LANES), F32)
    return pl.pallas_call(
        _tables_kernel,
        grid=(t // tm,),
        in_specs=[pl.BlockSpec((tm, 1), lambda i: (i, 0)), const, const, const],
        out_specs=[row, row, row, row],
        out_shape=[out, out, out, out],
        compiler_params=_params(("parallel",), 32),
        name="rope_tables",
    )(positions, inv_a, inv_b, sign_b)


LAT_COLS = OFF_KR + LANES
SHIFT_COLS = IN_WIDTH - OFF_QB
CAST_TILE = 1024


def _cast_transpose_kernel(w_ref, o_ref):
    o_ref[...] = jnp.transpose(w_ref[...]).astype(BF16)


def _prep_input_weight(w_in, layer):
    d = w_in.shape[1]
    w_t = jnp.swapaxes(w_in, 1, 2)
    tk = 512
    w_lat = pl.pallas_call(
        _cast_transpose_kernel,
        grid=(d // tk,),
        in_specs=[pl.BlockSpec((None, LAT_COLS, tk), lambda i: (layer, 0, i))],
        out_specs=pl.BlockSpec((tk, LAT_COLS), lambda i: (i, 0)),
        out_shape=jax.ShapeDtypeStruct((d, LAT_COLS), BF16),
        compiler_params=_params(("parallel",), 32),
        name="cast_latent_w",
    )(w_t)
    w_sh = pl.pallas_call(
        _cast_transpose_kernel,
        grid=(d // CAST_TILE, SHIFT_COLS // CAST_TILE),
        in_specs=[pl.BlockSpec((pl.Element(CAST_TILE), pl.Element(CAST_TILE)),
                               lambda i, j: (pl.multiple_of(OFF_QB + CAST_TILE * j, LANES // 2),
                                             pl.multiple_of(CAST_TILE * i, CAST_TILE)))],
        out_specs=pl.BlockSpec((CAST_TILE, CAST_TILE), lambda i, j: (i, j)),
        out_shape=jax.ShapeDtypeStruct((d, SHIFT_COLS), BF16),
        compiler_params=_params(("parallel", "parallel"), 32),
        name="cast_shift_w",
    )(w_t[layer])
    return w_lat, w_sh


def _latent_kernel(x_ref, w_ref, qn_ref, kvn_ref, cos_ref, sin_ref,
                   xb_ref, cq_ref, ckv_ref, kr_ref):
    xb = x_ref[...].astype(BF16)
    xb_ref[...] = xb
    acc = _dot(xb, w_ref[...])
    cq = acc[:, :Q_RANK]
    ckv = acc[:, Q_RANK:OFF_KR]
    cq = cq * lax.rsqrt(jnp.mean(cq * cq, axis=-1, keepdims=True) + RMS_EPS) * qn_ref[...]
    ckv = ckv * lax.rsqrt(jnp.mean(ckv * ckv, axis=-1, keepdims=True) + RMS_EPS) * kvn_ref[...]
    cq_ref[...] = cq.astype(BF16)
    ckv_ref[...] = ckv.astype(BF16)
    k = acc[:, OFF_KR:]
    lane = lax.broadcasted_iota(jnp.int32, k.shape, 1)
    kr = jnp.where(lane < MLA_ROPE, k * cos_ref[...], 0.0) + _rot_half_64(k) * sin_ref[...]
    kr_ref[...] = kr.astype(BF16)


def _latent_proj(x, w_lat, q_norm, kv_norm, cos_a, sin_a):
    t = x.shape[0]
    tm = 256
    row = lambda n: pl.BlockSpec((tm, n), lambda i: (i, 0))
    const = lambda r, n: pl.BlockSpec((r, n), lambda i: (0, 0))
    return pl.pallas_call(
        _latent_kernel,
        grid=(t // tm,),
        in_specs=[row(D_MODEL), const(D_MODEL, LAT_COLS), const(1, Q_RANK), const(1, KV_RANK),
                  row(LANES), row(LANES)],
        out_specs=[row(D_MODEL), row(Q_RANK), row(KV_RANK), row(LANES)],
        out_shape=[jax.ShapeDtypeStruct((t, D_MODEL), BF16),
                   jax.ShapeDtypeStruct((t, Q_RANK), BF16),
                   jax.ShapeDtypeStruct((t, KV_RANK), BF16),
                   jax.ShapeDtypeStruct((t, LANES), BF16)],
        compiler_params=_params(("parallel",), 52),
        name="latent_proj",
    )(x, w_lat, q_norm, kv_norm, cos_a, sin_a)


UP_HEADS_PER_STEP = 4


def _qup_kernel(cq_ref, wa_ref, cos_ref, sin_ref, q_ref, *, scale):
    a = _dot(cq_ref[...], wa_ref[...])
    cos = cos_ref[...] * scale
    sin = sin_ref[...] * scale
    for h in range(UP_HEADS_PER_STEP):
        c0 = h * 2 * LANES
        r = a[:, c0 + LANES:c0 + 2 * LANES]
        q_ref[:, c0:c0 + LANES] = (a[:, c0:c0 + LANES] * scale).astype(BF16)
        q_ref[:, c0 + LANES:c0 + 2 * LANES] = (r * cos + _rot_half_64(r) * sin).astype(BF16)


def _q_up(cq, w_qa, cos_a, sin_a):
    t = cq.shape[0]
    tm = 1024
    tn = UP_HEADS_PER_STEP * 2 * LANES
    scale = float((MLA_NOPE + MLA_ROPE) ** -0.5 * LOG2E)
    return pl.pallas_call(
        functools.partial(_qup_kernel, scale=scale),
        grid=(t // tm, MLA_HEADS // UP_HEADS_PER_STEP),
        in_specs=[pl.BlockSpec((tm, Q_RANK), lambda i, j: (i, 0)),
                  pl.BlockSpec((Q_RANK, tn), lambda i, j: (0, j)),
                  pl.BlockSpec((tm, LANES), lambda i, j: (i, 0)),
                  pl.BlockSpec((tm, LANES), lambda i, j: (i, 0))],
        out_specs=pl.BlockSpec((tm, tn), lambda i, j: (i, j)),
        out_shape=jax.ShapeDtypeStruct((t, MLA_HEADS * 2 * LANES), BF16),
        compiler_params=_params(("parallel", "parallel"), 32),
        name="q_up",
    )(cq, w_qa, cos_a, sin_a)


def _kvup_kernel(ckv_ref, w_ref, kr_ref, k_ref, v_ref):
    acc = _dot(ckv_ref[...], w_ref[...].astype(BF16))
    kr = kr_ref[...]
    for h in range(UP_HEADS_PER_STEP):
        c0 = h * 2 * LANES
        k_ref[:, c0:c0 + LANES] = acc[:, c0:c0 + LANES].astype(BF16)
        k_ref[:, c0 + LANES:c0 + 2 * LANES] = kr
        v_ref[:, h * MLA_V:(h + 1) * MLA_V] = acc[:, c0 + LANES:c0 + 2 * LANES].astype(BF16)


def _kv_up(ckv, w_kv_up, layer, kr):
    t = ckv.shape[0]
    tm = 1024
    tn = UP_HEADS_PER_STEP * 2 * LANES
    return pl.pallas_call(
        _kvup_kernel,
        grid=(t // tm, MLA_HEADS // UP_HEADS_PER_STEP),
        in_specs=[pl.BlockSpec((tm, KV_RANK), lambda i, j: (i, 0)),
                  pl.BlockSpec((None, KV_RANK, tn), lambda i, j: (layer, 0, j)),
                  pl.BlockSpec((tm, LANES), lambda i, j: (i, 0))],
        out_specs=[pl.BlockSpec((tm, tn), lambda i, j: (i, j)),
                   pl.BlockSpec((tm, UP_HEADS_PER_STEP * MLA_V), lambda i, j: (i, j))],
        out_shape=[jax.ShapeDtypeStruct((t, MLA_HEADS * 2 * LANES), BF16),
                   jax.ShapeDtypeStruct((t, MLA_HEADS * MLA_V), BF16)],
        compiler_params=_params(("parallel", "parallel"), 32),
        name="kv_up",
    )(ckv, w_kv_up, kr)


SWA_TN = 512
SWA_Q_TILES = SWA_HEADS * SWA_DIM // SWA_TN
SWA_ROPE_TILES = (SWA_HEADS + SWA_KV_HEADS) * SWA_DIM // SWA_TN


def _swaproj_kernel(xb_ref, w_ref, cos_ref, sin_ref, o_ref, *, scale):
    j = pl.program_id(1)
    acc = _dot(xb_ref[...], w_ref[...])

    @pl.when(j < SWA_ROPE_TILES)
    def _():
        sc = jnp.where(j < SWA_Q_TILES, scale, 1.0).astype(F32)
        cos = cos_ref[...] * sc
        sin = sin_ref[...] * sc
        for c in range(SWA_TN // LANES):
            a = acc[:, c * LANES:(c + 1) * LANES]
            r = a * cos + pltpu.roll(a, SWA_DIM // 2, 1) * sin
            o_ref[:, c * LANES:(c + 1) * LANES] = r.astype(BF16)

    @pl.when(j >= SWA_ROPE_TILES)
    def _():
        o_ref[...] = acc.astype(BF16)


def _swa_proj(xb, w_sh, cos_b, sin_b):
    t = xb.shape[0]
    tm = 1024
    return pl.pallas_call(
        functools.partial(_swaproj_kernel, scale=float(SWA_DIM ** -0.5 * LOG2E)),
        grid=(t // tm, SWA_COLS // SWA_TN),
        in_specs=[pl.BlockSpec((tm, D_MODEL), lambda i, j: (i, 0)),
                  pl.BlockSpec((D_MODEL, SWA_TN), lambda i, j: (0, j)),
                  pl.BlockSpec((tm, LANES), lambda i, j: (i, 0)),
                  pl.BlockSpec((tm, LANES), lambda i, j: (i, 0))],
        out_specs=pl.BlockSpec((tm, SWA_TN), lambda i, j: (i, j)),
        out_shape=jax.ShapeDtypeStruct((t, SWA_COLS), BF16),
        compiler_params=_params(("parallel", "parallel"), 48),
        name="swa_proj",
    )(xb, w_sh, cos_b, sin_b)


MLA_HEADS_PER_STEP = 2


def _mla_kernel(q_ref, k_ref, v_ref, o_ref):
    for h in range(MLA_HEADS_PER_STEP):
        q = q_ref[:, h * 2 * LANES:(h + 1) * 2 * LANES]
        k = k_ref[:, h * 2 * LANES:(h + 1) * 2 * LANES]
        s = lax.dot_general(q, k, (((1,), (1,)), ((), ())), preferred_element_type=F32)
        m = jnp.max(s, axis=-1, keepdims=True)
        p = jnp.exp2(s - m)
        l = jnp.sum(p, axis=-1, keepdims=True)
        o = _dot(p.astype(BF16), v_ref[:, h * MLA_V:(h + 1) * MLA_V])
        o_ref[:, h * MLA_V:(h + 1) * MLA_V] = (o / l).astype(BF16)


def _mla_attention(qf, kf, v, batch, seq):
    t = qf.shape[0]
    tq = 512
    nq = seq // tq
    hp = MLA_HEADS_PER_STEP
    return pl.pallas_call(
        _mla_kernel,
        grid=(batch, MLA_HEADS // hp, nq),
        in_specs=[pl.BlockSpec((tq, hp * 2 * LANES), lambda b, h, i: (b * nq + i, h)),
                  pl.BlockSpec((seq, hp * 2 * LANES), lambda b, h, i: (b, h)),
                  pl.BlockSpec((seq, hp * MLA_V), lambda b, h, i: (b, h))],
        out_specs=pl.BlockSpec((tq, hp * MLA_V), lambda b, h, i: (b * nq + i, h)),
        out_shape=jax.ShapeDtypeStruct((t, MLA_HEADS * MLA_V), BF16),
        compiler_params=_params(("parallel", "parallel", "parallel"), 52),
        name="mla_attention",
    )(qf, kf, v)


SWA_BLOCK = 128
SWA_TQ = 512
SWA_WIN = SWA_TQ + 2 * WINDOW


def _swa_kernel(q_ref, k_ref, v_ref, sink_ref, bias_ref, o_ref, *, seq):
    step = pl.program_id(2)
    q0 = step * SWA_TQ
    ks = pl.multiple_of(jnp.clip(q0 - WINDOW, 0, seq - SWA_WIN), WINDOW)
    bias = bias_ref[lax.div(q0 - ks, WINDOW)]
    kw = k_ref[pl.ds(ks, SWA_WIN), :]
    vw = v_ref[pl.ds(ks, SWA_WIN), :]
    for g in range(SWA_GROUP):
        q = q_ref[:, g * SWA_DIM:(g + 1) * SWA_DIM]
        s = lax.dot_general(q, kw, (((1,), (1,)), ((), ())), preferred_element_type=F32) + bias
        sink = sink_ref[g * SWA_BLOCK:g * SWA_BLOCK + 1, :1] * LOG2E
        m = jnp.maximum(jnp.max(s, axis=-1, keepdims=True), sink)
        e = jnp.exp2(s - m)
        denom = jnp.sum(e, axis=-1, keepdims=True) + jnp.exp2(sink - m)
        o = _dot(e.astype(BF16), vw) / denom
        o_ref[:, g * SWA_DIM:(g + 1) * SWA_DIM] = o.astype(BF16)


def _swa_band_bias():
    r = np.arange(SWA_TQ)[:, None]
    c = np.arange(SWA_WIN)[None, :]
    variants = [np.where(np.abs(v * WINDOW + r - c) <= WINDOW, 0.0, -np.inf) for v in range(3)]
    return jnp.asarray(np.stack(variants), F32)


def _swa_attention(qkv, sink_rows, batch, seq):
    t = qkv.shape[0]
    nq = seq // SWA_TQ
    gw = SWA_GROUP * SWA_DIM
    k_col0 = SWA_HEADS
    v_col0 = SWA_HEADS + SWA_KV_HEADS
    return pl.pallas_call(
        functools.partial(_swa_kernel, seq=seq),
        grid=(batch, SWA_KV_HEADS, nq),
        in_specs=[pl.BlockSpec((SWA_TQ, gw), lambda b, h, n: (b * nq + n, h)),
                  pl.BlockSpec((seq, SWA_DIM), lambda b, h, n: (b, k_col0 + h)),
                  pl.BlockSpec((seq, SWA_DIM), lambda b, h, n: (b, v_col0 + h)),
                  pl.BlockSpec((None, SWA_GROUP * SWA_BLOCK, LANES), lambda b, h, n: (h, 0, 0)),
                  pl.BlockSpec((3, SWA_TQ, SWA_WIN), lambda b, h, n: (0, 0, 0))],
        out_specs=pl.BlockSpec((SWA_TQ, gw), lambda b, h, n: (b * nq + n, h)),
        out_shape=jax.ShapeDtypeStruct((t, SWA_HEADS * SWA_DIM), BF16),
        compiler_params=_params(("parallel", "parallel", "parallel"), 48),
        name="swa_attention",
    )(qkv, qkv, qkv, sink_rows, _swa_band_bias())


MERGE_TN = 256
GATE_A_BLOCK0 = SWA_COLS // MERGE_TN
GATE_B_BLOCK0 = (SWA_COLS + D_MODEL) // MERGE_TN


def _merge_kernel(xb_ref, oa_ref, ob_ref, wga_ref, wgb_ref, wa_ref, wb_ref, o_ref):
    xb = xb_ref[...]
    ga = _sigmoid(_dot(xb, wga_ref[...]))
    ya = _dot(oa_ref[...], wa_ref[...].astype(BF16))
    acc = ga * ya
    gb = _sigmoid(_dot(xb, wgb_ref[...]))
    yb = _dot(ob_ref[...], wb_ref[...].astype(BF16))
    o_ref[...] = (acc + gb * yb).astype(BF16)


def _branch_merge(xb, o_a, o_b, w_sh, w_a, w_b, layer):
    t = xb.shape[0]
    tm, tn = 512, MERGE_TN
    ka = o_a.shape[1]
    row = lambda k: pl.BlockSpec((tm, k), lambda i, j: (i, 0))
    return pl.pallas_call(
        _merge_kernel,
        grid=(t // tm, D_MODEL // tn),
        in_specs=[row(D_MODEL), row(ka), row(ka),
                  pl.BlockSpec((D_MODEL, tn), lambda i, j: (0, GATE_A_BLOCK0 + j)),
                  pl.BlockSpec((D_MODEL, tn), lambda i, j: (0, GATE_B_BLOCK0 + j)),
                  pl.BlockSpec((None, ka, tn), lambda i, j: (layer, 0, j)),
                  pl.BlockSpec((None, ka, tn), lambda i, j: (layer, 0, j))],
        out_specs=pl.BlockSpec((tm, tn), lambda i, j: (i, j)),
        out_shape=jax.ShapeDtypeStruct((t, D_MODEL), BF16),
        compiler_params=_params(("parallel", "parallel"), 48),
        name="branch_merge",
    )(xb, o_a, o_b, w_sh, w_sh, w_a, w_b)


def _outproj_kernel(m_ref, w_ref, x_ref, o_ref):
    o_ref[...] = ALPHA * x_ref[...] + _dot(m_ref[...], w_ref[...].astype(BF16))


def _out_proj(merged, w_out, layer, x):
    t = x.shape[0]
    tm, tn = 1024, 512
    return pl.pallas_call(
        _outproj_kernel,
        grid=(t // tm, D_MODEL // tn),
        in_specs=[pl.BlockSpec((tm, D_MODEL), lambda i, j: (i, 0)),
                  pl.BlockSpec((None, D_MODEL, tn), lambda i, j: (layer, 0, j)),
                  pl.BlockSpec((tm, tn), lambda i, j: (i, j))],
        out_specs=pl.BlockSpec((tm, tn), lambda i, j: (i, j)),
        out_shape=jax.ShapeDtypeStruct((t, D_MODEL), F32),
        compiler_params=_params(("parallel", "parallel"), 52),
        name="out_proj",
    )(merged, w_out, x)


def _layer_norm(v, w, b):
    mu = jnp.mean(v, axis=-1, keepdims=True)
    c = v - mu
    var = jnp.mean(c * c, axis=-1, keepdims=True)
    return c * lax.rsqrt(var + LN_EPS) * w + b


def _split_bf16(v):
    hi = v.astype(BF16)
    lo = (v - hi.astype(F32)).astype(BF16)
    return hi, lo


def _ln_route_kernel(pre_ref, lnw_ref, lnb_ref, wr_ref, br_ref,
                     x1_ref, x1b_ref, x1lin_ref, route_ref, route_t_ref, cnt_ref, carry):
    i = pl.program_id(0)

    @pl.when(i == 0)
    def _():
        carry[...] = jnp.zeros_like(carry)

    x1 = _layer_norm(pre_ref[...], lnw_ref[...], lnb_ref[...])
    tm = x1.shape[0]
    x1_ref[...] = x1
    x1b_ref[...] = x1.astype(BF16)

    def to_row_linear(a, carry):
        r0 = pl.multiple_of(a * SUBLANES, SUBLANES)
        for q in range(ROW_CHUNKS // SUBLANES):
            tiles = [x1_ref[pl.ds(r0, SUBLANES), (q * SUBLANES + k) * LANES:(q * SUBLANES + k + 1) * LANES]
                     for k in range(SUBLANES)]
            rows = _transpose8(tiles)
            for b in range(SUBLANES):
                dst = pl.multiple_of((r0 + b) * ROW_CHUNKS + q * SUBLANES, SUBLANES)
                x1lin_ref[pl.ds(dst, SUBLANES), :] = rows[b]
        return carry

    lax.fori_loop(0, tm // SUBLANES, to_row_linear, 0)

    xh, xl = _split_bf16(x1)
    wh, wl = _split_bf16(wr_ref[...])
    logits = _dot(xh, wh) + (_dot(xl, wh) + _dot(xh, wl)) + br_ref[...]

    lane = lax.broadcasted_iota(jnp.int32, logits.shape, 1)
    neg = -jnp.inf
    gl = jnp.where(lane < N_GROUPS, logits, neg)
    gmax = jnp.max(gl, axis=-1, keepdims=True)
    gsum = jnp.sum(jnp.exp(gl - gmax), axis=-1, keepdims=True)
    g_w = 1.0 / gsum
    g_idx = jnp.min(jnp.where(gl == gmax, lane, ROUTE_LANES), axis=-1, keepdims=True)
    d = lane - (EXPERT_LANE0 + g_idx * EXPERTS_PER_GROUP)
    el = jnp.where((d >= 0) & (d < EXPERTS_PER_GROUP), logits, neg)
    m1 = jnp.max(el, axis=-1, keepdims=True)
    i1 = jnp.min(jnp.where(el == m1, lane, ROUTE_LANES), axis=-1, keepdims=True)
    el2 = jnp.where(lane == i1, neg, el)
    m2 = jnp.max(el2, axis=-1, keepdims=True)
    i2 = jnp.min(jnp.where(el2 == m2, lane, ROUTE_LANES), axis=-1, keepdims=True)
    e21 = jnp.exp(m2 - m1)
    w1 = g_w / (1.0 + e21)
    w2 = g_w * e21 / (1.0 + e21)

    oh1 = (lane == i1).astype(F32)
    oh2 = (lane == i2).astype(F32)
    oh = oh1 + oh2
    r_i = lax.broadcasted_iota(jnp.int32, (tm, tm), 0)
    c_i = lax.broadcasted_iota(jnp.int32, (tm, tm), 1)
    tri = jnp.where(c_i < r_i, 1.0, 0.0).astype(BF16)
    before = _dot(tri, oh.astype(BF16)) + carry[...]
    rank1 = jnp.sum(oh1 * before, axis=-1, keepdims=True)
    rank2 = jnp.sum(oh2 * before, axis=-1, keepdims=True)
    carry[...] = carry[...] + jnp.sum(oh, axis=0, keepdims=True)
    cnt_ref[...] = carry[...]

    slab = jnp.where(lane == 0, (i1 - EXPERT_LANE0).astype(F32), 0.0)
    slab = jnp.where(lane == 1, (i2 - EXPERT_LANE0).astype(F32), slab)
    slab = jnp.where(lane == 2, w1, slab)
    slab = jnp.where(lane == 3, w2, slab)
    slab = jnp.where(lane == 4, rank1, slab)
    slab = jnp.where(lane == 5, rank2, slab)
    route_ref[...] = slab
    route_t_ref[...] = jnp.transpose(slab)[:ROUTE_FIELDS, :]


def _ln_route(pre1, ln_w, ln_b, w_r, b_r):
    t = pre1.shape[0]
    tm = 256
    row = lambda n: pl.BlockSpec((tm, n), lambda i: (i, 0))
    const = lambda r, n: pl.BlockSpec((r, n), lambda i: (0, 0))
    return pl.pallas_call(
        _ln_route_kernel,
        grid=(t // tm,),
        in_specs=[row(D_MODEL), const(1, D_MODEL), const(1, D_MODEL),
                  const(D_MODEL, ROUTE_LANES), const(1, ROUTE_LANES)],
        out_specs=[row(D_MODEL), row(D_MODEL),
                   pl.BlockSpec((tm * ROW_CHUNKS, LANES), lambda i: (i, 0)),
                   row(ROUTE_LANES),
                   pl.BlockSpec((ROUTE_FIELDS, tm), lambda i: (0, i)),
                   const(1, ROUTE_LANES)],
        out_shape=[jax.ShapeDtypeStruct((t, D_MODEL), F32),
                   jax.ShapeDtypeStruct((t, D_MODEL), BF16),
                   jax.ShapeDtypeStruct((t * ROW_CHUNKS, LANES), F32),
                   jax.ShapeDtypeStruct((t, ROUTE_LANES), F32),
                   jax.ShapeDtypeStruct((ROUTE_FIELDS, t), F32),
                   jax.ShapeDtypeStruct((1, ROUTE_LANES), F32)],
        scratch_shapes=[pltpu.VMEM((1, ROUTE_LANES), F32)],
        compiler_params=_params(("arbitrary",), 52),
        name="ln1_route",
    )(pre1, ln_w, ln_b, w_r, b_r)


def _plan_kernel(e1_ref, e2_ref, r1_ref, r2_ref, cnt_ref,
                 pos1_ref, pos2_ref, src_ref, te_ref, nv_ref, tstart, *, n_tokens, n_tiles):
    def per_expert(e, first_tile):
        tstart[e] = first_tile
        c = cnt_ref[e]
        nt = lax.div(c + (MOE_TILE - 1), MOE_TILE)

        def set_te(k, carry):
            te_ref[first_tile + k] = e
            return carry

        lax.fori_loop(0, nt, set_te, 0)

        def pad_src(k, carry):
            src_ref[first_tile * MOE_TILE + k] = 0
            return carry

        lax.fori_loop(c, nt * MOE_TILE, pad_src, 0)
        return first_tile + nt

    nv = lax.fori_loop(0, N_EXPERTS, per_expert, 0)
    nv_ref[0] = nv
    last = te_ref[nv - 1]

    def tail_te(k, carry):
        te_ref[k] = last
        return carry

    lax.fori_loop(nv, n_tiles, tail_te, 0)

    def per_token(t, carry):
        p1 = tstart[e1_ref[t]] * MOE_TILE + r1_ref[t]
        p2 = tstart[e2_ref[t]] * MOE_TILE + r2_ref[t]
        pos1_ref[t] = p1
        pos2_ref[t] = p2
        src_ref[p1] = t
        src_ref[p2] = t
        return carry

    lax.fori_loop(0, n_tokens, per_token, 0, unroll=8)


def _dispatch_plan(e1, e2, r1, r2, cnt, n_tiles):
    t = e1.shape[0]
    smem = pl.BlockSpec(memory_space=pltpu.SMEM)
    grid_spec = pltpu.PrefetchScalarGridSpec(
        num_scalar_prefetch=5,
        grid=(1,),
        in_specs=[],
        out_specs=[smem, smem, smem, smem, smem],
        scratch_shapes=[pltpu.SMEM((N_EXPERTS,), jnp.int32)],
    )
    i32 = lambda n: jax.ShapeDtypeStruct((n,), jnp.int32)
    return pl.pallas_call(
        functools.partial(_plan_kernel, n_tokens=t, n_tiles=n_tiles),
        grid_spec=grid_spec,
        out_shape=[i32(t), i32(t), i32(n_tiles * MOE_TILE), i32(n_tiles), i32(1)],
        compiler_params=_params(("arbitrary",), 32),
        name="dispatch_plan",
    )(e1, e2, r1, r2, cnt)


def _moe_kernel(te_ref, nv_ref, src_ref, x_hbm, wg_hbm, wu_hbm, wd_hbm, ys_ref,
                stage, xb, hg, hu, act, ring, sem, wsem, *, layer):
    i = pl.program_id(0)
    j = pl.program_id(1)
    nv = nv_ref[0]
    slot = lax.rem(i, 2)
    tile_rows = MOE_TILE * ROW_CHUNKS
    chunk = i * MOE_STEPS + j
    wslot = lax.rem(chunk, MOE_RING)

    def issue_chunk(c):
        tile = lax.div(c, MOE_STEPS)
        step = lax.rem(c, MOE_STEPS)
        sl = lax.rem(c, MOE_RING)

        @pl.when(tile < nv)
        def _():
            e = te_ref[tile]

            @pl.when(step < MOE_NA)
            def _():
                r0 = pl.multiple_of(step * MOE_KC, MOE_KC)
                pltpu.make_async_copy(wg_hbm.at[layer, e, pl.ds(r0, MOE_KC), :],
                                      ring.at[sl, pl.ds(0, MOE_KC), :], wsem.at[sl]).start()
                pltpu.make_async_copy(wu_hbm.at[layer, e, pl.ds(r0, MOE_KC), :],
                                      ring.at[sl, pl.ds(MOE_KC, MOE_KC), :], wsem.at[sl]).start()

            @pl.when(step >= MOE_NA)
            def _():
                c0 = pl.multiple_of((step - MOE_NA) * MOE_NC, MOE_NC)
                pltpu.make_async_copy(wd_hbm.at[layer, e, :, pl.ds(c0, MOE_NC)],
                                      ring.at[sl], wsem.at[sl]).start()

    def row_copy(tile, sl, r):
        tok = src_ref[tile * MOE_TILE + r]
        return pltpu.make_async_copy(x_hbm.at[pl.ds(tok * ROW_CHUNKS, ROW_CHUNKS)],
                                     stage.at[sl, pl.ds(r * ROW_CHUNKS, ROW_CHUNKS)], sem.at[sl])

    def issue_rows(tile, sl, r0, n):
        def body(r, carry):
            row_copy(tile, sl, r).start()
            return carry

        lax.fori_loop(r0, r0 + n, body, 0)

    @pl.when(i < nv)
    def _():
        @pl.when(chunk == 0)
        def _():
            for c in range(MOE_RING - 1):
                issue_chunk(jnp.int32(c))
            issue_rows(0, 0, 0, MOE_TILE)

        issue_chunk(chunk + (MOE_RING - 1))

        @pl.when(j == 0)
        def _():
            pltpu.make_async_copy(x_hbm.at[pl.ds(0, tile_rows)], stage.at[slot], sem.at[slot]).wait()
            per = MOE_KC // LANES
            pair = 2 * SUBLANES

            def to_tiled(a, carry):
                t0 = pl.multiple_of(a * pair, pair)
                for q in range(ROW_CHUNKS // SUBLANES):
                    halves = []
                    for h in range(2):
                        rows = [stage[slot, pl.ds(pl.multiple_of(
                            (t0 + h * SUBLANES + b) * ROW_CHUNKS + q * SUBLANES, SUBLANES), SUBLANES), :]
                            for b in range(SUBLANES)]
                        halves.append(_transpose8(rows))
                    for k in range(SUBLANES):
                        s = q * SUBLANES + k
                        blk = jnp.concatenate([halves[0][k], halves[1][k]], axis=0).astype(BF16)
                        xb[s // per, pl.ds(t0, pair), (s % per) * LANES:(s % per + 1) * LANES] = blk
                return carry

            lax.fori_loop(0, MOE_TILE // pair, to_tiled, 0)

        @pl.when(i + 1 < nv)
        def _():
            issue_rows(i + 1, 1 - slot, j * MOE_ROWS_PER_STEP, MOE_ROWS_PER_STEP)

        pltpu.make_async_copy(wd_hbm.at[layer, 0, :, pl.ds(0, MOE_NC)], ring.at[wslot],
                              wsem.at[wslot]).wait()

        @pl.when(j < MOE_NA)
        def _():
            xk = xb[jnp.minimum(j, MOE_NA - 1)]
            pg = _dot(xk, ring[wslot, pl.ds(0, MOE_KC), :].astype(BF16))
            pu = _dot(xk, ring[wslot, pl.ds(MOE_KC, MOE_KC), :].astype(BF16))

            @pl.when(j == 0)
            def _():
                hg[...] = pg
                hu[...] = pu

            @pl.when(j > 0)
            def _():
                hg[...] += pg
                hu[...] += pu

        @pl.when(j == MOE_NA)
        def _():
            g = hg[...]
            act[...] = (g * _sigmoid(g) * hu[...]).astype(BF16)

        @pl.when(j >= MOE_NA)
        def _():
            y = _dot(act[...], ring[wslot].astype(BF16))
            per = MOE_NC // LANES
            ys2 = ys_ref.reshape(MOE_TILE * per, LANES)
            for a in range(MOE_TILE // SUBLANES):
                tiles = [y[a * SUBLANES:(a + 1) * SUBLANES, k * LANES:(k + 1) * LANES]
                         for k in range(per)]
                rows = _transpose8(tiles)
                for b in range(SUBLANES):
                    ys2[(a * SUBLANES + b) * per:(a * SUBLANES + b + 1) * per, :] = rows[b]


def _moe_experts(te, nv, src, x1lin, w_gate, w_up, w_down, layer, n_tiles):
    per = MOE_NC // LANES

    def out_map(i, j, te, nv, src):
        valid = i < nv[0]
        return (jnp.where(valid, i, nv[0] - 1),
                jnp.where(valid, jnp.maximum(j - MOE_NA, 0), MOE_NB - 1), 0)

    assert 2 * MOE_KC == EXPERT_FF and MOE_NC == EXPERT_FF
    any_spec = pl.BlockSpec(memory_space=pl.ANY)
    grid_spec = pltpu.PrefetchScalarGridSpec(
        num_scalar_prefetch=3,
        grid=(n_tiles, MOE_STEPS),
        in_specs=[any_spec, any_spec, any_spec, any_spec],
        out_specs=pl.BlockSpec((MOE_TILE, per, LANES), out_map),
        scratch_shapes=[pltpu.VMEM((2, MOE_TILE * ROW_CHUNKS, LANES), F32),
                        pltpu.VMEM((MOE_NA, MOE_TILE, MOE_KC), BF16),
                        pltpu.VMEM((MOE_TILE, EXPERT_FF), F32),
                        pltpu.VMEM((MOE_TILE, EXPERT_FF), F32),
                        pltpu.VMEM((MOE_TILE, EXPERT_FF), BF16),
                        pltpu.VMEM((MOE_RING, EXPERT_FF, MOE_NC), F32),
                        pltpu.SemaphoreType.DMA((2,)),
                        pltpu.SemaphoreType.DMA((MOE_RING,))],
    )
    ys = pl.pallas_call(
        functools.partial(_moe_kernel, layer=layer),
        grid_spec=grid_spec,
        out_shape=jax.ShapeDtypeStruct((n_tiles * MOE_TILE, ROW_CHUNKS, LANES), F32),
        compiler_params=_params(("arbitrary", "arbitrary"), 56),
        name="moe_experts",
    )(te, nv, src, x1lin,
      w_gate.reshape(DEPTH, N_EXPERTS, D_MODEL, EXPERT_FF),
      w_up.reshape(DEPTH, N_EXPERTS, D_MODEL, EXPERT_FF),
      w_down.reshape(DEPTH, N_EXPERTS, EXPERT_FF, D_MODEL))
    return ys


def _ple_kernel(x1b_ref, wpg_ref, p_ref, wpu_ref, x1_ref, o_ref):
    gate = _sigmoid(_dot(x1b_ref[...], wpg_ref[...].astype(BF16)))
    up = _dot(p_ref[...].astype(BF16), wpu_ref[...].astype(BF16))
    o_ref[...] = ALPHA * x1_ref[...] + gate * up


def _ple(x1b, w_pg, p, w_pu, layer, x1):
    t = x1.shape[0]
    tm, tn = 1024, 512
    return pl.pallas_call(
        _ple_kernel,
        grid=(t // tm, D_MODEL // tn),
        in_specs=[pl.BlockSpec((tm, D_MODEL), lambda i, j: (i, 0)),
                  pl.BlockSpec((None, D_MODEL, tn), lambda i, j: (layer, 0, j)),
                  pl.BlockSpec((None, tm, PLE_DIM), lambda i, j: (layer, i, 0)),
                  pl.BlockSpec((None, PLE_DIM, tn), lambda i, j: (layer, 0, j)),
                  pl.BlockSpec((tm, tn), lambda i, j: (i, j))],
        out_specs=pl.BlockSpec((tm, tn), lambda i, j: (i, j)),
        out_shape=jax.ShapeDtypeStruct((t, D_MODEL), F32),
        compiler_params=_params(("parallel", "parallel"), 52),
        name="ple",
    )(x1b, w_pg, p, w_pu, x1)


def _final_kernel(pos1_ref, pos2_ref, pre_ref, route_ref, lnw_ref, lnb_ref, ys_hbm,
                  o_ref, stage, vbuf, sem):
    i = pl.program_id(0)
    n = pl.num_programs(0)
    tm = pre_ref.shape[0]
    slot = lax.rem(i, 2)

    def issue(tile, sl):
        base = tile * tm

        def body(r, carry):
            dst = pl.ds(r * ROW_CHUNKS, ROW_CHUNKS)
            pltpu.make_async_copy(ys_hbm.at[pl.ds(pos1_ref[base + r] * ROW_CHUNKS, ROW_CHUNKS)],
                                  stage.at[sl, 0, dst], sem.at[sl]).start()
            pltpu.make_async_copy(ys_hbm.at[pl.ds(pos2_ref[base + r] * ROW_CHUNKS, ROW_CHUNKS)],
                                  stage.at[sl, 1, dst], sem.at[sl]).start()
            return carry

        lax.fori_loop(0, tm, body, 0)

    @pl.when(i == 0)
    def _():
        issue(0, 0)

    for k in range(2):
        pltpu.make_async_copy(ys_hbm.at[pl.ds(0, tm * ROW_CHUNKS)], stage.at[slot, k],
                              sem.at[slot]).wait()

    @pl.when(i + 1 < n)
    def _():
        issue(i + 1, 1 - slot)

    def combine(a, carry):
        r0 = pl.multiple_of(a * SUBLANES, SUBLANES)
        route = route_ref[pl.ds(r0, SUBLANES), :]
        w1 = jnp.broadcast_to(route[:, 2:3], (SUBLANES, LANES))
        w2 = jnp.broadcast_to(route[:, 3:4], (SUBLANES, LANES))
        for q in range(ROW_CHUNKS // SUBLANES):
            tiles = []
            for slot_half in range(2):
                rows = [stage[slot, slot_half, pl.ds(pl.multiple_of(
                    (r0 + b) * ROW_CHUNKS + q * SUBLANES, SUBLANES), SUBLANES), :]
                    for b in range(SUBLANES)]
                tiles.append(_transpose8(rows))
            for k in range(SUBLANES):
                s = q * SUBLANES + k
                vbuf[pl.ds(r0, SUBLANES), s * LANES:(s + 1) * LANES] = w1 * tiles[0][k] + w2 * tiles[1][k]
        return carry

    lax.fori_loop(0, tm // SUBLANES, combine, 0)
    o_ref[...] = _layer_norm(pre_ref[...] + vbuf[...], lnw_ref[...], lnb_ref[...])


def _final(pos1, pos2, pre2, route, ln_w, ln_b, ys_lin):
    t, d = pre2.shape
    tm = 256
    grid_spec = pltpu.PrefetchScalarGridSpec(
        num_scalar_prefetch=2,
        grid=(t // tm,),
        in_specs=[pl.BlockSpec((tm, d), lambda i, p1, p2: (i, 0)),
                  pl.BlockSpec((tm, ROUTE_LANES), lambda i, p1, p2: (i, 0)),
                  pl.BlockSpec((1, d), lambda i, p1, p2: (0, 0)),
                  pl.BlockSpec((1, d), lambda i, p1, p2: (0, 0)),
                  pl.BlockSpec(memory_space=pl.ANY)],
        out_specs=pl.BlockSpec((tm, d), lambda i, p1, p2: (i, 0)),
        scratch_shapes=[pltpu.VMEM((2, 2, tm * ROW_CHUNKS, LANES), F32),
                        pltpu.VMEM((tm, d), F32),
                        pltpu.SemaphoreType.DMA((2,))],
    )
    return pl.pallas_call(
        _final_kernel,
        grid_spec=grid_spec,
        out_shape=jax.ShapeDtypeStruct((t, d), F32),
        compiler_params=_params(("arbitrary",), 56),
        name="combine_ln2",
    )(pos1, pos2, pre2, route, ln_w, ln_b, ys_lin)


def _prep_q_weight(w_q_up):
    r = w_q_up.shape[0]
    wq = w_q_up.reshape(r, MLA_HEADS, MLA_NOPE + MLA_ROPE)
    z = jnp.zeros((r, MLA_HEADS, LANES - MLA_ROPE), w_q_up.dtype)
    return jnp.concatenate([wq, z], axis=-1).reshape(r, MLA_HEADS * 2 * LANES).astype(BF16)


def _layer(layer, x, p, tabs, batch, seq, w_in, q_norm, kv_norm, w_q_up, w_kv_up, sink, w_branch_a,
           w_branch_b, w_out, ln1_w, ln1_b, w_group, b_group, w_expert_router, b_expert,
           w_gate, w_up, w_down, w_ple_up, w_ple_gate, ln2_w, ln2_b):
    t = x.shape[0]
    cos_a, sin_a, cos_b, sin_b = tabs

    w_lat, w_sh = _prep_input_weight(w_in, layer)
    xb, cq, ckv, kr = _latent_proj(x, w_lat, q_norm[layer][None, :], kv_norm[layer][None, :],
                                   cos_a, sin_a)
    qf = _q_up(cq, _prep_q_weight(w_q_up[layer]), cos_a, sin_a)
    kf, v_a = _kv_up(ckv, w_kv_up, layer, kr)
    o_a = _mla_attention(qf, kf, v_a, batch, seq)

    qkv_b = _swa_proj(xb, w_sh, cos_b, sin_b)
    sink_rows = jnp.broadcast_to(
        jnp.repeat(sink[layer].astype(F32).reshape(SWA_KV_HEADS, SWA_GROUP), SWA_BLOCK, axis=1)[:, :, None],
        (SWA_KV_HEADS, SWA_GROUP * SWA_BLOCK, LANES))
    o_b = _swa_attention(qkv_b, sink_rows, batch, seq)

    merged = _branch_merge(xb, o_a, o_b, w_sh, w_branch_a, w_branch_b, layer)
    pre1 = _out_proj(merged, w_out, layer, x)

    pad = ROUTE_LANES - N_GROUPS - N_EXPERTS
    w_r = jnp.concatenate([w_group[layer], w_expert_router[layer], jnp.zeros((D_MODEL, pad), F32)],
                          axis=1)
    b_r = jnp.concatenate([b_group[layer], b_expert[layer], jnp.zeros((pad,), F32)])[None, :]
    x1, x1b, x1lin, route, route_t, counts = _ln_route(pre1, ln1_w[layer][None, :],
                                                       ln1_b[layer][None, :], w_r, b_r)

    n_tiles = (2 * t) // MOE_TILE + N_EXPERTS
    ri = route_t.astype(jnp.int32)
    cnt = counts[0, EXPERT_LANE0:EXPERT_LANE0 + N_EXPERTS].astype(jnp.int32)
    pos1, pos2, src, te, nv = _dispatch_plan(ri[0], ri[1], ri[4], ri[5], cnt, n_tiles)

    ys = _moe_experts(te, nv, src, x1lin, w_gate, w_up, w_down, layer, n_tiles)
    ys_lin = ys.reshape(n_tiles * MOE_TILE * ROW_CHUNKS, LANES)

    pre2 = _ple(x1b, w_ple_gate, p, w_ple_up, layer, x1)
    return _final(pos1, pos2, pre2, route, ln2_w[layer][None, :], ln2_b[layer][None, :], ys_lin)


def kernel(x, p, positions, w_in, q_norm, kv_norm, w_q_up, w_kv_up, sink, w_branch_a, w_branch_b,
           w_out, ln1_w, ln1_b, w_group, b_group, w_expert_router, b_expert, w_gate, w_up, w_down,
           w_ple_up, w_ple_gate, ln2_w, ln2_b):
    batch, seq, d = x.shape
    t = batch * seq
    h = x.reshape(t, d)
    tabs = _rope_tables(positions.reshape(t, 1))
    p2 = p.reshape(p.shape[0], t, PLE_DIM)
    for layer in range(w_in.shape[0]):
        h = _layer(layer, h, p2, tabs, batch, seq, w_in, q_norm, kv_norm, w_q_up, w_kv_up, sink,
                   w_branch_a, w_branch_b, w_out, ln1_w, ln1_b, w_group, b_group, w_expert_router,
                   b_expert, w_gate, w_up, w_down, w_ple_up, w_ple_gate, ln2_w, ln2_b)
    return h.reshape(batch, seq, d)
```

```python
import functools
import math

import numpy as np
import jax
import jax.numpy as jnp
from jax import lax
from jax.experimental import pallas as pl
from jax.experimental.pallas import tpu as pltpu

F32 = jnp.float32
BF16 = jnp.bfloat16

D_MODEL = 4096
MLA_HEADS = 16
MLA_NOPE = 128
MLA_ROPE = 64
MLA_V = 128
Q_RANK = 768
KV_RANK = 512
SWA_HEADS = 16
SWA_KV_HEADS = 4
SWA_GROUP = SWA_HEADS // SWA_KV_HEADS
SWA_DIM = 128
WINDOW = 128
ROPE_THETA = 10000.0
N_GROUPS = 8
EXPERTS_PER_GROUP = 8
N_EXPERTS = N_GROUPS * EXPERTS_PER_GROUP
EXPERT_FF = 1024
PLE_DIM = 256
LN_EPS = 1e-5
RMS_EPS = 1e-6
DEPTH = 1
ALPHA = (2.0 * DEPTH) ** 0.25
LOG2E = math.log2(math.e)

OFF_KR = Q_RANK + KV_RANK
OFF_QB = OFF_KR + MLA_ROPE
SWA_COLS = (SWA_HEADS + 2 * SWA_KV_HEADS) * SWA_DIM
IN_WIDTH = OFF_QB + SWA_COLS + 2 * D_MODEL

LANES = 128
MIB = 1024 * 1024

ROW_CHUNKS = D_MODEL // LANES

MOE_TILE = 336
MOE_KC = 512
MOE_NA = D_MODEL // MOE_KC
MOE_NC = 1024
MOE_NB = D_MODEL // MOE_NC
MOE_STEPS = MOE_NA + MOE_NB
MOE_ROWS_PER_STEP = MOE_TILE // MOE_STEPS
MOE_RING = 3
ROUTE_LANES = LANES
EXPERT_LANE0 = N_GROUPS
ROUTE_FIELDS = 8


def _params(semantics, vmem_mib):
    return pltpu.CompilerParams(dimension_semantics=semantics,
                                vmem_limit_bytes=vmem_mib * MIB)


def _sigmoid(v):
    return 1.0 / (1.0 + jnp.exp(-v))


def _dot(a, b):
    return jnp.dot(a, b, preferred_element_type=F32)


SUBLANES = 8


def _transpose8(vs):
    sub = lax.broadcasted_iota(jnp.int32, (SUBLANES, LANES), 0)
    vs = list(vs)
    for d in (4, 2, 1):
        low = (sub & d) == 0
        for i in range(SUBLANES):
            if i & d:
                continue
            a, b = vs[i], vs[i + d]
            vs[i] = jnp.where(low, a, pltpu.roll(b, d, 0))
            vs[i + d] = jnp.where(low, pltpu.roll(a, SUBLANES - d, 0), b)
    return vs


def _rot_half_64(v):
    lane = lax.broadcasted_iota(jnp.int32, v.shape, 1)
    half = MLA_ROPE // 2
    fwd = pltpu.roll(v, half, 1)
    bwd = pltpu.roll(v, LANES - half, 1)
    return jnp.where(lane < half, -bwd, jnp.where(lane < MLA_ROPE, fwd, 0.0))


def _tables_kernel(pos_ref, inv_a_ref, inv_b_ref, sign_b_ref, cos_a, sin_a, cos_b, sin_b):
    pos = pos_ref[...].astype(F32)
    ang_a = pos * inv_a_ref[...]
    ang_b = pos * inv_b_ref[...]
    cos_a[...] = jnp.cos(ang_a)
    sin_a[...] = jnp.sin(ang_a)
    cos_b[...] = jnp.cos(ang_b)
    sin_b[...] = jnp.sin(ang_b) * sign_b_ref[...]


def _rope_tables(positions):
    t = positions.shape[0]
    tm = 1024
    inv64 = ROPE_THETA ** (-np.arange(0, MLA_ROPE, 2, dtype=np.float32) / MLA_ROPE)
    inv128 = ROPE_THETA ** (-np.arange(0, SWA_DIM, 2, dtype=np.float32) / SWA_DIM)
    inv_a = jnp.asarray(np.tile(inv64, 4)[None, :], F32)
    inv_b = jnp.asarray(np.tile(inv128, 2)[None, :], F32)
    sign_b = jnp.asarray(np.concatenate([-np.ones(64), np.ones(64)])[None, :], F32)
    row = pl.BlockSpec((tm, LANES), lambda i: (i, 0))
    const = pl.BlockSpec((1, LANES), lambda i: (0, 0))
    out = jax.ShapeDtypeStruct((t, LANES), F32)
    return pl.pallas_call(
        _tables_kernel,
        grid=(t // tm,),
        in_specs=[pl.BlockSpec((tm, 1), lambda i: (i, 0)), const, const, const],
        out_specs=[row, row, row, row],
        out_shape=[out, out, out, out],
        compiler_params=_params(("parallel",), 32),
        name="rope_tables",
    )(positions, inv_a, inv_b, sign_b)


LAT_COLS = OFF_KR + LANES
SHIFT_COLS = IN_WIDTH - OFF_QB
CAST_TILE = 1024


def _cast_transpose_kernel(w_ref, o_ref):
    o_ref[...] = jnp.transpose(w_ref[...]).astype(BF16)


def _prep_input_weight(w_in, layer):
    d = w_in.shape[1]
    w_t = jnp.swapaxes(w_in, 1, 2)
    tk = 512
    w_lat = pl.pallas_call(
        _cast_transpose_kernel,
        grid=(d // tk,),
        in_specs=[pl.BlockSpec((None, LAT_COLS, tk), lambda i: (layer, 0, i))],
        out_specs=pl.BlockSpec((tk, LAT_COLS), lambda i: (i, 0)),
        out_shape=jax.ShapeDtypeStruct((d, LAT_COLS), BF16),
        compiler_params=_params(("parallel",), 32),
        name="cast_latent_w",
    )(w_t)
    w_sh = pl.pallas_call(
        _cast_transpose_kernel,
        grid=(d // CAST_TILE, SHIFT_COLS // CAST_TILE),
        in_specs=[pl.BlockSpec((pl.Element(CAST_TILE), pl.Element(CAST_TILE)),
                               lambda i, j: (pl.multiple_of(OFF_QB + CAST_TILE * j, LANES // 2),
                                             pl.multiple_of(CAST_TILE * i, CAST_TILE)))],
        out_specs=pl.BlockSpec((CAST_TILE, CAST_TILE), lambda i, j: (i, j)),
        out_shape=jax.ShapeDtypeStruct((d, SHIFT_COLS), BF16),
        compiler_params=_params(("parallel", "parallel"), 32),
        name="cast_shift_w",
    )(w_t[layer])
    return w_lat, w_sh


def _latent_kernel(x_ref, w_ref, qn_ref, kvn_ref, cos_ref, sin_ref,
                   xb_ref, cq_ref, ckv_ref, kr_ref):
    xb = x_ref[...].astype(BF16)
    xb_ref[...] = xb
    acc = _dot(xb, w_ref[...])
    cq = acc[:, :Q_RANK]
    ckv = acc[:, Q_RANK:OFF_KR]
    cq = cq * lax.rsqrt(jnp.mean(cq * cq, axis=-1, keepdims=True) + RMS_EPS) * qn_ref[...]
    ckv = ckv * lax.rsqrt(jnp.mean(ckv * ckv, axis=-1, keepdims=True) + RMS_EPS) * kvn_ref[...]
    cq_ref[...] = cq.astype(BF16)
    ckv_ref[...] = ckv.astype(BF16)
    k = acc[:, OFF_KR:]
    lane = lax.broadcasted_iota(jnp.int32, k.shape, 1)
    kr = jnp.where(lane < MLA_ROPE, k * cos_ref[...], 0.0) + _rot_half_64(k) * sin_ref[...]
    kr_ref[...] = kr.astype(BF16)


def _latent_proj(x, w_lat, q_norm, kv_norm, cos_a, sin_a):
    t = x.shape[0]
    tm = 512
    row = lambda n: pl.BlockSpec((tm, n), lambda i: (i, 0))
    const = lambda r, n: pl.BlockSpec((r, n), lambda i: (0, 0))
    resident_w = pl.BlockSpec((D_MODEL, LAT_COLS), lambda i: (0, 0), pipeline_mode=pl.Buffered(1))
    return pl.pallas_call(
        _latent_kernel,
        grid=(t // tm,),
        in_specs=[row(D_MODEL), resident_w, const(1, Q_RANK), const(1, KV_RANK),
                  row(LANES), row(LANES)],
        out_specs=[row(D_MODEL), row(Q_RANK), row(KV_RANK), row(LANES)],
        out_shape=[jax.ShapeDtypeStruct((t, D_MODEL), BF16),
                   jax.ShapeDtypeStruct((t, Q_RANK), BF16),
                   jax.ShapeDtypeStruct((t, KV_RANK), BF16),
                   jax.ShapeDtypeStruct((t, LANES), BF16)],
        compiler_params=_params(("parallel",), 52),
        name="latent_proj",
    )(x, w_lat, q_norm, kv_norm, cos_a, sin_a)


UP_HEADS_PER_STEP = 4


def _qup_kernel(cq_ref, wa_ref, cos_ref, sin_ref, q_ref, *, scale):
    a = _dot(cq_ref[...], wa_ref[...])
    cos = cos_ref[...] * scale
    sin = sin_ref[...] * scale
    for h in range(UP_HEADS_PER_STEP):
        c0 = h * 2 * LANES
        r = a[:, c0 + LANES:c0 + 2 * LANES]
        q_ref[:, c0:c0 + LANES] = (a[:, c0:c0 + LANES] * scale).astype(BF16)
        q_ref[:, c0 + LANES:c0 + 2 * LANES] = (r * cos + _rot_half_64(r) * sin).astype(BF16)


def _q_up(cq, w_qa, cos_a, sin_a):
    t = cq.shape[0]
    tm = 1024
    tn = UP_HEADS_PER_STEP * 2 * LANES
    scale = float((MLA_NOPE + MLA_ROPE) ** -0.5 * LOG2E)
    return pl.pallas_call(
        functools.partial(_qup_kernel, scale=scale),
        grid=(t // tm, MLA_HEADS // UP_HEADS_PER_STEP),
        in_specs=[pl.BlockSpec((tm, Q_RANK), lambda i, j: (i, 0)),
                  pl.BlockSpec((Q_RANK, tn), lambda i, j: (0, j)),
                  pl.BlockSpec((tm, LANES), lambda i, j: (i, 0)),
                  pl.BlockSpec((tm, LANES), lambda i, j: (i, 0))],
        out_specs=pl.BlockSpec((tm, tn), lambda i, j: (i, j)),
        out_shape=jax.ShapeDtypeStruct((t, MLA_HEADS * 2 * LANES), BF16),
        compiler_params=_params(("parallel", "parallel"), 32),
        name="q_up",
    )(cq, w_qa, cos_a, sin_a)


def _kvup_kernel(ckv_ref, w_ref, kr_ref, k_ref, v_ref):
    acc = _dot(ckv_ref[...], w_ref[...].astype(BF16))
    kr = kr_ref[...]
    for h in range(UP_HEADS_PER_STEP):
        c0 = h * 2 * LANES
        k_ref[:, c0:c0 + LANES] = acc[:, c0:c0 + LANES].astype(BF16)
        k_ref[:, c0 + LANES:c0 + 2 * LANES] = kr
        v_ref[:, h * MLA_V:(h + 1) * MLA_V] = acc[:, c0 + LANES:c0 + 2 * LANES].astype(BF16)


def _kv_up(ckv, w_kv_up, layer, kr):
    t = ckv.shape[0]
    tm = 1024
    tn = UP_HEADS_PER_STEP * 2 * LANES
    return pl.pallas_call(
        _kvup_kernel,
        grid=(t // tm, MLA_HEADS // UP_HEADS_PER_STEP),
        in_specs=[pl.BlockSpec((tm, KV_RANK), lambda i, j: (i, 0)),
                  pl.BlockSpec((None, KV_RANK, tn), lambda i, j: (layer, 0, j)),
                  pl.BlockSpec((tm, LANES), lambda i, j: (i, 0))],
        out_specs=[pl.BlockSpec((tm, tn), lambda i, j: (i, j)),
                   pl.BlockSpec((tm, UP_HEADS_PER_STEP * MLA_V), lambda i, j: (i, j))],
        out_shape=[jax.ShapeDtypeStruct((t, MLA_HEADS * 2 * LANES), BF16),
                   jax.ShapeDtypeStruct((t, MLA_HEADS * MLA_V), BF16)],
        compiler_params=_params(("parallel", "parallel"), 32),
        name="kv_up",
    )(ckv, w_kv_up, kr)


SWA_TN = 512
SWA_Q_TILES = SWA_HEADS * SWA_DIM // SWA_TN
SWA_ROPE_TILES = (SWA_HEADS + SWA_KV_HEADS) * SWA_DIM // SWA_TN


def _swaproj_kernel(xb_ref, w_ref, cos_ref, sin_ref, o_ref, *, scale):
    j = pl.program_id(1)
    acc = _dot(xb_ref[...], w_ref[...])

    @pl.when(j < SWA_ROPE_TILES)
    def _():
        sc = jnp.where(j < SWA_Q_TILES, scale, 1.0).astype(F32)
        cos = cos_ref[...] * sc
        sin = sin_ref[...] * sc
        for c in range(SWA_TN // LANES):
            a = acc[:, c * LANES:(c + 1) * LANES]
            r = a * cos + pltpu.roll(a, SWA_DIM // 2, 1) * sin
            o_ref[:, c * LANES:(c + 1) * LANES] = r.astype(BF16)

    @pl.when(j >= SWA_ROPE_TILES)
    def _():
        o_ref[...] = acc.astype(BF16)


def _swa_proj(xb, w_sh, cos_b, sin_b):
    t = xb.shape[0]
    tm = 1024
    return pl.pallas_call(
        functools.partial(_swaproj_kernel, scale=float(SWA_DIM ** -0.5 * LOG2E)),
        grid=(t // tm, SWA_COLS // SWA_TN),
        in_specs=[pl.BlockSpec((tm, D_MODEL), lambda i, j: (i, 0)),
                  pl.BlockSpec((D_MODEL, SWA_TN), lambda i, j: (0, j)),
                  pl.BlockSpec((tm, LANES), lambda i, j: (i, 0)),
                  pl.BlockSpec((tm, LANES), lambda i, j: (i, 0))],
        out_specs=pl.BlockSpec((tm, SWA_TN), lambda i, j: (i, j)),
        out_shape=jax.ShapeDtypeStruct((t, SWA_COLS), BF16),
        compiler_params=_params(("parallel", "parallel"), 48),
        name="swa_proj",
    )(xb, w_sh, cos_b, sin_b)


MLA_HEADS_PER_STEP = 2


def _mla_kernel(q_ref, k_ref, v_ref, o_ref):
    for h in range(MLA_HEADS_PER_STEP):
        q = q_ref[:, h * 2 * LANES:(h + 1) * 2 * LANES]
        k = k_ref[:, h * 2 * LANES:(h + 1) * 2 * LANES]
        s = lax.dot_general(q, k, (((1,), (1,)), ((), ())), preferred_element_type=F32)
        m = jnp.max(s, axis=-1, keepdims=True)
        p = jnp.exp2(s - m)
        l = jnp.sum(p, axis=-1, keepdims=True)
        o = _dot(p.astype(BF16), v_ref[:, h * MLA_V:(h + 1) * MLA_V])
        o_ref[:, h * MLA_V:(h + 1) * MLA_V] = (o / l).astype(BF16)


def _mla_attention(qf, kf, v, batch, seq):
    t = qf.shape[0]
    tq = 1024
    nq = seq // tq
    hp = MLA_HEADS_PER_STEP
    return pl.pallas_call(
        _mla_kernel,
        grid=(batch, MLA_HEADS // hp, nq),
        in_specs=[pl.BlockSpec((tq, hp * 2 * LANES), lambda b, h, i: (b * nq + i, h)),
                  pl.BlockSpec((seq, hp * 2 * LANES), lambda b, h, i: (b, h)),
                  pl.BlockSpec((seq, hp * MLA_V), lambda b, h, i: (b, h))],
        out_specs=pl.BlockSpec((tq, hp * MLA_V), lambda b, h, i: (b * nq + i, h)),
        out_shape=jax.ShapeDtypeStruct((t, MLA_HEADS * MLA_V), BF16),
        compiler_params=_params(("parallel", "parallel", "parallel"), 52),
        name="mla_attention",
    )(qf, kf, v)


SWA_BLOCK = 128
SWA_TQ = 512
SWA_WIN = SWA_TQ + 2 * WINDOW


def _swa_kernel(q_ref, k_ref, v_ref, sink_ref, bias_ref, o_ref, *, seq):
    step = pl.program_id(2)
    q0 = step * SWA_TQ
    ks = pl.multiple_of(jnp.clip(q0 - WINDOW, 0, seq - SWA_WIN), WINDOW)
    bias = bias_ref[lax.div(q0 - ks, WINDOW)]
    kw = k_ref[pl.ds(ks, SWA_WIN), :]
    vw = v_ref[pl.ds(ks, SWA_WIN), :]
    for g in range(SWA_GROUP):
        q = q_ref[:, g * SWA_DIM:(g + 1) * SWA_DIM]
        s = lax.dot_general(q, kw, (((1,), (1,)), ((), ())), preferred_element_type=F32) + bias
        sink = sink_ref[g * SWA_BLOCK:g * SWA_BLOCK + 1, :1] * LOG2E
        m = jnp.maximum(jnp.max(s, axis=-1, keepdims=True), sink)
        e = jnp.exp2(s - m)
        denom = jnp.sum(e, axis=-1, keepdims=True) + jnp.exp2(sink - m)
        o = _dot(e.astype(BF16), vw) / denom
        o_ref[:, g * SWA_DIM:(g + 1) * SWA_DIM] = o.astype(BF16)


def _swa_band_bias():
    r = np.arange(SWA_TQ)[:, None]
    c = np.arange(SWA_WIN)[None, :]
    variants = [np.where(np.abs(v * WINDOW + r - c) <= WINDOW, 0.0, -np.inf) for v in range(3)]
    return jnp.asarray(np.stack(variants), F32)


def _swa_attention(qkv, sink_rows, batch, seq):
    t = qkv.shape[0]
    nq = seq // SWA_TQ
    gw = SWA_GROUP * SWA_DIM
    k_col0 = SWA_HEADS
    v_col0 = SWA_HEADS + SWA_KV_HEADS
    return pl.pallas_call(
        functools.partial(_swa_kernel, seq=seq),
        grid=(batch, SWA_KV_HEADS, nq),
        in_specs=[pl.BlockSpec((SWA_TQ, gw), lambda b, h, n: (b * nq + n, h)),
                  pl.BlockSpec((seq, SWA_DIM), lambda b, h, n: (b, k_col0 + h)),
                  pl.BlockSpec((seq, SWA_DIM), lambda b, h, n: (b, v_col0 + h)),
                  pl.BlockSpec((None, SWA_GROUP * SWA_BLOCK, LANES), lambda b, h, n: (h, 0, 0)),
                  pl.BlockSpec((3, SWA_TQ, SWA_WIN), lambda b, h, n: (0, 0, 0))],
        out_specs=pl.BlockSpec((SWA_TQ, gw), lambda b, h, n: (b * nq + n, h)),
        out_shape=jax.ShapeDtypeStruct((t, SWA_HEADS * SWA_DIM), BF16),
        compiler_params=_params(("parallel", "parallel", "parallel"), 48),
        name="swa_attention",
    )(qkv, qkv, qkv, sink_rows, _swa_band_bias())


MERGE_TN = 256
GATE_A_BLOCK0 = SWA_COLS // MERGE_TN
GATE_B_BLOCK0 = (SWA_COLS + D_MODEL) // MERGE_TN


def _merge_kernel(xb_ref, oa_ref, ob_ref, wga_ref, wgb_ref, wa_ref, wb_ref, o_ref):
    xb = xb_ref[...]
    ga = _sigmoid(_dot(xb, wga_ref[...]))
    ya = _dot(oa_ref[...], wa_ref[...].astype(BF16))
    acc = ga * ya
    gb = _sigmoid(_dot(xb, wgb_ref[...]))
    yb = _dot(ob_ref[...], wb_ref[...].astype(BF16))
    o_ref[...] = (acc + gb * yb).astype(BF16)


def _branch_merge(xb, o_a, o_b, w_sh, w_a, w_b, layer):
    t = xb.shape[0]
    tm, tn = 512, MERGE_TN
    ka = o_a.shape[1]
    row = lambda k: pl.BlockSpec((tm, k), lambda i, j: (i, 0))
    return pl.pallas_call(
        _merge_kernel,
        grid=(t // tm, D_MODEL // tn),
        in_specs=[row(D_MODEL), row(ka), row(ka),
                  pl.BlockSpec((D_MODEL, tn), lambda i, j: (0, GATE_A_BLOCK0 + j)),
                  pl.BlockSpec((D_MODEL, tn), lambda i, j: (0, GATE_B_BLOCK0 + j)),
                  pl.BlockSpec((None, ka, tn), lambda i, j: (layer, 0, j)),
                  pl.BlockSpec((None, ka, tn), lambda i, j: (layer, 0, j))],
        out_specs=pl.BlockSpec((tm, tn), lambda i, j: (i, j)),
        out_shape=jax.ShapeDtypeStruct((t, D_MODEL), BF16),
        compiler_params=_params(("parallel", "parallel"), 48),
        name="branch_merge",
    )(xb, o_a, o_b, w_sh, w_sh, w_a, w_b)


def _outproj_kernel(m_ref, w_ref, x_ref, o_ref):
    o_ref[...] = ALPHA * x_ref[...] + _dot(m_ref[...], w_ref[...].astype(BF16))


def _out_proj(merged, w_out, layer, x):
    t = x.shape[0]
    tm, tn = 1024, 512
    return pl.pallas_call(
        _outproj_kernel,
        grid=(t // tm, D_MODEL // tn),
        in_specs=[pl.BlockSpec((tm, D_MODEL), lambda i, j: (i, 0)),
                  pl.BlockSpec((None, D_MODEL, tn), lambda i, j: (layer, 0, j)),
                  pl.BlockSpec((tm, tn), lambda i, j: (i, j))],
        out_specs=pl.BlockSpec((tm, tn), lambda i, j: (i, j)),
        out_shape=jax.ShapeDtypeStruct((t, D_MODEL), F32),
        compiler_params=_params(("parallel", "parallel"), 52),
        name="out_proj",
    )(merged, w_out, x)


def _layer_norm(v, w, b):
    mu = jnp.mean(v, axis=-1, keepdims=True)
    c = v - mu
    var = jnp.mean(c * c, axis=-1, keepdims=True)
    return c * lax.rsqrt(var + LN_EPS) * w + b


def _split_bf16(v):
    hi = v.astype(BF16)
    lo = (v - hi.astype(F32)).astype(BF16)
    return hi, lo


def _ln_route_kernel(pre_ref, lnw_ref, lnb_ref, wr_ref, br_ref,
                     x1_ref, x1b_ref, x1lin_ref, route_ref, route_t_ref, cnt_ref, carry):
    i = pl.program_id(0)

    @pl.when(i == 0)
    def _():
        carry[...] = jnp.zeros_like(carry)

    x1 = _layer_norm(pre_ref[...], lnw_ref[...], lnb_ref[...])
    tm = x1.shape[0]
    x1_ref[...] = x1
    x1b_ref[...] = x1.astype(BF16)

    def to_row_linear(a, carry):
        r0 = pl.multiple_of(a * SUBLANES, SUBLANES)
        for q in range(ROW_CHUNKS // SUBLANES):
            tiles = [x1_ref[pl.ds(r0, SUBLANES), (q * SUBLANES + k) * LANES:(q * SUBLANES + k + 1) * LANES]
                     for k in range(SUBLANES)]
            rows = _transpose8(tiles)
            for b in range(SUBLANES):
                dst = pl.multiple_of((r0 + b) * ROW_CHUNKS + q * SUBLANES, SUBLANES)
                x1lin_ref[pl.ds(dst, SUBLANES), :] = rows[b]
        return carry

    lax.fori_loop(0, tm // SUBLANES, to_row_linear, 0)

    xh, xl = _split_bf16(x1)
    wh, wl = _split_bf16(wr_ref[...])
    logits = _dot(xh, wh) + (_dot(xl, wh) + _dot(xh, wl)) + br_ref[...]

    lane = lax.broadcasted_iota(jnp.int32, logits.shape, 1)
    neg = -jnp.inf
    gl = jnp.where(lane < N_GROUPS, logits, neg)
    gmax = jnp.max(gl, axis=-1, keepdims=True)
    gsum = jnp.sum(jnp.exp(gl - gmax), axis=-1, keepdims=True)
    g_w = 1.0 / gsum
    g_idx = jnp.min(jnp.where(gl == gmax, lane, ROUTE_LANES), axis=-1, keepdims=True)
    d = lane - (EXPERT_LANE0 + g_idx * EXPERTS_PER_GROUP)
    el = jnp.where((d >= 0) & (d < EXPERTS_PER_GROUP), logits, neg)
    m1 = jnp.max(el, axis=-1, keepdims=True)
    i1 = jnp.min(jnp.where(el == m1, lane, ROUTE_LANES), axis=-1, keepdims=True)
    el2 = jnp.where(lane == i1, neg, el)
    m2 = jnp.max(el2, axis=-1, keepdims=True)
    i2 = jnp.min(jnp.where(el2 == m2, lane, ROUTE_LANES), axis=-1, keepdims=True)
    e21 = jnp.exp(m2 - m1)
    w1 = g_w / (1.0 + e21)
    w2 = g_w * e21 / (1.0 + e21)

    oh1 = (lane == i1).astype(F32)
    oh2 = (lane == i2).astype(F32)
    oh = oh1 + oh2
    r_i = lax.broadcasted_iota(jnp.int32, (tm, tm), 0)
    c_i = lax.broadcasted_iota(jnp.int32, (tm, tm), 1)
    tri = jnp.where(c_i < r_i, 1.0, 0.0).astype(BF16)
    before = _dot(tri, oh.astype(BF16)) + carry[...]
    rank1 = jnp.sum(oh1 * before, axis=-1, keepdims=True)
    rank2 = jnp.sum(oh2 * before, axis=-1, keepdims=True)
    carry[...] = carry[...] + jnp.sum(oh, axis=0, keepdims=True)
    cnt_ref[...] = carry[...]

    slab = jnp.where(lane == 0, (i1 - EXPERT_LANE0).astype(F32), 0.0)
    slab = jnp.where(lane == 1, (i2 - EXPERT_LANE0).astype(F32), slab)
    slab = jnp.where(lane == 2, w1, slab)
    slab = jnp.where(lane == 3, w2, slab)
    slab = jnp.where(lane == 4, rank1, slab)
    slab = jnp.where(lane == 5, rank2, slab)
    route_ref[...] = slab
    route_t_ref[...] = jnp.transpose(slab)[:ROUTE_FIELDS, :]


def _ln_route(pre1, ln_w, ln_b, w_r, b_r):
    t = pre1.shape[0]
    tm = 256
    row = lambda n: pl.BlockSpec((tm, n), lambda i: (i, 0))
    const = lambda r, n: pl.BlockSpec((r, n), lambda i: (0, 0))
    return pl.pallas_call(
        _ln_route_kernel,
        grid=(t // tm,),
        in_specs=[row(D_MODEL), const(1, D_MODEL), const(1, D_MODEL),
                  const(D_MODEL, ROUTE_LANES), const(1, ROUTE_LANES)],
        out_specs=[row(D_MODEL), row(D_MODEL),
                   pl.BlockSpec((tm * ROW_CHUNKS, LANES), lambda i: (i, 0)),
                   row(ROUTE_LANES),
                   pl.BlockSpec((ROUTE_FIELDS, tm), lambda i: (0, i)),
                   const(1, ROUTE_LANES)],
        out_shape=[jax.ShapeDtypeStruct((t, D_MODEL), F32),
                   jax.ShapeDtypeStruct((t, D_MODEL), BF16),
                   jax.ShapeDtypeStruct((t * ROW_CHUNKS, LANES), F32),
                   jax.ShapeDtypeStruct((t, ROUTE_LANES), F32),
                   jax.ShapeDtypeStruct((ROUTE_FIELDS, t), F32),
                   jax.ShapeDtypeStruct((1, ROUTE_LANES), F32)],
        scratch_shapes=[pltpu.VMEM((1, ROUTE_LANES), F32)],
        compiler_params=_params(("arbitrary",), 52),
        name="ln1_route",
    )(pre1, ln_w, ln_b, w_r, b_r)


def _plan_kernel(e1_ref, e2_ref, r1_ref, r2_ref, cnt_ref,
                 pos1_ref, pos2_ref, src_ref, te_ref, nv_ref, tstart, *, n_tokens, n_tiles):
    def per_expert(e, first_tile):
        tstart[e] = first_tile
        c = cnt_ref[e]
        nt = lax.div(c + (MOE_TILE - 1), MOE_TILE)

        def set_te(k, carry):
            te_ref[first_tile + k] = e
            return carry

        lax.fori_loop(0, nt, set_te, 0)

        def pad_src(k, carry):
            src_ref[first_tile * MOE_TILE + k] = 0
            return carry

        lax.fori_loop(c, nt * MOE_TILE, pad_src, 0)
        return first_tile + nt

    nv = lax.fori_loop(0, N_EXPERTS, per_expert, 0)
    nv_ref[0] = nv
    last = te_ref[nv - 1]

    def tail_te(k, carry):
        te_ref[k] = last
        return carry

    lax.fori_loop(nv, n_tiles, tail_te, 0)

    def per_token(t, carry):
        p1 = tstart[e1_ref[t]] * MOE_TILE + r1_ref[t]
        p2 = tstart[e2_ref[t]] * MOE_TILE + r2_ref[t]
        pos1_ref[t] = p1
        pos2_ref[t] = p2
        src_ref[p1] = t
        src_ref[p2] = t
        return carry

    lax.fori_loop(0, n_tokens, per_token, 0, unroll=8)


def _dispatch_plan(e1, e2, r1, r2, cnt, n_tiles):
    t = e1.shape[0]
    smem = pl.BlockSpec(memory_space=pltpu.SMEM)
    grid_spec = pltpu.PrefetchScalarGridSpec(
        num_scalar_prefetch=5,
        grid=(1,),
        in_specs=[],
        out_specs=[smem, smem, smem, smem, smem],
        scratch_shapes=[pltpu.SMEM((N_EXPERTS,), jnp.int32)],
    )
    i32 = lambda n: jax.ShapeDtypeStruct((n,), jnp.int32)
    return pl.pallas_call(
        functools.partial(_plan_kernel, n_tokens=t, n_tiles=n_tiles),
        grid_spec=grid_spec,
        out_shape=[i32(t), i32(t), i32(n_tiles * MOE_TILE), i32(n_tiles), i32(1)],
        compiler_params=_params(("arbitrary",), 32),
        name="dispatch_plan",
    )(e1, e2, r1, r2, cnt)


def _moe_kernel(te_ref, nv_ref, src_ref, x_hbm, wg_hbm, wu_hbm, wd_hbm, ys_ref,
                stage, xb, hg, hu, act, ring, sem, wsem, *, layer):
    i = pl.program_id(0)
    j = pl.program_id(1)
    nv = nv_ref[0]
    slot = lax.rem(i, 2)
    tile_rows = MOE_TILE * ROW_CHUNKS
    chunk = i * MOE_STEPS + j
    wslot = lax.rem(chunk, MOE_RING)

    def issue_chunk(c):
        tile = lax.div(c, MOE_STEPS)
        step = lax.rem(c, MOE_STEPS)
        sl = lax.rem(c, MOE_RING)

        @pl.when(tile < nv)
        def _():
            e = te_ref[tile]

            @pl.when(step < MOE_NA)
            def _():
                r0 = pl.multiple_of(step * MOE_KC, MOE_KC)
                pltpu.make_async_copy(wg_hbm.at[layer, e, pl.ds(r0, MOE_KC), :],
                                      ring.at[sl, pl.ds(0, MOE_KC), :], wsem.at[sl]).start(priority=0)
                pltpu.make_async_copy(wu_hbm.at[layer, e, pl.ds(r0, MOE_KC), :],
                                      ring.at[sl, pl.ds(MOE_KC, MOE_KC), :], wsem.at[sl]).start(priority=1)

            @pl.when(step >= MOE_NA)
            def _():
                c0 = pl.multiple_of((step - MOE_NA) * MOE_NC, MOE_NC)
                half = EXPERT_FF // 2
                for k in range(2):
                    rows = pl.ds(k * half, half)
                    pltpu.make_async_copy(wd_hbm.at[layer, e, rows, pl.ds(c0, MOE_NC)],
                                          ring.at[sl, rows, :], wsem.at[sl]).start(priority=k)

    def row_copy(tile, sl, r):
        tok = src_ref[tile * MOE_TILE + r]
        return pltpu.make_async_copy(x_hbm.at[pl.ds(tok * ROW_CHUNKS, ROW_CHUNKS)],
                                     stage.at[sl, pl.ds(r * ROW_CHUNKS, ROW_CHUNKS)], sem.at[sl])

    def issue_rows(tile, sl, r0, n):
        def body(k, carry):
            row_copy(tile, sl, r0 + 2 * k).start(priority=0)
            row_copy(tile, sl, r0 + 2 * k + 1).start(priority=1)
            return carry

        lax.fori_loop(0, n // 2, body, 0)

    @pl.when(i < nv)
    def _():
        @pl.when(chunk == 0)
        def _():
            for c in range(MOE_RING - 1):
                issue_chunk(jnp.int32(c))
            issue_rows(0, 0, 0, MOE_TILE)

        issue_chunk(chunk + (MOE_RING - 1))

        @pl.when(j == 0)
        def _():
            pltpu.make_async_copy(x_hbm.at[pl.ds(0, tile_rows)], stage.at[slot], sem.at[slot]).wait()
            per = MOE_KC // LANES
            pair = 2 * SUBLANES

            def to_tiled(a, carry):
                t0 = pl.multiple_of(a * pair, pair)
                for q in range(ROW_CHUNKS // SUBLANES):
                    halves = []
                    for h in range(2):
                        rows = [stage[slot, pl.ds(pl.multiple_of(
                            (t0 + h * SUBLANES + b) * ROW_CHUNKS + q * SUBLANES, SUBLANES), SUBLANES), :]
                            for b in range(SUBLANES)]
                        halves.append(_transpose8(rows))
                    for k in range(SUBLANES):
                        s = q * SUBLANES + k
                        blk = jnp.concatenate([halves[0][k], halves[1][k]], axis=0).astype(BF16)
                        xb[s // per, pl.ds(t0, pair), (s % per) * LANES:(s % per + 1) * LANES] = blk
                return carry

            lax.fori_loop(0, MOE_TILE // pair, to_tiled, 0)

        @pl.when(i + 1 < nv)
        def _():
            issue_rows(i + 1, 1 - slot, j * MOE_ROWS_PER_STEP, MOE_ROWS_PER_STEP)

        pltpu.make_async_copy(wd_hbm.at[layer, 0, :, pl.ds(0, MOE_NC)], ring.at[wslot],
                              wsem.at[wslot]).wait()

        @pl.when(j < MOE_NA)
        def _():
            xk = xb[jnp.minimum(j, MOE_NA - 1)]
            pg = _dot(xk, ring[wslot, pl.ds(0, MOE_KC), :].astype(BF16))
            pu = _dot(xk, ring[wslot, pl.ds(MOE_KC, MOE_KC), :].astype(BF16))

            @pl.when(j == 0)
            def _():
                hg[...] = pg
                hu[...] = pu

            @pl.when(j > 0)
            def _():
                hg[...] += pg
                hu[...] += pu

        @pl.when(j == MOE_NA)
        def _():
            g = hg[...]
            act[...] = (g * _sigmoid(g) * hu[...]).astype(BF16)

        @pl.when(j >= MOE_NA)
        def _():
            y = _dot(act[...], ring[wslot].astype(BF16))
            per = MOE_NC // LANES
            ys2 = ys_ref.reshape(MOE_TILE * per, LANES)
            for a in range(MOE_TILE // SUBLANES):
                tiles = [y[a * SUBLANES:(a + 1) * SUBLANES, k * LANES:(k + 1) * LANES]
                         for k in range(per)]
                rows = _transpose8(tiles)
                for b in range(SUBLANES):
                    ys2[(a * SUBLANES + b) * per:(a * SUBLANES + b + 1) * per, :] = rows[b]


def _moe_experts(te, nv, src, x1lin, w_gate, w_up, w_down, layer, n_tiles):
    per = MOE_NC // LANES

    def out_map(i, j, te, nv, src):
        valid = i < nv[0]
        return (jnp.where(valid, i, nv[0] - 1),
                jnp.where(valid, jnp.maximum(j - MOE_NA, 0), MOE_NB - 1), 0)

    assert 2 * MOE_KC == EXPERT_FF and MOE_NC == EXPERT_FF
    any_spec = pl.BlockSpec(memory_space=pl.ANY)
    grid_spec = pltpu.PrefetchScalarGridSpec(
        num_scalar_prefetch=3,
        grid=(n_tiles, MOE_STEPS),
        in_specs=[any_spec, any_spec, any_spec, any_spec],
        out_specs=pl.BlockSpec((MOE_TILE, per, LANES), out_map),
        scratch_shapes=[pltpu.VMEM((2, MOE_TILE * ROW_CHUNKS, LANES), F32),
                        pltpu.VMEM((MOE_NA, MOE_TILE, MOE_KC), BF16),
                        pltpu.VMEM((MOE_TILE, EXPERT_FF), F32),
                        pltpu.VMEM((MOE_TILE, EXPERT_FF), F32),
                        pltpu.VMEM((MOE_TILE, EXPERT_FF), BF16),
                        pltpu.VMEM((MOE_RING, EXPERT_FF, MOE_NC), F32),
                        pltpu.SemaphoreType.DMA((2,)),
                        pltpu.SemaphoreType.DMA((MOE_RING,))],
    )
    ys = pl.pallas_call(
        functools.partial(_moe_kernel, layer=layer),
        grid_spec=grid_spec,
        out_shape=jax.ShapeDtypeStruct((n_tiles * MOE_TILE, ROW_CHUNKS, LANES), F32),
        compiler_params=_params(("arbitrary", "arbitrary"), 56),
        name="moe_experts",
    )(te, nv, src, x1lin,
      w_gate.reshape(DEPTH, N_EXPERTS, D_MODEL, EXPERT_FF),
      w_up.reshape(DEPTH, N_EXPERTS, D_MODEL, EXPERT_FF),
      w_down.reshape(DEPTH, N_EXPERTS, EXPERT_FF, D_MODEL))
    return ys


def _ple_kernel(x1b_ref, wpg_ref, p_ref, wpu_ref, x1_ref, o_ref):
    gate = _sigmoid(_dot(x1b_ref[...], wpg_ref[...].astype(BF16)))
    up = _dot(p_ref[...].astype(BF16), wpu_ref[...].astype(BF16))
    o_ref[...] = ALPHA * x1_ref[...] + gate * up


def _ple(x1b, w_pg, p, w_pu, layer, x1):
    t = x1.shape[0]
    tm, tn = 1024, 512
    return pl.pallas_call(
        _ple_kernel,
        grid=(t // tm, D_MODEL // tn),
        in_specs=[pl.BlockSpec((tm, D_MODEL), lambda i, j: (i, 0)),
                  pl.BlockSpec((None, D_MODEL, tn), lambda i, j: (layer, 0, j)),
                  pl.BlockSpec((None, tm, PLE_DIM), lambda i, j: (layer, i, 0)),
                  pl.BlockSpec((None, PLE_DIM, tn), lambda i, j: (layer, 0, j)),
                  pl.BlockSpec((tm, tn), lambda i, j: (i, j))],
        out_specs=pl.BlockSpec((tm, tn), lambda i, j: (i, j)),
        out_shape=jax.ShapeDtypeStruct((t, D_MODEL), F32),
        compiler_params=_params(("parallel", "parallel"), 52),
        name="ple",
    )(x1b, w_pg, p, w_pu, x1)


def _final_kernel(pos1_ref, pos2_ref, pre_ref, route_ref, lnw_ref, lnb_ref, ys_hbm,
                  o_ref, stage, vbuf, sem):
    i = pl.program_id(0)
    n = pl.num_programs(0)
    tm = pre_ref.shape[0]
    slot = lax.rem(i, 2)

    def issue(tile, sl):
        base = tile * tm

        def body(r, carry):
            dst = pl.ds(r * ROW_CHUNKS, ROW_CHUNKS)
            pltpu.make_async_copy(ys_hbm.at[pl.ds(pos1_ref[base + r] * ROW_CHUNKS, ROW_CHUNKS)],
                                  stage.at[sl, 0, dst], sem.at[sl]).start(priority=0)
            pltpu.make_async_copy(ys_hbm.at[pl.ds(pos2_ref[base + r] * ROW_CHUNKS, ROW_CHUNKS)],
                                  stage.at[sl, 1, dst], sem.at[sl]).start(priority=1)
            return carry

        lax.fori_loop(0, tm, body, 0)

    @pl.when(i == 0)
    def _():
        issue(0, 0)

    for k in range(2):
        pltpu.make_async_copy(ys_hbm.at[pl.ds(0, tm * ROW_CHUNKS)], stage.at[slot, k],
                              sem.at[slot]).wait()

    @pl.when(i + 1 < n)
    def _():
        issue(i + 1, 1 - slot)

    def combine(a, carry):
        r0 = pl.multiple_of(a * SUBLANES, SUBLANES)
        route = route_ref[pl.ds(r0, SUBLANES), :]
        w1 = jnp.broadcast_to(route[:, 2:3], (SUBLANES, LANES))
        w2 = jnp.broadcast_to(route[:, 3:4], (SUBLANES, LANES))
        for q in range(ROW_CHUNKS // SUBLANES):
            tiles = []
            for slot_half in range(2):
                rows = [stage[slot, slot_half, pl.ds(pl.multiple_of(
                    (r0 + b) * ROW_CHUNKS + q * SUBLANES, SUBLANES), SUBLANES), :]
                    for b in range(SUBLANES)]
                tiles.append(_transpose8(rows))
            for k in range(SUBLANES):
                s = q * SUBLANES + k
                vbuf[pl.ds(r0, SUBLANES), s * LANES:(s + 1) * LANES] = w1 * tiles[0][k] + w2 * tiles[1][k]
        return carry

    lax.fori_loop(0, tm // SUBLANES, combine, 0)
    o_ref[...] = _layer_norm(pre_ref[...] + vbuf[...], lnw_ref[...], lnb_ref[...])


def _final(pos1, pos2, pre2, route, ln_w, ln_b, ys_lin):
    t, d = pre2.shape
    tm = 256
    grid_spec = pltpu.PrefetchScalarGridSpec(
        num_scalar_prefetch=2,
        grid=(t // tm,),
        in_specs=[pl.BlockSpec((tm, d), lambda i, p1, p2: (i, 0)),
                  pl.BlockSpec((tm, ROUTE_LANES), lambda i, p1, p2: (i, 0)),
                  pl.BlockSpec((1, d), lambda i, p1, p2: (0, 0)),
                  pl.BlockSpec((1, d), lambda i, p1, p2: (0, 0)),
                  pl.BlockSpec(memory_space=pl.ANY)],
        out_specs=pl.BlockSpec((tm, d), lambda i, p1, p2: (i, 0)),
        scratch_shapes=[pltpu.VMEM((2, 2, tm * ROW_CHUNKS, LANES), F32),
                        pltpu.VMEM((tm, d), F32),
                        pltpu.SemaphoreType.DMA((2,))],
    )
    return pl.pallas_call(
        _final_kernel,
        grid_spec=grid_spec,
        out_shape=jax.ShapeDtypeStruct((t, d), F32),
        compiler_params=_params(("arbitrary",), 56),
        name="combine_ln2",
    )(pos1, pos2, pre2, route, ln_w, ln_b, ys_lin)


def _prep_q_weight(w_q_up):
    r = w_q_up.shape[0]
    wq = w_q_up.reshape(r, MLA_HEADS, MLA_NOPE + MLA_ROPE)
    z = jnp.zeros((r, MLA_HEADS, LANES - MLA_ROPE), w_q_up.dtype)
    return jnp.concatenate([wq, z], axis=-1).reshape(r, MLA_HEADS * 2 * LANES).astype(BF16)


def _layer(layer, x, p, tabs, batch, seq, w_in, q_norm, kv_norm, w_q_up, w_kv_up, sink, w_branch_a,
           w_branch_b, w_out, ln1_w, ln1_b, w_group, b_group, w_expert_router, b_expert,
           w_gate, w_up, w_down, w_ple_up, w_ple_gate, ln2_w, ln2_b):
    t = x.shape[0]
    cos_a, sin_a, cos_b, sin_b = tabs

    w_lat, w_sh = _prep_input_weight(w_in, layer)
    xb, cq, ckv, kr = _latent_proj(x, w_lat, q_norm[layer][None, :], kv_norm[layer][None, :],
                                   cos_a, sin_a)
    qf = _q_up(cq, _prep_q_weight(w_q_up[layer]), cos_a, sin_a)
    kf, v_a = _kv_up(ckv, w_kv_up, layer, kr)
    o_a = _mla_attention(qf, kf, v_a, batch, seq)

    qkv_b = _swa_proj(xb, w_sh, cos_b, sin_b)
    sink_rows = jnp.broadcast_to(
        jnp.repeat(sink[layer].astype(F32).reshape(SWA_KV_HEADS, SWA_GROUP), SWA_BLOCK, axis=1)[:, :, None],
        (SWA_KV_HEADS, SWA_GROUP * SWA_BLOCK, LANES))
    o_b = _swa_attention(qkv_b, sink_rows, batch, seq)

    merged = _branch_merge(xb, o_a, o_b, w_sh, w_branch_a, w_branch_b, layer)
    pre1 = _out_proj(merged, w_out, layer, x)

    pad = ROUTE_LANES - N_GROUPS - N_EXPERTS
    w_r = jnp.concatenate([w_group[layer], w_expert_router[layer], jnp.zeros((D_MODEL, pad), F32)],
                          axis=1)
    b_r = jnp.concatenate([b_group[layer], b_expert[layer], jnp.zeros((pad,), F32)])[None, :]
    x1, x1b, x1lin, route, route_t, counts = _ln_route(pre1, ln1_w[layer][None, :],
                                                       ln1_b[layer][None, :], w_r, b_r)

    n_tiles = (2 * t) // MOE_TILE + N_EXPERTS
    ri = route_t.astype(jnp.int32)
    cnt = counts[0, EXPERT_LANE0:EXPERT_LANE0 + N_EXPERTS].astype(jnp.int32)
    pos1, pos2, src, te, nv = _dispatch_plan(ri[0], ri[1], ri[4], ri[5], cnt, n_tiles)

    ys = _moe_experts(te, nv, src, x1lin, w_gate, w_up, w_down, layer, n_tiles)
    ys_lin = ys.reshape(n_tiles * MOE_TILE * ROW_CHUNKS, LANES)

    pre2 = _ple(x1b, w_ple_gate, p, w_ple_up, layer, x1)
    return _final(pos1, pos2, pre2, route, ln2_w[layer][None, :], ln2_b[layer][None, :], ys_lin)


def kernel(x, p, positions, w_in, q_norm, kv_norm, w_q_up, w_kv_up, sink, w_branch_a, w_branch_b,
           w_out, ln1_w, ln1_b, w_group, b_group, w_expert_router, b_expert, w_gate, w_up, w_down,
           w_ple_up, w_ple_gate, ln2_w, ln2_b):
    batch, seq, d = x.shape
    t = batch * seq
    h = x.reshape(t, d)
    tabs = _rope_tables(positions.reshape(t, 1))
    p2 = p.reshape(p.shape[0], t, PLE_DIM)
    for layer in range(w_in.shape[0]):
        h = _layer(layer, h, p2, tabs, batch, seq, w_in, q_norm, kv_norm, w_q_up, w_kv_up, sink,
                   w_branch_a, w_branch_b, w_out, ln1_w, ln1_b, w_group, b_group, w_expert_router,
                   b_expert, w_gate, w_up, w_down, w_ple_up, w_ple_gate, ln2_w, ln2_b)
    return h.reshape(batch, seq, d)
```

```python
import functools
import math

import numpy as np
import jax
import jax.numpy as jnp
from jax import lax
from jax.experimental import pallas as pl
from jax.experimental.pallas import tpu as pltpu

F32 = jnp.float32
BF16 = jnp.bfloat16

D_MODEL = 4096
MLA_HEADS = 16
MLA_NOPE = 128
MLA_ROPE = 64
MLA_V = 128
Q_RANK = 768
KV_RANK = 512
SWA_HEADS = 16
SWA_KV_HEADS = 4
SWA_GROUP = SWA_HEADS // SWA_KV_HEADS
SWA_DIM = 128
WINDOW = 128
ROPE_THETA = 10000.0
N_GROUPS = 8
EXPERTS_PER_GROUP = 8
N_EXPERTS = N_GROUPS * EXPERTS_PER_GROUP
EXPERT_FF = 1024
PLE_DIM = 256
LN_EPS = 1e-5
RMS_EPS = 1e-6
DEPTH = 1
ALPHA = (2.0 * DEPTH) ** 0.25
LOG2E = math.log2(math.e)

OFF_KR = Q_RANK + KV_RANK
OFF_QB = OFF_KR + MLA_ROPE
SWA_COLS = (SWA_HEADS + 2 * SWA_KV_HEADS) * SWA_DIM
IN_WIDTH = OFF_QB + SWA_COLS + 2 * D_MODEL

LANES = 128
MIB = 1024 * 1024

ROW_CHUNKS = D_MODEL // LANES

MOE_TILE = 336
MOE_KC = 512
MOE_NA = D_MODEL // MOE_KC
MOE_NC = 1024
MOE_NB = D_MODEL // MOE_NC
MOE_STEPS = MOE_NA + MOE_NB
MOE_ROWS_PER_STEP = MOE_TILE // MOE_STEPS
MOE_RING = 3
ROUTE_LANES = LANES
EXPERT_LANE0 = N_GROUPS
ROUTE_FIELDS = 8


def _params(semantics, vmem_mib):
    return pltpu.CompilerParams(dimension_semantics=semantics,
                                vmem_limit_bytes=vmem_mib * MIB)


def _sigmoid(v):
    return 1.0 / (1.0 + jnp.exp(-v))


def _dot(a, b):
    return jnp.dot(a, b, preferred_element_type=F32)


SUBLANES = 8


def _transpose8(vs):
    sub = lax.broadcasted_iota(jnp.int32, (SUBLANES, LANES), 0)
    vs = list(vs)
    for d in (4, 2, 1):
        low = (sub & d) == 0
        for i in range(SUBLANES):
            if i & d:
                continue
            a, b = vs[i], vs[i + d]
            vs[i] = jnp.where(low, a, pltpu.roll(b, d, 0))
            vs[i + d] = jnp.where(low, pltpu.roll(a, SUBLANES - d, 0), b)
    return vs


def _rot_half_64(v):
    lane = lax.broadcasted_iota(jnp.int32, v.shape, 1)
    half = MLA_ROPE // 2
    fwd = pltpu.roll(v, half, 1)
    bwd = pltpu.roll(v, LANES - half, 1)
    return jnp.where(lane < half, -bwd, jnp.where(lane < MLA_ROPE, fwd, 0.0))


def _tables_kernel(pos_ref, inv_a_ref, inv_b_ref, sign_b_ref, cos_a, sin_a, cos_b, sin_b):
    pos = pos_ref[...].astype(F32)
    ang_a = pos * inv_a_ref[...]
    ang_b = pos * inv_b_ref[...]
    cos_a[...] = jnp.cos(ang_a)
    sin_a[...] = jnp.sin(ang_a)
    cos_b[...] = jnp.cos(ang_b)
    sin_b[...] = jnp.sin(ang_b) * sign_b_ref[...]


def _rope_tables(positions):
    t = positions.shape[0]
    tm = 1024
    inv64 = ROPE_THETA ** (-np.arange(0, MLA_ROPE, 2, dtype=np.float32) / MLA_ROPE)
    inv128 = ROPE_THETA ** (-np.arange(0, SWA_DIM, 2, dtype=np.float32) / SWA_DIM)
    inv_a = jnp.asarray(np.tile(inv64, 4)[None, :], F32)
    inv_b = jnp.asarray(np.tile(inv128, 2)[None, :], F32)
    sign_b = jnp.asarray(np.concatenate([-np.ones(64), np.ones(64)])[None, :], F32)
    row = pl.BlockSpec((tm, LANES), lambda i: (i, 0))
    const = pl.BlockSpec((1, LANES), lambda i: (0, 0))
    out = jax.ShapeDtypeStruct((t, LANES), F32)
    return pl.pallas_call(
        _tables_kernel,
        grid=(t // tm,),
        in_specs=[pl.BlockSpec((tm, 1), lambda i: (i, 0)), const, const, const],
        out_specs=[row, row, row, row],
        out_shape=[out, out, out, out],
        compiler_params=_params(("parallel",), 32),
        name="rope_tables",
    )(positions, inv_a, inv_b, sign_b)


LAT_COLS = OFF_KR + LANES
SHIFT_COLS = IN_WIDTH - OFF_QB
CAST_TILE = 1024


def _cast_transpose_kernel(w_ref, o_ref):
    o_ref[...] = jnp.transpose(w_ref[...]).astype(BF16)


def _prep_input_weight(w_in, layer):
    d = w_in.shape[1]
    w_t = jnp.swapaxes(w_in, 1, 2)
    tk = 512
    w_lat = pl.pallas_call(
        _cast_transpose_kernel,
        grid=(d // tk,),
        in_specs=[pl.BlockSpec((None, LAT_COLS, tk), lambda i: (layer, 0, i))],
        out_specs=pl.BlockSpec((tk, LAT_COLS), lambda i: (i, 0)),
        out_shape=jax.ShapeDtypeStruct((d, LAT_COLS), BF16),
        compiler_params=_params(("parallel",), 32),
        name="cast_latent_w",
    )(w_t)
    w_sh = pl.pallas_call(
        _cast_transpose_kernel,
        grid=(d // CAST_TILE, SHIFT_COLS // CAST_TILE),
        in_specs=[pl.BlockSpec((pl.Element(CAST_TILE), pl.Element(CAST_TILE)),
                               lambda i, j: (pl.multiple_of(OFF_QB + CAST_TILE * j, LANES // 2),
                                             pl.multiple_of(CAST_TILE * i, CAST_TILE)))],
        out_specs=pl.BlockSpec((CAST_TILE, CAST_TILE), lambda i, j: (i, j)),
        out_shape=jax.ShapeDtypeStruct((d, SHIFT_COLS), BF16),
        compiler_params=_params(("parallel", "parallel"), 32),
        name="cast_shift_w",
    )(w_t[layer])
    return w_lat, w_sh


def _latent_kernel(x_ref, w_ref, qn_ref, kvn_ref, cos_ref, sin_ref,
                   xb_ref, cq_ref, ckv_ref, kr_ref):
    xb = x_ref[...].astype(BF16)
    xb_ref[...] = xb
    acc = _dot(xb, w_ref[...])
    cq = acc[:, :Q_RANK]
    ckv = acc[:, Q_RANK:OFF_KR]
    cq = cq * lax.rsqrt(jnp.mean(cq * cq, axis=-1, keepdims=True) + RMS_EPS) * qn_ref[...]
    ckv = ckv * lax.rsqrt(jnp.mean(ckv * ckv, axis=-1, keepdims=True) + RMS_EPS) * kvn_ref[...]
    cq_ref[...] = cq.astype(BF16)
    ckv_ref[...] = ckv.astype(BF16)
    k = acc[:, OFF_KR:]
    lane = lax.broadcasted_iota(jnp.int32, k.shape, 1)
    kr = jnp.where(lane < MLA_ROPE, k * cos_ref[...], 0.0) + _rot_half_64(k) * sin_ref[...]
    kr_ref[...] = kr.astype(BF16)


def _latent_proj(x, w_lat, q_norm, kv_norm, cos_a, sin_a):
    t = x.shape[0]
    tm = 512
    row = lambda n: pl.BlockSpec((tm, n), lambda i: (i, 0))
    const = lambda r, n: pl.BlockSpec((r, n), lambda i: (0, 0))
    resident_w = pl.BlockSpec((D_MODEL, LAT_COLS), lambda i: (0, 0), pipeline_mode=pl.Buffered(1))
    return pl.pallas_call(
        _latent_kernel,
        grid=(t // tm,),
        in_specs=[row(D_MODEL), resident_w, const(1, Q_RANK), const(1, KV_RANK),
                  row(LANES), row(LANES)],
        out_specs=[row(D_MODEL), row(Q_RANK), row(KV_RANK), row(LANES)],
        out_shape=[jax.ShapeDtypeStruct((t, D_MODEL), BF16),
                   jax.ShapeDtypeStruct((t, Q_RANK), BF16),
                   jax.ShapeDtypeStruct((t, KV_RANK), BF16),
                   jax.ShapeDtypeStruct((t, LANES), BF16)],
        compiler_params=_params(("parallel",), 52),
        name="latent_proj",
    )(x, w_lat, q_norm, kv_norm, cos_a, sin_a)


UP_HEADS_PER_STEP = 4


def _qup_kernel(cq_ref, wa_ref, cos_ref, sin_ref, q_ref, *, scale):
    a = _dot(cq_ref[...], wa_ref[...])
    cos = cos_ref[...] * scale
    sin = sin_ref[...] * scale
    for h in range(UP_HEADS_PER_STEP):
        c0 = h * 2 * LANES
        r = a[:, c0 + LANES:c0 + 2 * LANES]
        q_ref[:, c0:c0 + LANES] = (a[:, c0:c0 + LANES] * scale).astype(BF16)
        q_ref[:, c0 + LANES:c0 + 2 * LANES] = (r * cos + _rot_half_64(r) * sin).astype(BF16)


def _q_up(cq, w_qa, cos_a, sin_a):
    t = cq.shape[0]
    tm = 1024
    tn = UP_HEADS_PER_STEP * 2 * LANES
    scale = float((MLA_NOPE + MLA_ROPE) ** -0.5 * LOG2E)
    return pl.pallas_call(
        functools.partial(_qup_kernel, scale=scale),
        grid=(t // tm, MLA_HEADS // UP_HEADS_PER_STEP),
        in_specs=[pl.BlockSpec((tm, Q_RANK), lambda i, j: (i, 0)),
                  pl.BlockSpec((Q_RANK, tn), lambda i, j: (0, j)),
                  pl.BlockSpec((tm, LANES), lambda i, j: (i, 0)),
                  pl.BlockSpec((tm, LANES), lambda i, j: (i, 0))],
        out_specs=pl.BlockSpec((tm, tn), lambda i, j: (i, j)),
        out_shape=jax.ShapeDtypeStruct((t, MLA_HEADS * 2 * LANES), BF16),
        compiler_params=_params(("parallel", "parallel"), 32),
        name="q_up",
    )(cq, w_qa, cos_a, sin_a)


def _kvup_kernel(ckv_ref, w_ref, kr_ref, k_ref, v_ref):
    acc = _dot(ckv_ref[...], w_ref[...].astype(BF16))
    kr = kr_ref[...]
    for h in range(UP_HEADS_PER_STEP):
        c0 = h * 2 * LANES
        k_ref[:, c0:c0 + LANES] = acc[:, c0:c0 + LANES].astype(BF16)
        k_ref[:, c0 + LANES:c0 + 2 * LANES] = kr
        v_ref[:, h * MLA_V:(h + 1) * MLA_V] = acc[:, c0 + LANES:c0 + 2 * LANES].astype(BF16)


def _kv_up(ckv, w_kv_up, layer, kr):
    t = ckv.shape[0]
    tm = 1024
    tn = UP_HEADS_PER_STEP * 2 * LANES
    return pl.pallas_call(
        _kvup_kernel,
        grid=(t // tm, MLA_HEADS // UP_HEADS_PER_STEP),
        in_specs=[pl.BlockSpec((tm, KV_RANK), lambda i, j: (i, 0)),
                  pl.BlockSpec((None, KV_RANK, tn), lambda i, j: (layer, 0, j)),
                  pl.BlockSpec((tm, LANES), lambda i, j: (i, 0))],
        out_specs=[pl.BlockSpec((tm, tn), lambda i, j: (i, j)),
                   pl.BlockSpec((tm, UP_HEADS_PER_STEP * MLA_V), lambda i, j: (i, j))],
        out_shape=[jax.ShapeDtypeStruct((t, MLA_HEADS * 2 * LANES), BF16),
                   jax.ShapeDtypeStruct((t, MLA_HEADS * MLA_V), BF16)],
        compiler_params=_params(("parallel", "parallel"), 32),
        name="kv_up",
    )(ckv, w_kv_up, kr)


SWA_TN = 512
SWA_Q_TILES = SWA_HEADS * SWA_DIM // SWA_TN
SWA_ROPE_TILES = (SWA_HEADS + SWA_KV_HEADS) * SWA_DIM // SWA_TN


def _swaproj_kernel(xb_ref, w_ref, cos_ref, sin_ref, o_ref, *, scale):
    j = pl.program_id(1)
    acc = _dot(xb_ref[...], w_ref[...])

    @pl.when(j < SWA_ROPE_TILES)
    def _():
        sc = jnp.where(j < SWA_Q_TILES, scale, 1.0).astype(F32)
        cos = cos_ref[...] * sc
        sin = sin_ref[...] * sc
        for c in range(SWA_TN // LANES):
            a = acc[:, c * LANES:(c + 1) * LANES]
            r = a * cos + pltpu.roll(a, SWA_DIM // 2, 1) * sin
            o_ref[:, c * LANES:(c + 1) * LANES] = r.astype(BF16)

    @pl.when(j >= SWA_ROPE_TILES)
    def _():
        o_ref[...] = acc.astype(BF16)


def _swa_proj(xb, w_sh, cos_b, sin_b):
    t = xb.shape[0]
    tm = 1024
    return pl.pallas_call(
        functools.partial(_swaproj_kernel, scale=float(SWA_DIM ** -0.5 * LOG2E)),
        grid=(t // tm, SWA_COLS // SWA_TN),
        in_specs=[pl.BlockSpec((tm, D_MODEL), lambda i, j: (i, 0)),
                  pl.BlockSpec((D_MODEL, SWA_TN), lambda i, j: (0, j)),
                  pl.BlockSpec((tm, LANES), lambda i, j: (i, 0)),
                  pl.BlockSpec((tm, LANES), lambda i, j: (i, 0))],
        out_specs=pl.BlockSpec((tm, SWA_TN), lambda i, j: (i, j)),
        out_shape=jax.ShapeDtypeStruct((t, SWA_COLS), BF16),
        compiler_params=_params(("parallel", "parallel"), 48),
        name="swa_proj",
    )(xb, w_sh, cos_b, sin_b)


MLA_HEADS_PER_STEP = 2


def _mla_kernel(q_ref, k_ref, v_ref, o_ref):
    for h in range(MLA_HEADS_PER_STEP):
        q = q_ref[:, h * 2 * LANES:(h + 1) * 2 * LANES]
        k = k_ref[:, h * 2 * LANES:(h + 1) * 2 * LANES]
        s = lax.dot_general(q, k, (((1,), (1,)), ((), ())), preferred_element_type=F32)
        m = jnp.max(s, axis=-1, keepdims=True)
        p = jnp.exp2(s - m)
        l = jnp.sum(p, axis=-1, keepdims=True)
        o = _dot(p.astype(BF16), v_ref[:, h * MLA_V:(h + 1) * MLA_V])
        o_ref[:, h * MLA_V:(h + 1) * MLA_V] = (o / l).astype(BF16)


def _mla_attention(qf, kf, v, batch, seq):
    t = qf.shape[0]
    tq = 1024
    nq = seq // tq
    hp = MLA_HEADS_PER_STEP
    return pl.pallas_call(
        _mla_kernel,
        grid=(batch, MLA_HEADS // hp, nq),
        in_specs=[pl.BlockSpec((tq, hp * 2 * LANES), lambda b, h, i: (b * nq + i, h)),
                  pl.BlockSpec((seq, hp * 2 * LANES), lambda b, h, i: (b, h)),
                  pl.BlockSpec((seq, hp * MLA_V), lambda b, h, i: (b, h))],
        out_specs=pl.BlockSpec((tq, hp * MLA_V), lambda b, h, i: (b * nq + i, h)),
        out_shape=jax.ShapeDtypeStruct((t, MLA_HEADS * MLA_V), BF16),
        compiler_params=_params(("parallel", "parallel", "parallel"), 52),
        name="mla_attention",
    )(qf, kf, v)


SWA_BLOCK = 128
SWA_TQ = 512
SWA_WIN = SWA_TQ + 2 * WINDOW


def _swa_kernel(q_ref, k_ref, v_ref, sink_ref, bias_ref, o_ref, *, seq):
    step = pl.program_id(2)
    q0 = step * SWA_TQ
    ks = pl.multiple_of(jnp.clip(q0 - WINDOW, 0, seq - SWA_WIN), WINDOW)
    bias = bias_ref[lax.div(q0 - ks, WINDOW)]
    kw = k_ref[pl.ds(ks, SWA_WIN), :]
    vw = v_ref[pl.ds(ks, SWA_WIN), :]
    for g in range(SWA_GROUP):
        q = q_ref[:, g * SWA_DIM:(g + 1) * SWA_DIM]
        s = lax.dot_general(q, kw, (((1,), (1,)), ((), ())), preferred_element_type=F32) + bias
        sink = sink_ref[g * SWA_BLOCK:g * SWA_BLOCK + 1, :1] * LOG2E
        m = jnp.maximum(jnp.max(s, axis=-1, keepdims=True), sink)
        e = jnp.exp2(s - m)
        denom = jnp.sum(e, axis=-1, keepdims=True) + jnp.exp2(sink - m)
        o = _dot(e.astype(BF16), vw) / denom
        o_ref[:, g * SWA_DIM:(g + 1) * SWA_DIM] = o.astype(BF16)


def _swa_band_bias():
    r = np.arange(SWA_TQ)[:, None]
    c = np.arange(SWA_WIN)[None, :]
    variants = [np.where(np.abs(v * WINDOW + r - c) <= WINDOW, 0.0, -np.inf) for v in range(3)]
    return jnp.asarray(np.stack(variants), F32)


def _swa_attention(qkv, sink_rows, batch, seq):
    t = qkv.shape[0]
    nq = seq // SWA_TQ
    gw = SWA_GROUP * SWA_DIM
    k_col0 = SWA_HEADS
    v_col0 = SWA_HEADS + SWA_KV_HEADS
    return pl.pallas_call(
        functools.partial(_swa_kernel, seq=seq),
        grid=(batch, SWA_KV_HEADS, nq),
        in_specs=[pl.BlockSpec((SWA_TQ, gw), lambda b, h, n: (b * nq + n, h)),
                  pl.BlockSpec((seq, SWA_DIM), lambda b, h, n: (b, k_col0 + h)),
                  pl.BlockSpec((seq, SWA_DIM), lambda b, h, n: (b, v_col0 + h)),
                  pl.BlockSpec((None, SWA_GROUP * SWA_BLOCK, LANES), lambda b, h, n: (h, 0, 0)),
                  pl.BlockSpec((3, SWA_TQ, SWA_WIN), lambda b, h, n: (0, 0, 0))],
        out_specs=pl.BlockSpec((SWA_TQ, gw), lambda b, h, n: (b * nq + n, h)),
        out_shape=jax.ShapeDtypeStruct((t, SWA_HEADS * SWA_DIM), BF16),
        compiler_params=_params(("parallel", "parallel", "parallel"), 48),
        name="swa_attention",
    )(qkv, qkv, qkv, sink_rows, _swa_band_bias())


MERGE_TN = 256
GATE_A_BLOCK0 = SWA_COLS // MERGE_TN
GATE_B_BLOCK0 = (SWA_COLS + D_MODEL) // MERGE_TN


def _merge_kernel(xb_ref, oa_ref, ob_ref, wga_ref, wgb_ref, wa_ref, wb_ref, o_ref):
    xb = xb_ref[...]
    ga = _sigmoid(_dot(xb, wga_ref[...]))
    ya = _dot(oa_ref[...], wa_ref[...].astype(BF16))
    acc = ga * ya
    gb = _sigmoid(_dot(xb, wgb_ref[...]))
    yb = _dot(ob_ref[...], wb_ref[...].astype(BF16))
    o_ref[...] = (acc + gb * yb).astype(BF16)


def _branch_merge(xb, o_a, o_b, w_sh, w_a, w_b, layer):
    t = xb.shape[0]
    tm, tn = 512, MERGE_TN
    ka = o_a.shape[1]
    row = lambda k: pl.BlockSpec((tm, k), lambda i, j: (i, 0))
    return pl.pallas_call(
        _merge_kernel,
        grid=(t // tm, D_MODEL // tn),
        in_specs=[row(D_MODEL), row(ka), row(ka),
                  pl.BlockSpec((D_MODEL, tn), lambda i, j: (0, GATE_A_BLOCK0 + j)),
                  pl.BlockSpec((D_MODEL, tn), lambda i, j: (0, GATE_B_BLOCK0 + j)),
                  pl.BlockSpec((None, ka, tn), lambda i, j: (layer, 0, j)),
                  pl.BlockSpec((None, ka, tn), lambda i, j: (layer, 0, j))],
        out_specs=pl.BlockSpec((tm, tn), lambda i, j: (i, j)),
        out_shape=jax.ShapeDtypeStruct((t, D_MODEL), BF16),
        compiler_params=_params(("parallel", "parallel"), 48),
        name="branch_merge",
    )(xb, o_a, o_b, w_sh, w_sh, w_a, w_b)


def _outproj_kernel(m_ref, w_ref, x_ref, o_ref):
    o_ref[...] = ALPHA * x_ref[...] + _dot(m_ref[...], w_ref[...].astype(BF16))


def _out_proj(merged, w_out, layer, x):
    t = x.shape[0]
    tm, tn = 1024, 512
    return pl.pallas_call(
        _outproj_kernel,
        grid=(t // tm, D_MODEL // tn),
        in_specs=[pl.BlockSpec((tm, D_MODEL), lambda i, j: (i, 0)),
                  pl.BlockSpec((None, D_MODEL, tn), lambda i, j: (layer, 0, j)),
                  pl.BlockSpec((tm, tn), lambda i, j: (i, j))],
        out_specs=pl.BlockSpec((tm, tn), lambda i, j: (i, j)),
        out_shape=jax.ShapeDtypeStruct((t, D_MODEL), F32),
        compiler_params=_params(("parallel", "parallel"), 52),
        name="out_proj",
    )(merged, w_out, x)


def _layer_norm(v, w, b):
    mu = jnp.mean(v, axis=-1, keepdims=True)
    c = v - mu
    var = jnp.mean(c * c, axis=-1, keepdims=True)
    return c * lax.rsqrt(var + LN_EPS) * w + b


def _split_bf16(v):
    hi = v.astype(BF16)
    lo = (v - hi.astype(F32)).astype(BF16)
    return hi, lo


def _ln_route_kernel(pre_ref, lnw_ref, lnb_ref, wr_ref, br_ref,
                     x1_ref, x1b_ref, x1lin_ref, route_ref, route_t_ref, cnt_ref, carry):
    i = pl.program_id(0)

    @pl.when(i == 0)
    def _():
        carry[...] = jnp.zeros_like(carry)

    x1 = _layer_norm(pre_ref[...], lnw_ref[...], lnb_ref[...])
    tm = x1.shape[0]
    x1_ref[...] = x1
    x1b_ref[...] = x1.astype(BF16)

    def to_row_linear(a, carry):
        r0 = pl.multiple_of(a * SUBLANES, SUBLANES)
        for q in range(ROW_CHUNKS // SUBLANES):
            tiles = [x1_ref[pl.ds(r0, SUBLANES), (q * SUBLANES + k) * LANES:(q * SUBLANES + k + 1) * LANES]
                     for k in range(SUBLANES)]
            rows = _transpose8(tiles)
            for b in range(SUBLANES):
                dst = pl.multiple_of((r0 + b) * ROW_CHUNKS + q * SUBLANES, SUBLANES)
                x1lin_ref[pl.ds(dst, SUBLANES), :] = rows[b]
        return carry

    lax.fori_loop(0, tm // SUBLANES, to_row_linear, 0)

    xh, xl = _split_bf16(x1)
    wh, wl = _split_bf16(wr_ref[...])
    logits = _dot(xh, wh) + (_dot(xl, wh) + _dot(xh, wl)) + br_ref[...]

    lane = lax.broadcasted_iota(jnp.int32, logits.shape, 1)
    neg = -jnp.inf
    gl = jnp.where(lane < N_GROUPS, logits, neg)
    gmax = jnp.max(gl, axis=-1, keepdims=True)
    gsum = jnp.sum(jnp.exp(gl - gmax), axis=-1, keepdims=True)
    g_w = 1.0 / gsum
    g_idx = jnp.min(jnp.where(gl == gmax, lane, ROUTE_LANES), axis=-1, keepdims=True)
    d = lane - (EXPERT_LANE0 + g_idx * EXPERTS_PER_GROUP)
    el = jnp.where((d >= 0) & (d < EXPERTS_PER_GROUP), logits, neg)
    m1 = jnp.max(el, axis=-1, keepdims=True)
    i1 = jnp.min(jnp.where(el == m1, lane, ROUTE_LANES), axis=-1, keepdims=True)
    el2 = jnp.where(lane == i1, neg, el)
    m2 = jnp.max(el2, axis=-1, keepdims=True)
    i2 = jnp.min(jnp.where(el2 == m2, lane, ROUTE_LANES), axis=-1, keepdims=True)
    e21 = jnp.exp(m2 - m1)
    w1 = g_w / (1.0 + e21)
    w2 = g_w * e21 / (1.0 + e21)

    oh1 = (lane == i1).astype(F32)
    oh2 = (lane == i2).astype(F32)
    oh = oh1 + oh2
    r_i = lax.broadcasted_iota(jnp.int32, (tm, tm), 0)
    c_i = lax.broadcasted_iota(jnp.int32, (tm, tm), 1)
    tri = jnp.where(c_i < r_i, 1.0, 0.0).astype(BF16)
    before = _dot(tri, oh.astype(BF16)) + carry[...]
    rank1 = jnp.sum(oh1 * before, axis=-1, keepdims=True)
    rank2 = jnp.sum(oh2 * before, axis=-1, keepdims=True)
    carry[...] = carry[...] + jnp.sum(oh, axis=0, keepdims=True)
    cnt_ref[...] = carry[...]

    slab = jnp.where(lane == 0, (i1 - EXPERT_LANE0).astype(F32), 0.0)
    slab = jnp.where(lane == 1, (i2 - EXPERT_LANE0).astype(F32), slab)
    slab = jnp.where(lane == 2, w1, slab)
    slab = jnp.where(lane == 3, w2, slab)
    slab = jnp.where(lane == 4, rank1, slab)
    slab = jnp.where(lane == 5, rank2, slab)
    route_ref[...] = slab
    route_t_ref[...] = jnp.transpose(slab)[:ROUTE_FIELDS, :]


def _ln_route(pre1, ln_w, ln_b, w_r, b_r):
    t = pre1.shape[0]
    tm = 256
    row = lambda n: pl.BlockSpec((tm, n), lambda i: (i, 0))
    const = lambda r, n: pl.BlockSpec((r, n), lambda i: (0, 0))
    return pl.pallas_call(
        _ln_route_kernel,
        grid=(t // tm,),
        in_specs=[row(D_MODEL), const(1, D_MODEL), const(1, D_MODEL),
                  const(D_MODEL, ROUTE_LANES), const(1, ROUTE_LANES)],
        out_specs=[row(D_MODEL), row(D_MODEL),
                   pl.BlockSpec((tm * ROW_CHUNKS, LANES), lambda i: (i, 0)),
                   row(ROUTE_LANES),
                   pl.BlockSpec((ROUTE_FIELDS, tm), lambda i: (0, i)),
                   const(1, ROUTE_LANES)],
        out_shape=[jax.ShapeDtypeStruct((t, D_MODEL), F32),
                   jax.ShapeDtypeStruct((t, D_MODEL), BF16),
                   jax.ShapeDtypeStruct((t * ROW_CHUNKS, LANES), F32),
                   jax.ShapeDtypeStruct((t, ROUTE_LANES), F32),
                   jax.ShapeDtypeStruct((ROUTE_FIELDS, t), F32),
                   jax.ShapeDtypeStruct((1, ROUTE_LANES), F32)],
        scratch_shapes=[pltpu.VMEM((1, ROUTE_LANES), F32)],
        compiler_params=_params(("arbitrary",), 52),
        name="ln1_route",
    )(pre1, ln_w, ln_b, w_r, b_r)


def _plan_kernel(e1_ref, e2_ref, r1_ref, r2_ref, cnt_ref,
                 pos1_ref, pos2_ref, src_ref, te_ref, nv_ref, tstart, *, n_tokens, n_tiles):
    def per_expert(e, first_tile):
        tstart[e] = first_tile
        c = cnt_ref[e]
        nt = lax.div(c + (MOE_TILE - 1), MOE_TILE)

        def set_te(k, carry):
            te_ref[first_tile + k] = e
            return carry

        lax.fori_loop(0, nt, set_te, 0)

        def pad_src(k, carry):
            src_ref[first_tile * MOE_TILE + k] = 0
            return carry

        lax.fori_loop(c, nt * MOE_TILE, pad_src, 0)
        return first_tile + nt

    nv = lax.fori_loop(0, N_EXPERTS, per_expert, 0)
    nv_ref[0] = nv
    last = te_ref[nv - 1]

    def tail_te(k, carry):
        te_ref[k] = last
        return carry

    lax.fori_loop(nv, n_tiles, tail_te, 0)

    def per_token(t, carry):
        p1 = tstart[e1_ref[t]] * MOE_TILE + r1_ref[t]
        p2 = tstart[e2_ref[t]] * MOE_TILE + r2_ref[t]
        pos1_ref[t] = p1
        pos2_ref[t] = p2
        src_ref[p1] = t
        src_ref[p2] = t
        return carry

    lax.fori_loop(0, n_tokens, per_token, 0, unroll=8)


def _dispatch_plan(e1, e2, r1, r2, cnt, n_tiles):
    t = e1.shape[0]
    smem = pl.BlockSpec(memory_space=pltpu.SMEM)
    grid_spec = pltpu.PrefetchScalarGridSpec(
        num_scalar_prefetch=5,
        grid=(1,),
        in_specs=[],
        out_specs=[smem, smem, smem, smem, smem],
        scratch_shapes=[pltpu.SMEM((N_EXPERTS,), jnp.int32)],
    )
    i32 = lambda n: jax.ShapeDtypeStruct((n,), jnp.int32)
    return pl.pallas_call(
        functools.partial(_plan_kernel, n_tokens=t, n_tiles=n_tiles),
        grid_spec=grid_spec,
        out_shape=[i32(t), i32(t), i32(n_tiles * MOE_TILE), i32(n_tiles), i32(1)],
        compiler_params=_params(("arbitrary",), 32),
        name="dispatch_plan",
    )(e1, e2, r1, r2, cnt)


def _moe_kernel(te_ref, nv_ref, src_ref, x_hbm, wg_hbm, wu_hbm, wd_hbm, ys_ref,
                stage, xb, hg, hu, act, ring, sem, wsem, *, layer):
    i = pl.program_id(0)
    nv = nv_ref[0]
    slot = lax.rem(i, 2)
    tile_rows = MOE_TILE * ROW_CHUNKS
    half_ff = EXPERT_FF // 2
    out_per = MOE_NC // LANES

    def issue_chunk(tile, step):
        sl = step % MOE_RING
        e = te_ref[tile]
        if step < MOE_NA:
            rows = pl.ds(step * MOE_KC, MOE_KC)
            pltpu.make_async_copy(wg_hbm.at[layer, e, rows, :],
                                  ring.at[sl, pl.ds(0, MOE_KC), :], wsem.at[sl]).start(priority=0)
            pltpu.make_async_copy(wu_hbm.at[layer, e, rows, :],
                                  ring.at[sl, pl.ds(MOE_KC, MOE_KC), :], wsem.at[sl]).start(priority=1)
        else:
            cols = pl.ds((step - MOE_NA) * MOE_NC, MOE_NC)
            for k in range(2):
                rows = pl.ds(k * half_ff, half_ff)
                pltpu.make_async_copy(wd_hbm.at[layer, e, rows, cols],
                                      ring.at[sl, rows, :], wsem.at[sl]).start(priority=k)

    def wait_chunk(step):
        sl = step % MOE_RING
        pltpu.make_async_copy(wd_hbm.at[layer, 0, :, pl.ds(0, MOE_NC)], ring.at[sl],
                              wsem.at[sl]).wait()

    def row_copy(tile, sl, r):
        tok = src_ref[tile * MOE_TILE + r]
        return pltpu.make_async_copy(x_hbm.at[pl.ds(tok * ROW_CHUNKS, ROW_CHUNKS)],
                                     stage.at[sl, pl.ds(r * ROW_CHUNKS, ROW_CHUNKS)], sem.at[sl])

    def issue_rows(tile, sl, r0, n):
        def body(k, carry):
            row_copy(tile, sl, r0 + 2 * k).start(priority=0)
            row_copy(tile, sl, r0 + 2 * k + 1).start(priority=1)
            return carry

        lax.fori_loop(0, n // 2, body, 0)

    def to_tiled(a, carry):
        per = MOE_KC // LANES
        pair = 2 * SUBLANES
        t0 = pl.multiple_of(a * pair, pair)
        for q in range(ROW_CHUNKS // SUBLANES):
            halves = []
            for h in range(2):
                rows = [stage[slot, pl.ds(pl.multiple_of(
                    (t0 + h * SUBLANES + b) * ROW_CHUNKS + q * SUBLANES, SUBLANES), SUBLANES), :]
                    for b in range(SUBLANES)]
                halves.append(_transpose8(rows))
            for k in range(SUBLANES):
                s = q * SUBLANES + k
                blk = jnp.concatenate([halves[0][k], halves[1][k]], axis=0).astype(BF16)
                xb[s // per, pl.ds(t0, pair), (s % per) * LANES:(s % per + 1) * LANES] = blk
        return carry

    def up_gate_step(step):
        sl = step % MOE_RING
        xk = xb[step]
        pg = _dot(xk, ring[sl, 0:MOE_KC, :].astype(BF16))
        pu = _dot(xk, ring[sl, MOE_KC:2 * MOE_KC, :].astype(BF16))
        if step == 0:
            hg[...] = pg
            hu[...] = pu
        elif step < MOE_NA - 1:
            hg[...] += pg
            hu[...] += pu
        else:
            g = hg[...] + pg
            act[...] = (g * _sigmoid(g) * (hu[...] + pu)).astype(BF16)

    def down_step(step):
        sl = step % MOE_RING
        y = _dot(act[...], ring[sl].astype(BF16))
        ys2 = ys_ref.reshape(tile_rows, LANES)
        s0 = (step - MOE_NA) * out_per
        for a in range(MOE_TILE // SUBLANES):
            tiles = [y[a * SUBLANES:(a + 1) * SUBLANES, k * LANES:(k + 1) * LANES]
                     for k in range(out_per)]
            rows = _transpose8(tiles)
            for b in range(SUBLANES):
                r0 = (a * SUBLANES + b) * ROW_CHUNKS + s0
                ys2[r0:r0 + out_per, :] = rows[b]

    @pl.when(i < nv)
    def _():
        @pl.when(i == 0)
        def _():
            for c in range(MOE_RING - 1):
                issue_chunk(0, c)
            issue_rows(0, 0, 0, MOE_TILE)

        pltpu.make_async_copy(x_hbm.at[pl.ds(0, tile_rows)], stage.at[slot], sem.at[slot]).wait()
        lax.fori_loop(0, MOE_TILE // (2 * SUBLANES), to_tiled, 0)
        has_next = i + 1 < nv

        for step in range(MOE_STEPS):
            ahead = step + MOE_RING - 1
            if ahead < MOE_STEPS:
                issue_chunk(i, ahead)
            else:
                pl.when(has_next)(functools.partial(issue_chunk, i + 1, ahead - MOE_STEPS))
            pl.when(has_next)(functools.partial(
                issue_rows, i + 1, 1 - slot, step * MOE_ROWS_PER_STEP, MOE_ROWS_PER_STEP))
            wait_chunk(step)
            if step < MOE_NA:
                up_gate_step(step)
            else:
                down_step(step)


def _moe_experts(te, nv, src, x1lin, w_gate, w_up, w_down, layer, n_tiles):
    per = MOE_NC // LANES

    def out_map(i, te, nv, src):
        return (jnp.minimum(i, nv[0] - 1), 0, 0)

    assert 2 * MOE_KC == EXPERT_FF and MOE_NC == EXPERT_FF
    any_spec = pl.BlockSpec(memory_space=pl.ANY)
    grid_spec = pltpu.PrefetchScalarGridSpec(
        num_scalar_prefetch=3,
        grid=(n_tiles,),
        in_specs=[any_spec, any_spec, any_spec, any_spec],
        out_specs=pl.BlockSpec((MOE_TILE, ROW_CHUNKS, LANES), out_map),
        scratch_shapes=[pltpu.VMEM((2, MOE_TILE * ROW_CHUNKS, LANES), F32),
                        pltpu.VMEM((MOE_NA, MOE_TILE, MOE_KC), BF16),
                        pltpu.VMEM((MOE_TILE, EXPERT_FF), F32),
                        pltpu.VMEM((MOE_TILE, EXPERT_FF), F32),
                        pltpu.VMEM((MOE_TILE, EXPERT_FF), BF16),
                        pltpu.VMEM((MOE_RING, EXPERT_FF, MOE_NC), F32),
                        pltpu.SemaphoreType.DMA((2,)),
                        pltpu.SemaphoreType.DMA((MOE_RING,))],
    )
    ys = pl.pallas_call(
        functools.partial(_moe_kernel, layer=layer),
        grid_spec=grid_spec,
        out_shape=jax.ShapeDtypeStruct((n_tiles * MOE_TILE, ROW_CHUNKS, LANES), F32),
        compiler_params=_params(("arbitrary",), 56),
        name="moe_experts",
    )(te, nv, src, x1lin,
      w_gate.reshape(DEPTH, N_EXPERTS, D_MODEL, EXPERT_FF),
      w_up.reshape(DEPTH, N_EXPERTS, D_MODEL, EXPERT_FF),
      w_down.reshape(DEPTH, N_EXPERTS, EXPERT_FF, D_MODEL))
    return ys


def _ple_kernel(x1b_ref, wpg_ref, p_ref, wpu_ref, x1_ref, o_ref):
    gate = _sigmoid(_dot(x1b_ref[...], wpg_ref[...].astype(BF16)))
    up = _dot(p_ref[...].astype(BF16), wpu_ref[...].astype(BF16))
    o_ref[...] = ALPHA * x1_ref[...] + gate * up


def _ple(x1b, w_pg, p, w_pu, layer, x1):
    t = x1.shape[0]
    tm, tn = 1024, 512
    return pl.pallas_call(
        _ple_kernel,
        grid=(t // tm, D_MODEL // tn),
        in_specs=[pl.BlockSpec((tm, D_MODEL), lambda i, j: (i, 0)),
                  pl.BlockSpec((None, D_MODEL, tn), lambda i, j: (layer, 0, j)),
                  pl.BlockSpec((None, tm, PLE_DIM), lambda i, j: (layer, i, 0)),
                  pl.BlockSpec((None, PLE_DIM, tn), lambda i, j: (layer, 0, j)),
                  pl.BlockSpec((tm, tn), lambda i, j: (i, j))],
        out_specs=pl.BlockSpec((tm, tn), lambda i, j: (i, j)),
        out_shape=jax.ShapeDtypeStruct((t, D_MODEL), F32),
        compiler_params=_params(("parallel", "parallel"), 52),
        name="ple",
    )(x1b, w_pg, p, w_pu, x1)


def _final_kernel(pos1_ref, pos2_ref, pre_ref, route_ref, lnw_ref, lnb_ref, ys_hbm,
                  o_ref, stage, vbuf, sem):
    i = pl.program_id(0)
    n = pl.num_programs(0)
    tm = pre_ref.shape[0]
    slot = lax.rem(i, 2)

    def issue(tile, sl):
        base = tile * tm

        def body(r, carry):
            dst = pl.ds(r * ROW_CHUNKS, ROW_CHUNKS)
            pltpu.make_async_copy(ys_hbm.at[pl.ds(pos1_ref[base + r] * ROW_CHUNKS, ROW_CHUNKS)],
                                  stage.at[sl, 0, dst], sem.at[sl]).start(priority=0)
            pltpu.make_async_copy(ys_hbm.at[pl.ds(pos2_ref[base + r] * ROW_CHUNKS, ROW_CHUNKS)],
                                  stage.at[sl, 1, dst], sem.at[sl]).start(priority=1)
            return carry

        lax.fori_loop(0, tm, body, 0)

    @pl.when(i == 0)
    def _():
        issue(0, 0)

    for k in range(2):
        pltpu.make_async_copy(ys_hbm.at[pl.ds(0, tm * ROW_CHUNKS)], stage.at[slot, k],
                              sem.at[slot]).wait()

    @pl.when(i + 1 < n)
    def _():
        issue(i + 1, 1 - slot)

    def combine(a, carry):
        r0 = pl.multiple_of(a * SUBLANES, SUBLANES)
        route = route_ref[pl.ds(r0, SUBLANES), :]
        w1 = jnp.broadcast_to(route[:, 2:3], (SUBLANES, LANES))
        w2 = jnp.broadcast_to(route[:, 3:4], (SUBLANES, LANES))
        for q in range(ROW_CHUNKS // SUBLANES):
            tiles = []
            for slot_half in range(2):
                rows = [stage[slot, slot_half, pl.ds(pl.multiple_of(
                    (r0 + b) * ROW_CHUNKS + q * SUBLANES, SUBLANES), SUBLANES), :]
                    for b in range(SUBLANES)]
                tiles.append(_transpose8(rows))
            for k in range(SUBLANES):
                s = q * SUBLANES + k
                vbuf[pl.ds(r0, SUBLANES), s * LANES:(s + 1) * LANES] = w1 * tiles[0][k] + w2 * tiles[1][k]
        return carry

    lax.fori_loop(0, tm // SUBLANES, combine, 0)
    o_ref[...] = _layer_norm(pre_ref[...] + vbuf[...], lnw_ref[...], lnb_ref[...])


def _final(pos1, pos2, pre2, route, ln_w, ln_b, ys_lin):
    t, d = pre2.shape
    tm = 256
    grid_spec = pltpu.PrefetchScalarGridSpec(
        num_scalar_prefetch=2,
        grid=(t // tm,),
        in_specs=[pl.BlockSpec((tm, d), lambda i, p1, p2: (i, 0)),
                  pl.BlockSpec((tm, ROUTE_LANES), lambda i, p1, p2: (i, 0)),
                  pl.BlockSpec((1, d), lambda i, p1, p2: (0, 0)),
                  pl.BlockSpec((1, d), lambda i, p1, p2: (0, 0)),
                  pl.BlockSpec(memory_space=pl.ANY)],
        out_specs=pl.BlockSpec((tm, d), lambda i, p1, p2: (i, 0)),
        scratch_shapes=[pltpu.VMEM((2, 2, tm * ROW_CHUNKS, LANES), F32),
                        pltpu.VMEM((tm, d), F32),
                        pltpu.SemaphoreType.DMA((2,))],
    )
    return pl.pallas_call(
        _final_kernel,
        grid_spec=grid_spec,
        out_shape=jax.ShapeDtypeStruct((t, d), F32),
        compiler_params=_params(("arbitrary",), 56),
        name="combine_ln2",
    )(pos1, pos2, pre2, route, ln_w, ln_b, ys_lin)


def _prep_q_weight(w_q_up):
    r = w_q_up.shape[0]
    wq = w_q_up.reshape(r, MLA_HEADS, MLA_NOPE + MLA_ROPE)
    z = jnp.zeros((r, MLA_HEADS, LANES - MLA_ROPE), w_q_up.dtype)
    return jnp.concatenate([wq, z], axis=-1).reshape(r, MLA_HEADS * 2 * LANES).astype(BF16)


def _layer(layer, x, p, tabs, batch, seq, w_in, q_norm, kv_norm, w_q_up, w_kv_up, sink, w_branch_a,
           w_branch_b, w_out, ln1_w, ln1_b, w_group, b_group, w_expert_router, b_expert,
           w_gate, w_up, w_down, w_ple_up, w_ple_gate, ln2_w, ln2_b):
    t = x.shape[0]
    cos_a, sin_a, cos_b, sin_b = tabs

    w_lat, w_sh = _prep_input_weight(w_in, layer)
    xb, cq, ckv, kr = _latent_proj(x, w_lat, q_norm[layer][None, :], kv_norm[layer][None, :],
                                   cos_a, sin_a)
    qf = _q_up(cq, _prep_q_weight(w_q_up[layer]), cos_a, sin_a)
    kf, v_a = _kv_up(ckv, w_kv_up, layer, kr)
    o_a = _mla_attention(qf, kf, v_a, batch, seq)

    qkv_b = _swa_proj(xb, w_sh, cos_b, sin_b)
    sink_rows = jnp.broadcast_to(
        jnp.repeat(sink[layer].astype(F32).reshape(SWA_KV_HEADS, SWA_GROUP), SWA_BLOCK, axis=1)[:, :, None],
        (SWA_KV_HEADS, SWA_GROUP * SWA_BLOCK, LANES))
    o_b = _swa_attention(qkv_b, sink_rows, batch, seq)

    merged = _branch_merge(xb, o_a, o_b, w_sh, w_branch_a, w_branch_b, layer)
    pre1 = _out_proj(merged, w_out, layer, x)

    pad = ROUTE_LANES - N_GROUPS - N_EXPERTS
    w_r = jnp.concatenate([w_group[layer], w_expert_router[layer], jnp.zeros((D_MODEL, pad), F32)],
                          axis=1)
    b_r = jnp.concatenate([b_group[layer], b_expert[layer], jnp.zeros((pad,), F32)])[None, :]
    x1, x1b, x1lin, route, route_t, counts = _ln_route(pre1, ln1_w[layer][None, :],
                                                       ln1_b[layer][None, :], w_r, b_r)

    n_tiles = (2 * t) // MOE_TILE + N_EXPERTS
    ri = route_t.astype(jnp.int32)
    cnt = counts[0, EXPERT_LANE0:EXPERT_LANE0 + N_EXPERTS].astype(jnp.int32)
    pos1, pos2, src, te, nv = _dispatch_plan(ri[0], ri[1], ri[4], ri[5], cnt, n_tiles)

    ys = _moe_experts(te, nv, src, x1lin, w_gate, w_up, w_down, layer, n_tiles)
    ys_lin = ys.reshape(n_tiles * MOE_TILE * ROW_CHUNKS, LANES)

    pre2 = _ple(x1b, w_ple_gate, p, w_ple_up, layer, x1)
    return _final(pos1, pos2, pre2, route, ln2_w[layer][None, :], ln2_b[layer][None, :], ys_lin)


def kernel(x, p, positions, w_in, q_norm, kv_norm, w_q_up, w_kv_up, sink, w_branch_a, w_branch_b,
           w_out, ln1_w, ln1_b, w_group, b_group, w_expert_router, b_expert, w_gate, w_up, w_down,
           w_ple_up, w_ple_gate, ln2_w, ln2_b):
    batch, seq, d = x.shape
    t = batch * seq
    h = x.reshape(t, d)
    tabs = _rope_tables(positions.reshape(t, 1))
    p2 = p.reshape(p.shape[0], t, PLE_DIM)
    for layer in range(w_in.shape[0]):
        h = _layer(layer, h, p2, tabs, batch, seq, w_in, q_norm, kv_norm, w_q_up, w_kv_up, sink,
                   w_branch_a, w_branch_b, w_out, ln1_w, ln1_b, w_group, b_group, w_expert_router,
                   b_expert, w_gate, w_up, w_down, w_ple_up, w_ple_gate, ln2_w, ln2_b)
    return h.reshape(batch, seq, d)
```

```python
import functools
import math

import numpy as np
import jax
import jax.numpy as jnp
from jax import lax
from jax.experimental import pallas as pl
from jax.experimental.pallas import tpu as pltpu

F32 = jnp.float32
BF16 = jnp.bfloat16

D_MODEL = 4096
MLA_HEADS = 16
MLA_NOPE = 128
MLA_ROPE = 64
MLA_V = 128
Q_RANK = 768
KV_RANK = 512
SWA_HEADS = 16
SWA_KV_HEADS = 4
SWA_GROUP = SWA_HEADS // SWA_KV_HEADS
SWA_DIM = 128
WINDOW = 128
ROPE_THETA = 10000.0
N_GROUPS = 8
EXPERTS_PER_GROUP = 8
N_EXPERTS = N_GROUPS * EXPERTS_PER_GROUP
EXPERT_FF = 1024
PLE_DIM = 256
LN_EPS = 1e-5
RMS_EPS = 1e-6
DEPTH = 1
ALPHA = (2.0 * DEPTH) ** 0.25
LOG2E = math.log2(math.e)

OFF_KR = Q_RANK + KV_RANK
OFF_QB = OFF_KR + MLA_ROPE
SWA_COLS = (SWA_HEADS + 2 * SWA_KV_HEADS) * SWA_DIM
IN_WIDTH = OFF_QB + SWA_COLS + 2 * D_MODEL

LANES = 128
MIB = 1024 * 1024

ROW_CHUNKS = D_MODEL // LANES

MOE_TILE = 336
MOE_KC = 512
MOE_NA = D_MODEL // MOE_KC
MOE_NC = 1024
MOE_NB = D_MODEL // MOE_NC
MOE_STEPS = MOE_NA + MOE_NB
MOE_ROWS_PER_STEP = MOE_TILE // MOE_STEPS
MOE_RING = 4
assert MOE_STEPS % MOE_RING == 0
ROUTE_LANES = LANES
EXPERT_LANE0 = N_GROUPS
ROUTE_FIELDS = 8


def _params(semantics, vmem_mib):
    return pltpu.CompilerParams(dimension_semantics=semantics,
                                vmem_limit_bytes=vmem_mib * MIB)


def _sigmoid(v):
    return 1.0 / (1.0 + jnp.exp(-v))


def _dot(a, b):
    return jnp.dot(a, b, preferred_element_type=F32)


SUBLANES = 8


def _transpose8(vs):
    sub = lax.broadcasted_iota(jnp.int32, (SUBLANES, LANES), 0)
    vs = list(vs)
    for d in (4, 2, 1):
        low = (sub & d) == 0
        for i in range(SUBLANES):
            if i & d:
                continue
            a, b = vs[i], vs[i + d]
            vs[i] = jnp.where(low, a, pltpu.roll(b, d, 0))
            vs[i + d] = jnp.where(low, pltpu.roll(a, SUBLANES - d, 0), b)
    return vs


def _rot_half_64(v):
    lane = lax.broadcasted_iota(jnp.int32, v.shape, 1)
    half = MLA_ROPE // 2
    fwd = pltpu.roll(v, half, 1)
    bwd = pltpu.roll(v, LANES - half, 1)
    return jnp.where(lane < half, -bwd, jnp.where(lane < MLA_ROPE, fwd, 0.0))


def _tables_kernel(pos_ref, inv_a_ref, inv_b_ref, sign_b_ref, cos_a, sin_a, cos_b, sin_b):
    pos = pos_ref[...].astype(F32)
    ang_a = pos * inv_a_ref[...]
    ang_b = pos * inv_b_ref[...]
    cos_a[...] = jnp.cos(ang_a)
    sin_a[...] = jnp.sin(ang_a)
    cos_b[...] = jnp.cos(ang_b)
    sin_b[...] = jnp.sin(ang_b) * sign_b_ref[...]


def _rope_tables(positions):
    t = positions.shape[0]
    tm = 1024
    inv64 = ROPE_THETA ** (-np.arange(0, MLA_ROPE, 2, dtype=np.float32) / MLA_ROPE)
    inv128 = ROPE_THETA ** (-np.arange(0, SWA_DIM, 2, dtype=np.float32) / SWA_DIM)
    inv_a = jnp.asarray(np.tile(inv64, 4)[None, :], F32)
    inv_b = jnp.asarray(np.tile(inv128, 2)[None, :], F32)
    sign_b = jnp.asarray(np.concatenate([-np.ones(64), np.ones(64)])[None, :], F32)
    row = pl.BlockSpec((tm, LANES), lambda i: (i, 0))
    const = pl.BlockSpec((1, LANES), lambda i: (0, 0))
    out = jax.ShapeDtypeStruct((t, LANES), F32)
    return pl.pallas_call(
        _tables_kernel,
        grid=(t // tm,),
        in_specs=[pl.BlockSpec((tm, 1), lambda i: (i, 0)), const, const, const],
        out_specs=[row, row, row, row],
        out_shape=[out, out, out, out],
        compiler_params=_params(("parallel",), 32),
        name="rope_tables",
    )(positions, inv_a, inv_b, sign_b)


LAT_COLS = OFF_KR + LANES
SHIFT_COLS = IN_WIDTH - OFF_QB
CAST_TILE = 1024


def _cast_transpose_kernel(w_ref, o_ref):
    o_ref[...] = jnp.transpose(w_ref[...]).astype(BF16)


def _prep_input_weight(w_in, layer):
    d = w_in.shape[1]
    w_t = jnp.swapaxes(w_in, 1, 2)
    tk = 512
    w_lat = pl.pallas_call(
        _cast_transpose_kernel,
        grid=(d // tk,),
        in_specs=[pl.BlockSpec((None, LAT_COLS, tk), lambda i: (layer, 0, i))],
        out_specs=pl.BlockSpec((tk, LAT_COLS), lambda i: (i, 0)),
        out_shape=jax.ShapeDtypeStruct((d, LAT_COLS), BF16),
        compiler_params=_params(("parallel",), 32),
        name="cast_latent_w",
    )(w_t)
    w_sh = pl.pallas_call(
        _cast_transpose_kernel,
        grid=(d // CAST_TILE, SHIFT_COLS // CAST_TILE),
        in_specs=[pl.BlockSpec((pl.Element(CAST_TILE), pl.Element(CAST_TILE)),
                               lambda i, j: (pl.multiple_of(OFF_QB + CAST_TILE * j, LANES // 2),
                                             pl.multiple_of(CAST_TILE * i, CAST_TILE)))],
        out_specs=pl.BlockSpec((CAST_TILE, CAST_TILE), lambda i, j: (i, j)),
        out_shape=jax.ShapeDtypeStruct((d, SHIFT_COLS), BF16),
        compiler_params=_params(("parallel", "parallel"), 32),
        name="cast_shift_w",
    )(w_t[layer])
    return w_lat, w_sh


def _latent_kernel(x_ref, w_ref, qn_ref, kvn_ref, cos_ref, sin_ref,
                   xb_ref, cq_ref, ckv_ref, kr_ref):
    xb = x_ref[...].astype(BF16)
    xb_ref[...] = xb
    acc = _dot(xb, w_ref[...])
    cq = acc[:, :Q_RANK]
    ckv = acc[:, Q_RANK:OFF_KR]
    cq = cq * lax.rsqrt(jnp.mean(cq * cq, axis=-1, keepdims=True) + RMS_EPS) * qn_ref[...]
    ckv = ckv * lax.rsqrt(jnp.mean(ckv * ckv, axis=-1, keepdims=True) + RMS_EPS) * kvn_ref[...]
    cq_ref[...] = cq.astype(BF16)
    ckv_ref[...] = ckv.astype(BF16)
    k = acc[:, OFF_KR:]
    lane = lax.broadcasted_iota(jnp.int32, k.shape, 1)
    kr = jnp.where(lane < MLA_ROPE, k * cos_ref[...], 0.0) + _rot_half_64(k) * sin_ref[...]
    kr_ref[...] = kr.astype(BF16)


def _latent_proj(x, w_lat, q_norm, kv_norm, cos_a, sin_a):
    t = x.shape[0]
    tm = 512
    row = lambda n: pl.BlockSpec((tm, n), lambda i: (i, 0))
    const = lambda r, n: pl.BlockSpec((r, n), lambda i: (0, 0))
    resident_w = pl.BlockSpec((D_MODEL, LAT_COLS), lambda i: (0, 0), pipeline_mode=pl.Buffered(1))
    return pl.pallas_call(
        _latent_kernel,
        grid=(t // tm,),
        in_specs=[row(D_MODEL), resident_w, const(1, Q_RANK), const(1, KV_RANK),
                  row(LANES), row(LANES)],
        out_specs=[row(D_MODEL), row(Q_RANK), row(KV_RANK), row(LANES)],
        out_shape=[jax.ShapeDtypeStruct((t, D_MODEL), BF16),
                   jax.ShapeDtypeStruct((t, Q_RANK), BF16),
                   jax.ShapeDtypeStruct((t, KV_RANK), BF16),
                   jax.ShapeDtypeStruct((t, LANES), BF16)],
        compiler_params=_params(("parallel",), 52),
        name="latent_proj",
    )(x, w_lat, q_norm, kv_norm, cos_a, sin_a)


UP_HEADS_PER_STEP = 4


def _qup_kernel(cq_ref, wa_ref, cos_ref, sin_ref, q_ref, *, scale):
    a = _dot(cq_ref[...], wa_ref[...])
    cos = cos_ref[...] * scale
    sin = sin_ref[...] * scale
    for h in range(UP_HEADS_PER_STEP):
        c0 = h * 2 * LANES
        r = a[:, c0 + LANES:c0 + 2 * LANES]
        q_ref[:, c0:c0 + LANES] = (a[:, c0:c0 + LANES] * scale).astype(BF16)
        q_ref[:, c0 + LANES:c0 + 2 * LANES] = (r * cos + _rot_half_64(r) * sin).astype(BF16)


def _q_up(cq, w_qa, cos_a, sin_a):
    t = cq.shape[0]
    tm = 1024
    tn = UP_HEADS_PER_STEP * 2 * LANES
    scale = float((MLA_NOPE + MLA_ROPE) ** -0.5 * LOG2E)
    return pl.pallas_call(
        functools.partial(_qup_kernel, scale=scale),
        grid=(t // tm, MLA_HEADS // UP_HEADS_PER_STEP),
        in_specs=[pl.BlockSpec((tm, Q_RANK), lambda i, j: (i, 0)),
                  pl.BlockSpec((Q_RANK, tn), lambda i, j: (0, j)),
                  pl.BlockSpec((tm, LANES), lambda i, j: (i, 0)),
                  pl.BlockSpec((tm, LANES), lambda i, j: (i, 0))],
        out_specs=pl.BlockSpec((tm, tn), lambda i, j: (i, j)),
        out_shape=jax.ShapeDtypeStruct((t, MLA_HEADS * 2 * LANES), BF16),
        compiler_params=_params(("parallel", "parallel"), 32),
        name="q_up",
    )(cq, w_qa, cos_a, sin_a)


def _kvup_kernel(ckv_ref, w_ref, kr_ref, k_ref, v_ref):
    acc = _dot(ckv_ref[...], w_ref[...].astype(BF16))
    kr = kr_ref[...]
    for h in range(UP_HEADS_PER_STEP):
        c0 = h * 2 * LANES
        k_ref[:, c0:c0 + LANES] = acc[:, c0:c0 + LANES].astype(BF16)
        k_ref[:, c0 + LANES:c0 + 2 * LANES] = kr
        v_ref[:, h * MLA_V:(h + 1) * MLA_V] = acc[:, c0 + LANES:c0 + 2 * LANES].astype(BF16)


def _kv_up(ckv, w_kv_up, layer, kr):
    t = ckv.shape[0]
    tm = 1024
    tn = UP_HEADS_PER_STEP * 2 * LANES
    return pl.pallas_call(
        _kvup_kernel,
        grid=(t // tm, MLA_HEADS // UP_HEADS_PER_STEP),
        in_specs=[pl.BlockSpec((tm, KV_RANK), lambda i, j: (i, 0)),
                  pl.BlockSpec((None, KV_RANK, tn), lambda i, j: (layer, 0, j)),
                  pl.BlockSpec((tm, LANES), lambda i, j: (i, 0))],
        out_specs=[pl.BlockSpec((tm, tn), lambda i, j: (i, j)),
                   pl.BlockSpec((tm, UP_HEADS_PER_STEP * MLA_V), lambda i, j: (i, j))],
        out_shape=[jax.ShapeDtypeStruct((t, MLA_HEADS * 2 * LANES), BF16),
                   jax.ShapeDtypeStruct((t, MLA_HEADS * MLA_V), BF16)],
        compiler_params=_params(("parallel", "parallel"), 32),
        name="kv_up",
    )(ckv, w_kv_up, kr)


SWA_TN = 512
SWA_Q_TILES = SWA_HEADS * SWA_DIM // SWA_TN
SWA_ROPE_TILES = (SWA_HEADS + SWA_KV_HEADS) * SWA_DIM // SWA_TN


def _swaproj_kernel(xb_ref, w_ref, cos_ref, sin_ref, o_ref, *, scale):
    j = pl.program_id(1)
    acc = _dot(xb_ref[...], w_ref[...])

    @pl.when(j < SWA_ROPE_TILES)
    def _():
        sc = jnp.where(j < SWA_Q_TILES, scale, 1.0).astype(F32)
        cos = cos_ref[...] * sc
        sin = sin_ref[...] * sc
        for c in range(SWA_TN // LANES):
            a = acc[:, c * LANES:(c + 1) * LANES]
            r = a * cos + pltpu.roll(a, SWA_DIM // 2, 1) * sin
            o_ref[:, c * LANES:(c + 1) * LANES] = r.astype(BF16)

    @pl.when(j >= SWA_ROPE_TILES)
    def _():
        o_ref[...] = acc.astype(BF16)


def _swa_proj(xb, w_sh, cos_b, sin_b):
    t = xb.shape[0]
    tm = 1024
    return pl.pallas_call(
        functools.partial(_swaproj_kernel, scale=float(SWA_DIM ** -0.5 * LOG2E)),
        grid=(t // tm, SWA_COLS // SWA_TN),
        in_specs=[pl.BlockSpec((tm, D_MODEL), lambda i, j: (i, 0)),
                  pl.BlockSpec((D_MODEL, SWA_TN), lambda i, j: (0, j)),
                  pl.BlockSpec((tm, LANES), lambda i, j: (i, 0)),
                  pl.BlockSpec((tm, LANES), lambda i, j: (i, 0))],
        out_specs=pl.BlockSpec((tm, SWA_TN), lambda i, j: (i, j)),
        out_shape=jax.ShapeDtypeStruct((t, SWA_COLS), BF16),
        compiler_params=_params(("parallel", "parallel"), 48),
        name="swa_proj",
    )(xb, w_sh, cos_b, sin_b)


MLA_HEADS_PER_STEP = 2


def _mla_kernel(q_ref, k_ref, v_ref, o_ref):
    for h in range(MLA_HEADS_PER_STEP):
        q = q_ref[:, h * 2 * LANES:(h + 1) * 2 * LANES]
        k = k_ref[:, h * 2 * LANES:(h + 1) * 2 * LANES]
        s = lax.dot_general(q, k, (((1,), (1,)), ((), ())), preferred_element_type=F32)
        m = jnp.max(s, axis=-1, keepdims=True)
        p = jnp.exp2(s - m)
        l = jnp.sum(p, axis=-1, keepdims=True)
        o = _dot(p.astype(BF16), v_ref[:, h * MLA_V:(h + 1) * MLA_V])
        o_ref[:, h * MLA_V:(h + 1) * MLA_V] = (o / l).astype(BF16)


def _mla_attention(qf, kf, v, batch, seq):
    t = qf.shape[0]
    tq = 1024
    nq = seq // tq
    hp = MLA_HEADS_PER_STEP
    return pl.pallas_call(
        _mla_kernel,
        grid=(batch, MLA_HEADS // hp, nq),
        in_specs=[pl.BlockSpec((tq, hp * 2 * LANES), lambda b, h, i: (b * nq + i, h)),
                  pl.BlockSpec((seq, hp * 2 * LANES), lambda b, h, i: (b, h)),
                  pl.BlockSpec((seq, hp * MLA_V), lambda b, h, i: (b, h))],
        out_specs=pl.BlockSpec((tq, hp * MLA_V), lambda b, h, i: (b * nq + i, h)),
        out_shape=jax.ShapeDtypeStruct((t, MLA_HEADS * MLA_V), BF16),
        compiler_params=_params(("parallel", "parallel", "parallel"), 52),
        name="mla_attention",
    )(qf, kf, v)


SWA_BLOCK = 128
SWA_TQ = 512
SWA_WIN = SWA_TQ + 2 * WINDOW


def _swa_kernel(q_ref, k_ref, v_ref, sink_ref, bias_ref, o_ref, *, seq):
    step = pl.program_id(2)
    q0 = step * SWA_TQ
    ks = pl.multiple_of(jnp.clip(q0 - WINDOW, 0, seq - SWA_WIN), WINDOW)
    bias = bias_ref[lax.div(q0 - ks, WINDOW)]
    kw = k_ref[pl.ds(ks, SWA_WIN), :]
    vw = v_ref[pl.ds(ks, SWA_WIN), :]
    for g in range(SWA_GROUP):
        q = q_ref[:, g * SWA_DIM:(g + 1) * SWA_DIM]
        s = lax.dot_general(q, kw, (((1,), (1,)), ((), ())), preferred_element_type=F32) + bias
        sink = sink_ref[g * SWA_BLOCK:g * SWA_BLOCK + 1, :1] * LOG2E
        m = jnp.maximum(jnp.max(s, axis=-1, keepdims=True), sink)
        e = jnp.exp2(s - m)
        denom = jnp.sum(e, axis=-1, keepdims=True) + jnp.exp2(sink - m)
        o = _dot(e.astype(BF16), vw) / denom
        o_ref[:, g * SWA_DIM:(g + 1) * SWA_DIM] = o.astype(BF16)


def _swa_band_bias():
    r = np.arange(SWA_TQ)[:, None]
    c = np.arange(SWA_WIN)[None, :]
    variants = [np.where(np.abs(v * WINDOW + r - c) <= WINDOW, 0.0, -np.inf) for v in range(3)]
    return jnp.asarray(np.stack(variants), F32)


def _swa_attention(qkv, sink_rows, batch, seq):
    t = qkv.shape[0]
    nq = seq // SWA_TQ
    gw = SWA_GROUP * SWA_DIM
    k_col0 = SWA_HEADS
    v_col0 = SWA_HEADS + SWA_KV_HEADS
    return pl.pallas_call(
        functools.partial(_swa_kernel, seq=seq),
        grid=(batch, SWA_KV_HEADS, nq),
        in_specs=[pl.BlockSpec((SWA_TQ, gw), lambda b, h, n: (b * nq + n, h)),
                  pl.BlockSpec((seq, SWA_DIM), lambda b, h, n: (b, k_col0 + h)),
                  pl.BlockSpec((seq, SWA_DIM), lambda b, h, n: (b, v_col0 + h)),
                  pl.BlockSpec((None, SWA_GROUP * SWA_BLOCK, LANES), lambda b, h, n: (h, 0, 0)),
                  pl.BlockSpec((3, SWA_TQ, SWA_WIN), lambda b, h, n: (0, 0, 0))],
        out_specs=pl.BlockSpec((SWA_TQ, gw), lambda b, h, n: (b * nq + n, h)),
        out_shape=jax.ShapeDtypeStruct((t, SWA_HEADS * SWA_DIM), BF16),
        compiler_params=_params(("parallel", "parallel", "parallel"), 48),
        name="swa_attention",
    )(qkv, qkv, qkv, sink_rows, _swa_band_bias())


MERGE_TN = 256
GATE_A_BLOCK0 = SWA_COLS // MERGE_TN
GATE_B_BLOCK0 = (SWA_COLS + D_MODEL) // MERGE_TN


def _merge_kernel(xb_ref, oa_ref, ob_ref, wga_ref, wgb_ref, wa_ref, wb_ref, o_ref):
    xb = xb_ref[...]
    ga = _sigmoid(_dot(xb, wga_ref[...]))
    ya = _dot(oa_ref[...], wa_ref[...].astype(BF16))
    acc = ga * ya
    gb = _sigmoid(_dot(xb, wgb_ref[...]))
    yb = _dot(ob_ref[...], wb_ref[...].astype(BF16))
    o_ref[...] = (acc + gb * yb).astype(BF16)


def _branch_merge(xb, o_a, o_b, w_sh, w_a, w_b, layer):
    t = xb.shape[0]
    tm, tn = 512, MERGE_TN
    ka = o_a.shape[1]
    row = lambda k: pl.BlockSpec((tm, k), lambda i, j: (i, 0))
    return pl.pallas_call(
        _merge_kernel,
        grid=(t // tm, D_MODEL // tn),
        in_specs=[row(D_MODEL), row(ka), row(ka),
                  pl.BlockSpec((D_MODEL, tn), lambda i, j: (0, GATE_A_BLOCK0 + j)),
                  pl.BlockSpec((D_MODEL, tn), lambda i, j: (0, GATE_B_BLOCK0 + j)),
                  pl.BlockSpec((None, ka, tn), lambda i, j: (layer, 0, j)),
                  pl.BlockSpec((None, ka, tn), lambda i, j: (layer, 0, j))],
        out_specs=pl.BlockSpec((tm, tn), lambda i, j: (i, j)),
        out_shape=jax.ShapeDtypeStruct((t, D_MODEL), BF16),
        compiler_params=_params(("parallel", "parallel"), 48),
        name="branch_merge",
    )(xb, o_a, o_b, w_sh, w_sh, w_a, w_b)


def _outproj_kernel(m_ref, w_ref, x_ref, o_ref):
    o_ref[...] = ALPHA * x_ref[...] + _dot(m_ref[...], w_ref[...].astype(BF16))


def _out_proj(merged, w_out, layer, x):
    t = x.shape[0]
    tm, tn = 1024, 512
    return pl.pallas_call(
        _outproj_kernel,
        grid=(t // tm, D_MODEL // tn),
        in_specs=[pl.BlockSpec((tm, D_MODEL), lambda i, j: (i, 0)),
                  pl.BlockSpec((None, D_MODEL, tn), lambda i, j: (layer, 0, j)),
                  pl.BlockSpec((tm, tn), lambda i, j: (i, j))],
        out_specs=pl.BlockSpec((tm, tn), lambda i, j: (i, j)),
        out_shape=jax.ShapeDtypeStruct((t, D_MODEL), F32),
        compiler_params=_params(("parallel", "parallel"), 52),
        name="out_proj",
    )(merged, w_out, x)


def _layer_norm(v, w, b):
    mu = jnp.mean(v, axis=-1, keepdims=True)
    c = v - mu
    var = jnp.mean(c * c, axis=-1, keepdims=True)
    return c * lax.rsqrt(var + LN_EPS) * w + b


def _split_bf16(v):
    hi = v.astype(BF16)
    lo = (v - hi.astype(F32)).astype(BF16)
    return hi, lo


def _ln_route_kernel(pre_ref, lnw_ref, lnb_ref, wr_ref, br_ref,
                     x1_ref, x1b_ref, x1lin_ref, route_ref, route_t_ref, cnt_ref, carry):
    i = pl.program_id(0)

    @pl.when(i == 0)
    def _():
        carry[...] = jnp.zeros_like(carry)

    x1 = _layer_norm(pre_ref[...], lnw_ref[...], lnb_ref[...])
    tm = x1.shape[0]
    x1_ref[...] = x1
    x1b_ref[...] = x1.astype(BF16)

    def to_row_linear(a, carry):
        r0 = pl.multiple_of(a * SUBLANES, SUBLANES)
        pack = 2 * SUBLANES
        for qq in range(ROW_CHUNKS // pack):
            parts = []
            for q in (2 * qq, 2 * qq + 1):
                tiles = [x1_ref[pl.ds(r0, SUBLANES),
                                (q * SUBLANES + k) * LANES:(q * SUBLANES + k + 1) * LANES]
                         for k in range(SUBLANES)]
                parts.append(_transpose8(tiles))
            for b in range(SUBLANES):
                dst = pl.multiple_of((r0 + b) * ROW_CHUNKS + qq * pack, pack)
                x1lin_ref[pl.ds(dst, pack), :] = jnp.concatenate(
                    [parts[0][b], parts[1][b]], axis=0).astype(BF16)
        return carry

    lax.fori_loop(0, tm // SUBLANES, to_row_linear, 0)

    xh, xl = _split_bf16(x1)
    wh, wl = _split_bf16(wr_ref[...])
    logits = _dot(xh, wh) + (_dot(xl, wh) + _dot(xh, wl)) + br_ref[...]

    lane = lax.broadcasted_iota(jnp.int32, logits.shape, 1)
    neg = -jnp.inf
    gl = jnp.where(lane < N_GROUPS, logits, neg)
    gmax = jnp.max(gl, axis=-1, keepdims=True)
    gsum = jnp.sum(jnp.exp(gl - gmax), axis=-1, keepdims=True)
    g_w = 1.0 / gsum
    g_idx = jnp.min(jnp.where(gl == gmax, lane, ROUTE_LANES), axis=-1, keepdims=True)
    d = lane - (EXPERT_LANE0 + g_idx * EXPERTS_PER_GROUP)
    el = jnp.where((d >= 0) & (d < EXPERTS_PER_GROUP), logits, neg)
    m1 = jnp.max(el, axis=-1, keepdims=True)
    i1 = jnp.min(jnp.where(el == m1, lane, ROUTE_LANES), axis=-1, keepdims=True)
    el2 = jnp.where(lane == i1, neg, el)
    m2 = jnp.max(el2, axis=-1, keepdims=True)
    i2 = jnp.min(jnp.where(el2 == m2, lane, ROUTE_LANES), axis=-1, keepdims=True)
    e21 = jnp.exp(m2 - m1)
    w1 = g_w / (1.0 + e21)
    w2 = g_w * e21 / (1.0 + e21)

    oh1 = (lane == i1).astype(F32)
    oh2 = (lane == i2).astype(F32)
    oh = oh1 + oh2
    r_i = lax.broadcasted_iota(jnp.int32, (tm, tm), 0)
    c_i = lax.broadcasted_iota(jnp.int32, (tm, tm), 1)
    tri = jnp.where(c_i < r_i, 1.0, 0.0).astype(BF16)
    before = _dot(tri, oh.astype(BF16)) + carry[...]
    rank1 = jnp.sum(oh1 * before, axis=-1, keepdims=True)
    rank2 = jnp.sum(oh2 * before, axis=-1, keepdims=True)
    carry[...] = carry[...] + jnp.sum(oh, axis=0, keepdims=True)
    cnt_ref[...] = carry[...]

    slab = jnp.where(lane == 0, (i1 - EXPERT_LANE0).astype(F32), 0.0)
    slab = jnp.where(lane == 1, (i2 - EXPERT_LANE0).astype(F32), slab)
    slab = jnp.where(lane == 2, w1, slab)
    slab = jnp.where(lane == 3, w2, slab)
    slab = jnp.where(lane == 4, rank1, slab)
    slab = jnp.where(lane == 5, rank2, slab)
    slab = jnp.where(lane == 6, (i1 - EXPERT_LANE0).astype(F32) * float(1 << SLOT_SHIFT) + rank1, slab)
    slab = jnp.where(lane == 7, (i2 - EXPERT_LANE0).astype(F32) * float(1 << SLOT_SHIFT) + rank2, slab)
    route_ref[...] = slab
    route_t_ref[...] = jnp.transpose(slab)[:ROUTE_FIELDS, :]


def _ln_route(pre1, ln_w, ln_b, w_r, b_r):
    t = pre1.shape[0]
    tm = 256
    row = lambda n: pl.BlockSpec((tm, n), lambda i: (i, 0))
    const = lambda r, n: pl.BlockSpec((r, n), lambda i: (0, 0))
    return pl.pallas_call(
        _ln_route_kernel,
        grid=(t // tm,),
        in_specs=[row(D_MODEL), const(1, D_MODEL), const(1, D_MODEL),
                  const(D_MODEL, ROUTE_LANES), const(1, ROUTE_LANES)],
        out_specs=[row(D_MODEL), row(D_MODEL),
                   pl.BlockSpec((tm * ROW_CHUNKS, LANES), lambda i: (i, 0)),
                   row(ROUTE_LANES),
                   pl.BlockSpec((ROUTE_FIELDS, tm), lambda i: (0, i)),
                   const(1, ROUTE_LANES)],
        out_shape=[jax.ShapeDtypeStruct((t, D_MODEL), F32),
                   jax.ShapeDtypeStruct((t, D_MODEL), BF16),
                   jax.ShapeDtypeStruct((t * ROW_CHUNKS, LANES), BF16),
                   jax.ShapeDtypeStruct((t, ROUTE_LANES), F32),
                   jax.ShapeDtypeStruct((ROUTE_FIELDS, t), F32),
                   jax.ShapeDtypeStruct((1, ROUTE_LANES), F32)],
        scratch_shapes=[pltpu.VMEM((1, ROUTE_LANES), F32)],
        compiler_params=_params(("arbitrary",), 52),
        name="ln1_route",
    )(pre1, ln_w, ln_b, w_r, b_r)


SLOT_SHIFT = 16
SLOT_MASK = (1 << SLOT_SHIFT) - 1


def _plan_kernel(c1_ref, c2_ref, cnt_ref,
                 pos_ref, src_ref, te_ref, nv_ref, tstart, *, n_tokens, n_tiles):
    def per_expert(e, first_tile):
        tstart[e] = first_tile
        c = cnt_ref[e]
        nt = lax.div(c + (MOE_TILE - 1), MOE_TILE)

        def set_te(k, carry):
            te_ref[first_tile + k] = e
            return carry

        lax.fori_loop(0, nt, set_te, 0)

        def pad_src(k, carry):
            src_ref[first_tile * MOE_TILE + k] = 0
            return carry

        lax.fori_loop(c, nt * MOE_TILE, pad_src, 0)
        return first_tile + nt

    nv = lax.fori_loop(0, N_EXPERTS, per_expert, 0)
    nv_ref[0] = nv
    last = te_ref[nv - 1]

    def tail_te(k, carry):
        te_ref[k] = last
        return carry

    lax.fori_loop(nv, n_tiles, tail_te, 0)

    def per_token(t, carry):
        c1 = c1_ref[t]
        c2 = c2_ref[t]
        p1 = tstart[lax.shift_right_logical(c1, SLOT_SHIFT)] * MOE_TILE + (c1 & SLOT_MASK)
        p2 = tstart[lax.shift_right_logical(c2, SLOT_SHIFT)] * MOE_TILE + (c2 & SLOT_MASK)
        pos_ref[t] = p1 | lax.shift_left(p2, SLOT_SHIFT)
        src_ref[p1] = t
        src_ref[p2] = t
        return carry

    lax.fori_loop(0, n_tokens, per_token, 0, unroll=8)


def _dispatch_plan(c1, c2, cnt, n_tiles):
    t = c1.shape[0]
    assert n_tiles * MOE_TILE <= SLOT_MASK + 1 and 2 * t <= SLOT_MASK + 1
    smem = pl.BlockSpec(memory_space=pltpu.SMEM)
    grid_spec = pltpu.PrefetchScalarGridSpec(
        num_scalar_prefetch=3,
        grid=(1,),
        in_specs=[],
        out_specs=[smem, smem, smem, smem],
        scratch_shapes=[pltpu.SMEM((N_EXPERTS,), jnp.int32)],
    )
    i32 = lambda n: jax.ShapeDtypeStruct((n,), jnp.int32)
    return pl.pallas_call(
        functools.partial(_plan_kernel, n_tokens=t, n_tiles=n_tiles),
        grid_spec=grid_spec,
        out_shape=[i32(t), i32(n_tiles * MOE_TILE), i32(n_tiles), i32(1)],
        compiler_params=_params(("arbitrary",), 32),
        name="dispatch_plan",
    )(c1, c2, cnt)


def _moe_kernel(te_ref, nv_ref, src_ref, x_hbm, wg_hbm, wu_hbm, wd_hbm, ys_ref,
                stage, xb, hg, hu, act, ring, sem, wsem, *, layer):
    i = pl.program_id(0)
    nv = nv_ref[0]
    slot = lax.rem(i, 2)
    tile_rows = MOE_TILE * ROW_CHUNKS
    half_ff = EXPERT_FF // 2
    out_per = MOE_NC // LANES

    def issue_chunk(tile, step):
        sl = step % MOE_RING
        e = te_ref[tile]
        if step < MOE_NA:
            rows = pl.ds(step * MOE_KC, MOE_KC)
            pltpu.make_async_copy(wg_hbm.at[layer, e, rows, :],
                                  ring.at[sl, pl.ds(0, MOE_KC), :], wsem.at[sl]).start(priority=0)
            pltpu.make_async_copy(wu_hbm.at[layer, e, rows, :],
                                  ring.at[sl, pl.ds(MOE_KC, MOE_KC), :], wsem.at[sl]).start(priority=1)
        else:
            cols = pl.ds((step - MOE_NA) * MOE_NC, MOE_NC)
            for k in range(2):
                rows = pl.ds(k * half_ff, half_ff)
                pltpu.make_async_copy(wd_hbm.at[layer, e, rows, cols],
                                      ring.at[sl, rows, :], wsem.at[sl]).start(priority=k)

    def wait_chunk(step):
        sl = step % MOE_RING
        pltpu.make_async_copy(wd_hbm.at[layer, 0, :, pl.ds(0, MOE_NC)], ring.at[sl],
                              wsem.at[sl]).wait()

    def row_copy(tile, sl, r):
        tok = src_ref[tile * MOE_TILE + r]
        return pltpu.make_async_copy(x_hbm.at[pl.ds(tok * ROW_CHUNKS, ROW_CHUNKS)],
                                     stage.at[sl, pl.ds(r * ROW_CHUNKS, ROW_CHUNKS)], sem.at[sl])

    def issue_rows(tile, sl, r0, n):
        def body(k, carry):
            row_copy(tile, sl, r0 + 2 * k).start(priority=0)
            row_copy(tile, sl, r0 + 2 * k + 1).start(priority=1)
            return carry

        lax.fori_loop(0, n // 2, body, 0)

    def to_tiled(a, carry):
        per = MOE_KC // LANES
        pair = 2 * SUBLANES
        t0 = pl.multiple_of(a * pair, pair)
        for qq in range(ROW_CHUNKS // pair):
            halves = []
            for h in range(2):
                blks = [stage[slot, pl.ds(pl.multiple_of(
                    (t0 + h * SUBLANES + b) * ROW_CHUNKS + qq * pair, pair), pair), :].astype(F32)
                    for b in range(SUBLANES)]
                halves.append(_transpose8([v[:SUBLANES] for v in blks])
                              + _transpose8([v[SUBLANES:] for v in blks]))
            for k in range(pair):
                s = qq * pair + k
                blk = jnp.concatenate([halves[0][k], halves[1][k]], axis=0).astype(BF16)
                xb[s // per, pl.ds(t0, pair), (s % per) * LANES:(s % per + 1) * LANES] = blk
        return carry

    def up_gate_step(step):
        sl = step % MOE_RING
        xk = xb[step]
        pg = _dot(xk, ring[sl, 0:MOE_KC, :].astype(BF16))
        pu = _dot(xk, ring[sl, MOE_KC:2 * MOE_KC, :].astype(BF16))
        if step == 0:
            hg[...] = pg
            hu[...] = pu
        elif step < MOE_NA - 1:
            hg[...] += pg
            hu[...] += pu
        else:
            g = hg[...] + pg
            act[...] = (g * _sigmoid(g) * (hu[...] + pu)).astype(BF16)

    def down_step(step):
        sl = step % MOE_RING
        y = _dot(act[...], ring[sl].astype(BF16))
        ys2 = ys_ref.reshape(tile_rows, LANES)
        s0 = (step - MOE_NA) * out_per
        for a in range(MOE_TILE // SUBLANES):
            tiles = [y[a * SUBLANES:(a + 1) * SUBLANES, k * LANES:(k + 1) * LANES]
                     for k in range(out_per)]
            rows = _transpose8(tiles)
            for b in range(SUBLANES):
                r0 = (a * SUBLANES + b) * ROW_CHUNKS + s0
                ys2[r0:r0 + out_per, :] = rows[b]

    @pl.when(i < nv)
    def _():
        @pl.when(i == 0)
        def _():
            for c in range(MOE_RING - 1):
                issue_chunk(0, c)
            issue_rows(0, 0, 0, MOE_TILE)

        pltpu.make_async_copy(x_hbm.at[pl.ds(0, tile_rows)], stage.at[slot], sem.at[slot]).wait()
        lax.fori_loop(0, MOE_TILE // (2 * SUBLANES), to_tiled, 0)
        has_next = i + 1 < nv

        for step in range(MOE_STEPS):
            ahead = step + MOE_RING - 1
            if ahead < MOE_STEPS:
                issue_chunk(i, ahead)
            else:
                pl.when(has_next)(functools.partial(issue_chunk, i + 1, ahead - MOE_STEPS))
            pl.when(has_next)(functools.partial(
                issue_rows, i + 1, 1 - slot, step * MOE_ROWS_PER_STEP, MOE_ROWS_PER_STEP))
            wait_chunk(step)
            if step < MOE_NA:
                up_gate_step(step)
            else:
                down_step(step)


def _moe_experts(te, nv, src, x1lin, w_gate, w_up, w_down, layer, n_tiles):
    per = MOE_NC // LANES

    def out_map(i, te, nv, src):
        return (jnp.minimum(i, nv[0] - 1), 0, 0)

    assert 2 * MOE_KC == EXPERT_FF and MOE_NC == EXPERT_FF
    any_spec = pl.BlockSpec(memory_space=pl.ANY)
    grid_spec = pltpu.PrefetchScalarGridSpec(
        num_scalar_prefetch=3,
        grid=(n_tiles,),
        in_specs=[any_spec, any_spec, any_spec, any_spec],
        out_specs=pl.BlockSpec((MOE_TILE, ROW_CHUNKS, LANES), out_map),
        scratch_shapes=[pltpu.VMEM((2, MOE_TILE * ROW_CHUNKS, LANES), BF16),
                        pltpu.VMEM((MOE_NA, MOE_TILE, MOE_KC), BF16),
                        pltpu.VMEM((MOE_TILE, EXPERT_FF), F32),
                        pltpu.VMEM((MOE_TILE, EXPERT_FF), F32),
                        pltpu.VMEM((MOE_TILE, EXPERT_FF), BF16),
                        pltpu.VMEM((MOE_RING, EXPERT_FF, MOE_NC), F32),
                        pltpu.SemaphoreType.DMA((2,)),
                        pltpu.SemaphoreType.DMA((MOE_RING,))],
    )
    ys = pl.pallas_call(
        functools.partial(_moe_kernel, layer=layer),
        grid_spec=grid_spec,
        out_shape=jax.ShapeDtypeStruct((n_tiles * MOE_TILE, ROW_CHUNKS, LANES), F32),
        compiler_params=_params(("arbitrary",), 56),
        name="moe_experts",
    )(te, nv, src, x1lin,
      w_gate.reshape(DEPTH, N_EXPERTS, D_MODEL, EXPERT_FF),
      w_up.reshape(DEPTH, N_EXPERTS, D_MODEL, EXPERT_FF),
      w_down.reshape(DEPTH, N_EXPERTS, EXPERT_FF, D_MODEL))
    return ys


def _ple_kernel(x1b_ref, wpg_ref, p_ref, wpu_ref, x1_ref, o_ref):
    gate = _sigmoid(_dot(x1b_ref[...], wpg_ref[...].astype(BF16)))
    up = _dot(p_ref[...].astype(BF16), wpu_ref[...].astype(BF16))
    o_ref[...] = ALPHA * x1_ref[...] + gate * up


def _ple(x1b, w_pg, p, w_pu, layer, x1):
    t = x1.shape[0]
    tm, tn = 1024, 512
    return pl.pallas_call(
        _ple_kernel,
        grid=(t // tm, D_MODEL // tn),
        in_specs=[pl.BlockSpec((tm, D_MODEL), lambda i, j: (i, 0)),
                  pl.BlockSpec((None, D_MODEL, tn), lambda i, j: (layer, 0, j)),
                  pl.BlockSpec((None, tm, PLE_DIM), lambda i, j: (layer, i, 0)),
                  pl.BlockSpec((None, PLE_DIM, tn), lambda i, j: (layer, 0, j)),
                  pl.BlockSpec((tm, tn), lambda i, j: (i, j))],
        out_specs=pl.BlockSpec((tm, tn), lambda i, j: (i, j)),
        out_shape=jax.ShapeDtypeStruct((t, D_MODEL), F32),
        compiler_params=_params(("parallel", "parallel"), 52),
        name="ple",
    )(x1b, w_pg, p, w_pu, x1)


def _final_kernel(pos_ref, pre_ref, route_ref, lnw_ref, lnb_ref, ys_hbm,
                  o_ref, stage, vbuf, sem):
    i = pl.program_id(0)
    n = pl.num_programs(0)
    tm = pre_ref.shape[0]
    slot = lax.rem(i, 2)

    def issue(tile, sl):
        base = tile * tm

        def body(r, carry):
            dst = pl.ds(r * ROW_CHUNKS, ROW_CHUNKS)
            packed = pos_ref[base + r]
            p1 = packed & SLOT_MASK
            p2 = lax.shift_right_logical(packed, SLOT_SHIFT)
            pltpu.make_async_copy(ys_hbm.at[pl.ds(p1 * ROW_CHUNKS, ROW_CHUNKS)],
                                  stage.at[sl, 0, dst], sem.at[sl]).start(priority=0)
            pltpu.make_async_copy(ys_hbm.at[pl.ds(p2 * ROW_CHUNKS, ROW_CHUNKS)],
                                  stage.at[sl, 1, dst], sem.at[sl]).start(priority=1)
            return carry

        lax.fori_loop(0, tm, body, 0)

    @pl.when(i == 0)
    def _():
        issue(0, 0)

    for k in range(2):
        pltpu.make_async_copy(ys_hbm.at[pl.ds(0, tm * ROW_CHUNKS)], stage.at[slot, k],
                              sem.at[slot]).wait()

    @pl.when(i + 1 < n)
    def _():
        issue(i + 1, 1 - slot)

    def combine(a, carry):
        r0 = pl.multiple_of(a * SUBLANES, SUBLANES)
        route = route_ref[pl.ds(r0, SUBLANES), :]
        w1 = jnp.broadcast_to(route[:, 2:3], (SUBLANES, LANES))
        w2 = jnp.broadcast_to(route[:, 3:4], (SUBLANES, LANES))
        for q in range(ROW_CHUNKS // SUBLANES):
            tiles = []
            for slot_half in range(2):
                rows = [stage[slot, slot_half, pl.ds(pl.multiple_of(
                    (r0 + b) * ROW_CHUNKS + q * SUBLANES, SUBLANES), SUBLANES), :]
                    for b in range(SUBLANES)]
                tiles.append(_transpose8(rows))
            for k in range(SUBLANES):
                s = q * SUBLANES + k
                vbuf[pl.ds(r0, SUBLANES), s * LANES:(s + 1) * LANES] = w1 * tiles[0][k] + w2 * tiles[1][k]
        return carry

    lax.fori_loop(0, tm // SUBLANES, combine, 0)
    o_ref[...] = _layer_norm(pre_ref[...] + vbuf[...], lnw_ref[...], lnb_ref[...])


def _final(pos, pre2, route, ln_w, ln_b, ys_lin):
    t, d = pre2.shape
    tm = 256
    grid_spec = pltpu.PrefetchScalarGridSpec(
        num_scalar_prefetch=1,
        grid=(t // tm,),
        in_specs=[pl.BlockSpec((tm, d), lambda i, pos: (i, 0)),
                  pl.BlockSpec((tm, ROUTE_LANES), lambda i, pos: (i, 0)),
                  pl.BlockSpec((1, d), lambda i, pos: (0, 0)),
                  pl.BlockSpec((1, d), lambda i, pos: (0, 0)),
                  pl.BlockSpec(memory_space=pl.ANY)],
        out_specs=pl.BlockSpec((tm, d), lambda i, pos: (i, 0)),
        scratch_shapes=[pltpu.VMEM((2, 2, tm * ROW_CHUNKS, LANES), F32),
                        pltpu.VMEM((tm, d), F32),
                        pltpu.SemaphoreType.DMA((2,))],
    )
    return pl.pallas_call(
        _final_kernel,
        grid_spec=grid_spec,
        out_shape=jax.ShapeDtypeStruct((t, d), F32),
        compiler_params=_params(("arbitrary",), 56),
        name="combine_ln2",
    )(pos, pre2, route, ln_w, ln_b, ys_lin)


def _prep_q_weight(w_q_up):
    r = w_q_up.shape[0]
    wq = w_q_up.reshape(r, MLA_HEADS, MLA_NOPE + MLA_ROPE)
    z = jnp.zeros((r, MLA_HEADS, LANES - MLA_ROPE), w_q_up.dtype)
    return jnp.concatenate([wq, z], axis=-1).reshape(r, MLA_HEADS * 2 * LANES).astype(BF16)


def _layer(layer, x, p, tabs, batch, seq, w_in, q_norm, kv_norm, w_q_up, w_kv_up, sink, w_branch_a,
           w_branch_b, w_out, ln1_w, ln1_b, w_group, b_group, w_expert_router, b_expert,
           w_gate, w_up, w_down, w_ple_up, w_ple_gate, ln2_w, ln2_b):
    t = x.shape[0]
    cos_a, sin_a, cos_b, sin_b = tabs

    w_lat, w_sh = _prep_input_weight(w_in, layer)
    xb, cq, ckv, kr = _latent_proj(x, w_lat, q_norm[layer][None, :], kv_norm[layer][None, :],
                                   cos_a, sin_a)
    qf = _q_up(cq, _prep_q_weight(w_q_up[layer]), cos_a, sin_a)
    kf, v_a = _kv_up(ckv, w_kv_up, layer, kr)
    o_a = _mla_attention(qf, kf, v_a, batch, seq)

    qkv_b = _swa_proj(xb, w_sh, cos_b, sin_b)
    sink_rows = jnp.broadcast_to(
        jnp.repeat(sink[layer].astype(F32).reshape(SWA_KV_HEADS, SWA_GROUP), SWA_BLOCK, axis=1)[:, :, None],
        (SWA_KV_HEADS, SWA_GROUP * SWA_BLOCK, LANES))
    o_b = _swa_attention(qkv_b, sink_rows, batch, seq)

    merged = _branch_merge(xb, o_a, o_b, w_sh, w_branch_a, w_branch_b, layer)
    pre1 = _out_proj(merged, w_out, layer, x)

    pad = ROUTE_LANES - N_GROUPS - N_EXPERTS
    w_r = jnp.concatenate([w_group[layer], w_expert_router[layer], jnp.zeros((D_MODEL, pad), F32)],
                          axis=1)
    b_r = jnp.concatenate([b_group[layer], b_expert[layer], jnp.zeros((pad,), F32)])[None, :]
    x1, x1b, x1lin, route, route_t, counts = _ln_route(pre1, ln1_w[layer][None, :],
                                                       ln1_b[layer][None, :], w_r, b_r)

    n_tiles = (2 * t) // MOE_TILE + N_EXPERTS
    ri = route_t.astype(jnp.int32)
    cnt = counts[0, EXPERT_LANE0:EXPERT_LANE0 + N_EXPERTS].astype(jnp.int32)
    pos, src, te, nv = _dispatch_plan(ri[6], ri[7], cnt, n_tiles)

    ys = _moe_experts(te, nv, src, x1lin, w_gate, w_up, w_down, layer, n_tiles)
    ys_lin = ys.reshape(n_tiles * MOE_TILE * ROW_CHUNKS, LANES)

    pre2 = _ple(x1b, w_ple_gate, p, w_ple_up, layer, x1)
    return _final(pos, pre2, route, ln2_w[layer][None, :], ln2_b[layer][None, :], ys_lin)


def kernel(x, p, positions, w_in, q_norm, kv_norm, w_q_up, w_kv_up, sink, w_branch_a, w_branch_b,
           w_out, ln1_w, ln1_b, w_group, b_group, w_expert_router, b_expert, w_gate, w_up, w_down,
           w_ple_up, w_ple_gate, ln2_w, ln2_b):
    batch, seq, d = x.shape
    t = batch * seq
    h = x.reshape(t, d)
    tabs = _rope_tables(positions.reshape(t, 1))
    p2 = p.reshape(p.shape[0], t, PLE_DIM)
    for layer in range(w_in.shape[0]):
        h = _layer(layer, h, p2, tabs, batch, seq, w_in, q_norm, kv_norm, w_q_up, w_kv_up, sink,
                   w_branch_a, w_branch_b, w_out, ln1_w, ln1_b, w_group, b_group, w_expert_router,
                   b_expert, w_gate, w_up, w_down, w_ple_up, w_ple_gate, ln2_w, ln2_b)
    return h.reshape(batch, seq, d)
```

```python
import functools
import math

import numpy as np
import jax
import jax.numpy as jnp
from jax import lax
from jax.experimental import pallas as pl
from jax.experimental.pallas import tpu as pltpu

F32 = jnp.float32
BF16 = jnp.bfloat16

D_MODEL = 4096
MLA_HEADS = 16
MLA_NOPE = 128
MLA_ROPE = 64
MLA_V = 128
Q_RANK = 768
KV_RANK = 512
SWA_HEADS = 16
SWA_KV_HEADS = 4
SWA_GROUP = SWA_HEADS // SWA_KV_HEADS
SWA_DIM = 128
WINDOW = 128
ROPE_THETA = 10000.0
N_GROUPS = 8
EXPERTS_PER_GROUP = 8
N_EXPERTS = N_GROUPS * EXPERTS_PER_GROUP
EXPERT_FF = 1024
PLE_DIM = 256
LN_EPS = 1e-5
RMS_EPS = 1e-6
DEPTH = 1
ALPHA = (2.0 * DEPTH) ** 0.25
LOG2E = math.log2(math.e)

OFF_KR = Q_RANK + KV_RANK
OFF_QB = OFF_KR + MLA_ROPE
SWA_COLS = (SWA_HEADS + 2 * SWA_KV_HEADS) * SWA_DIM
IN_WIDTH = OFF_QB + SWA_COLS + 2 * D_MODEL

LANES = 128
MIB = 1024 * 1024

ROW_CHUNKS = D_MODEL // LANES

MOE_TILE = 336
MOE_KC = 512
MOE_NA = D_MODEL // MOE_KC
MOE_NC = 1024
MOE_NB = D_MODEL // MOE_NC
MOE_STEPS = MOE_NA + MOE_NB
MOE_ROWS_PER_STEP = MOE_TILE // MOE_STEPS
MOE_RING = 4
assert MOE_STEPS % MOE_RING == 0
ROUTE_LANES = LANES
EXPERT_LANE0 = N_GROUPS
ROUTE_FIELDS = 8


def _params(semantics, vmem_mib):
    return pltpu.CompilerParams(dimension_semantics=semantics,
                                vmem_limit_bytes=vmem_mib * MIB)


def _sigmoid(v):
    return 1.0 / (1.0 + jnp.exp(-v))


def _dot(a, b):
    return jnp.dot(a, b, preferred_element_type=F32)


SUBLANES = 8


def _transpose8(vs):
    sub = lax.broadcasted_iota(jnp.int32, (SUBLANES, LANES), 0)
    vs = list(vs)
    for d in (4, 2, 1):
        low = (sub & d) == 0
        for i in range(SUBLANES):
            if i & d:
                continue
            a, b = vs[i], vs[i + d]
            vs[i] = jnp.where(low, a, pltpu.roll(b, d, 0))
            vs[i + d] = jnp.where(low, pltpu.roll(a, SUBLANES - d, 0), b)
    return vs


def _rot_half_64(v):
    lane = lax.broadcasted_iota(jnp.int32, v.shape, 1)
    half = MLA_ROPE // 2
    fwd = pltpu.roll(v, half, 1)
    bwd = pltpu.roll(v, LANES - half, 1)
    return jnp.where(lane < half, -bwd, jnp.where(lane < MLA_ROPE, fwd, 0.0))


def _tables_kernel(pos_ref, inv_a_ref, inv_b_ref, sign_b_ref, cos_a, sin_a, cos_b, sin_b):
    pos = pos_ref[...].astype(F32)
    ang_a = pos * inv_a_ref[...]
    ang_b = pos * inv_b_ref[...]
    cos_a[...] = jnp.cos(ang_a)
    sin_a[...] = jnp.sin(ang_a)
    cos_b[...] = jnp.cos(ang_b)
    sin_b[...] = jnp.sin(ang_b) * sign_b_ref[...]


def _rope_tables(positions):
    t = positions.shape[0]
    tm = 1024
    inv64 = ROPE_THETA ** (-np.arange(0, MLA_ROPE, 2, dtype=np.float32) / MLA_ROPE)
    inv128 = ROPE_THETA ** (-np.arange(0, SWA_DIM, 2, dtype=np.float32) / SWA_DIM)
    inv_a = jnp.asarray(np.tile(inv64, 4)[None, :], F32)
    inv_b = jnp.asarray(np.tile(inv128, 2)[None, :], F32)
    sign_b = jnp.asarray(np.concatenate([-np.ones(64), np.ones(64)])[None, :], F32)
    row = pl.BlockSpec((tm, LANES), lambda i: (i, 0))
    const = pl.BlockSpec((1, LANES), lambda i: (0, 0))
    out = jax.ShapeDtypeStruct((t, LANES), F32)
    return pl.pallas_call(
        _tables_kernel,
        grid=(t // tm,),
        in_specs=[pl.BlockSpec((tm, 1), lambda i: (i, 0)), const, const, const],
        out_specs=[row, row, row, row],
        out_shape=[out, out, out, out],
        compiler_params=_params(("parallel",), 32),
        name="rope_tables",
    )(positions, inv_a, inv_b, sign_b)


LAT_COLS = OFF_KR + LANES
SHIFT_COLS = IN_WIDTH - OFF_QB
CAST_TILE = 1024


def _cast_transpose_kernel(w_ref, o_ref):
    o_ref[...] = jnp.transpose(w_ref[...]).astype(BF16)


def _prep_input_weight(w_in, layer):
    d = w_in.shape[1]
    w_t = jnp.swapaxes(w_in, 1, 2)
    tk = 512
    w_lat = pl.pallas_call(
        _cast_transpose_kernel,
        grid=(d // tk,),
        in_specs=[pl.BlockSpec((None, LAT_COLS, tk), lambda i: (layer, 0, i))],
        out_specs=pl.BlockSpec((tk, LAT_COLS), lambda i: (i, 0)),
        out_shape=jax.ShapeDtypeStruct((d, LAT_COLS), BF16),
        compiler_params=_params(("parallel",), 32),
        name="cast_latent_w",
    )(w_t)
    w_sh = pl.pallas_call(
        _cast_transpose_kernel,
        grid=(d // CAST_TILE, SHIFT_COLS // CAST_TILE),
        in_specs=[pl.BlockSpec((pl.Element(CAST_TILE), pl.Element(CAST_TILE)),
                               lambda i, j: (pl.multiple_of(OFF_QB + CAST_TILE * j, LANES // 2),
                                             pl.multiple_of(CAST_TILE * i, CAST_TILE)))],
        out_specs=pl.BlockSpec((CAST_TILE, CAST_TILE), lambda i, j: (i, j)),
        out_shape=jax.ShapeDtypeStruct((d, SHIFT_COLS), BF16),
        compiler_params=_params(("parallel", "parallel"), 32),
        name="cast_shift_w",
    )(w_t[layer])
    return w_lat, w_sh


def _latent_kernel(x_ref, w_ref, qn_ref, kvn_ref, cos_ref, sin_ref,
                   xb_ref, cq_ref, ckv_ref, kr_ref):
    xb = x_ref[...].astype(BF16)
    xb_ref[...] = xb
    acc = _dot(xb, w_ref[...])
    cq = acc[:, :Q_RANK]
    ckv = acc[:, Q_RANK:OFF_KR]
    cq = cq * lax.rsqrt(jnp.mean(cq * cq, axis=-1, keepdims=True) + RMS_EPS) * qn_ref[...]
    ckv = ckv * lax.rsqrt(jnp.mean(ckv * ckv, axis=-1, keepdims=True) + RMS_EPS) * kvn_ref[...]
    cq_ref[...] = cq.astype(BF16)
    ckv_ref[...] = ckv.astype(BF16)
    k = acc[:, OFF_KR:]
    lane = lax.broadcasted_iota(jnp.int32, k.shape, 1)
    kr = jnp.where(lane < MLA_ROPE, k * cos_ref[...], 0.0) + _rot_half_64(k) * sin_ref[...]
    kr_ref[...] = kr.astype(BF16)


def _latent_proj(x, w_lat, q_norm, kv_norm, cos_a, sin_a):
    t = x.shape[0]
    tm = 512
    row = lambda n: pl.BlockSpec((tm, n), lambda i: (i, 0))
    const = lambda r, n: pl.BlockSpec((r, n), lambda i: (0, 0))
    resident_w = pl.BlockSpec((D_MODEL, LAT_COLS), lambda i: (0, 0), pipeline_mode=pl.Buffered(1))
    return pl.pallas_call(
        _latent_kernel,
        grid=(t // tm,),
        in_specs=[row(D_MODEL), resident_w, const(1, Q_RANK), const(1, KV_RANK),
                  row(LANES), row(LANES)],
        out_specs=[row(D_MODEL), row(Q_RANK), row(KV_RANK), row(LANES)],
        out_shape=[jax.ShapeDtypeStruct((t, D_MODEL), BF16),
                   jax.ShapeDtypeStruct((t, Q_RANK), BF16),
                   jax.ShapeDtypeStruct((t, KV_RANK), BF16),
                   jax.ShapeDtypeStruct((t, LANES), BF16)],
        compiler_params=_params(("parallel",), 52),
        name="latent_proj",
    )(x, w_lat, q_norm, kv_norm, cos_a, sin_a)


UP_HEADS_PER_STEP = 4


def _qup_kernel(cq_ref, wa_ref, cos_ref, sin_ref, q_ref, *, scale):
    a = _dot(cq_ref[...], wa_ref[...])
    cos = cos_ref[...] * scale
    sin = sin_ref[...] * scale
    for h in range(UP_HEADS_PER_STEP):
        c0 = h * 2 * LANES
        r = a[:, c0 + LANES:c0 + 2 * LANES]
        q_ref[:, c0:c0 + LANES] = (a[:, c0:c0 + LANES] * scale).astype(BF16)
        q_ref[:, c0 + LANES:c0 + 2 * LANES] = (r * cos + _rot_half_64(r) * sin).astype(BF16)


def _q_up(cq, w_qa, cos_a, sin_a):
    t = cq.shape[0]
    tm = 1024
    tn = UP_HEADS_PER_STEP * 2 * LANES
    scale = float((MLA_NOPE + MLA_ROPE) ** -0.5 * LOG2E)
    return pl.pallas_call(
        functools.partial(_qup_kernel, scale=scale),
        grid=(t // tm, MLA_HEADS // UP_HEADS_PER_STEP),
        in_specs=[pl.BlockSpec((tm, Q_RANK), lambda i, j: (i, 0)),
                  pl.BlockSpec((Q_RANK, tn), lambda i, j: (0, j)),
                  pl.BlockSpec((tm, LANES), lambda i, j: (i, 0)),
                  pl.BlockSpec((tm, LANES), lambda i, j: (i, 0))],
        out_specs=pl.BlockSpec((tm, tn), lambda i, j: (i, j)),
        out_shape=jax.ShapeDtypeStruct((t, MLA_HEADS * 2 * LANES), BF16),
        compiler_params=_params(("parallel", "parallel"), 32),
        name="q_up",
    )(cq, w_qa, cos_a, sin_a)


def _kvup_kernel(ckv_ref, w_ref, kr_ref, k_ref, v_ref):
    acc = _dot(ckv_ref[...], w_ref[...].astype(BF16))
    kr = kr_ref[...]
    for h in range(UP_HEADS_PER_STEP):
        c0 = h * 2 * LANES
        k_ref[:, c0:c0 + LANES] = acc[:, c0:c0 + LANES].astype(BF16)
        k_ref[:, c0 + LANES:c0 + 2 * LANES] = kr
        v_ref[:, h * MLA_V:(h + 1) * MLA_V] = acc[:, c0 + LANES:c0 + 2 * LANES].astype(BF16)


def _kv_up(ckv, w_kv_up, layer, kr):
    t = ckv.shape[0]
    tm = 1024
    tn = UP_HEADS_PER_STEP * 2 * LANES
    return pl.pallas_call(
        _kvup_kernel,
        grid=(t // tm, MLA_HEADS // UP_HEADS_PER_STEP),
        in_specs=[pl.BlockSpec((tm, KV_RANK), lambda i, j: (i, 0)),
                  pl.BlockSpec((None, KV_RANK, tn), lambda i, j: (layer, 0, j)),
                  pl.BlockSpec((tm, LANES), lambda i, j: (i, 0))],
        out_specs=[pl.BlockSpec((tm, tn), lambda i, j: (i, j)),
                   pl.BlockSpec((tm, UP_HEADS_PER_STEP * MLA_V), lambda i, j: (i, j))],
        out_shape=[jax.ShapeDtypeStruct((t, MLA_HEADS * 2 * LANES), BF16),
                   jax.ShapeDtypeStruct((t, MLA_HEADS * MLA_V), BF16)],
        compiler_params=_params(("parallel", "parallel"), 32),
        name="kv_up",
    )(ckv, w_kv_up, kr)


SWA_TN = 512
SWA_Q_TILES = SWA_HEADS * SWA_DIM // SWA_TN
SWA_ROPE_TILES = (SWA_HEADS + SWA_KV_HEADS) * SWA_DIM // SWA_TN


def _swaproj_kernel(xb_ref, w_ref, cos_ref, sin_ref, o_ref, *, scale):
    j = pl.program_id(1)
    acc = _dot(xb_ref[...], w_ref[...])

    @pl.when(j < SWA_ROPE_TILES)
    def _():
        sc = jnp.where(j < SWA_Q_TILES, scale, 1.0).astype(F32)
        cos = cos_ref[...] * sc
        sin = sin_ref[...] * sc
        for c in range(SWA_TN // LANES):
            a = acc[:, c * LANES:(c + 1) * LANES]
            r = a * cos + pltpu.roll(a, SWA_DIM // 2, 1) * sin
            o_ref[:, c * LANES:(c + 1) * LANES] = r.astype(BF16)

    @pl.when(j >= SWA_ROPE_TILES)
    def _():
        o_ref[...] = acc.astype(BF16)


def _swa_proj(xb, w_sh, cos_b, sin_b):
    t = xb.shape[0]
    tm = 1024
    return pl.pallas_call(
        functools.partial(_swaproj_kernel, scale=float(SWA_DIM ** -0.5 * LOG2E)),
        grid=(t // tm, SWA_COLS // SWA_TN),
        in_specs=[pl.BlockSpec((tm, D_MODEL), lambda i, j: (i, 0)),
                  pl.BlockSpec((D_MODEL, SWA_TN), lambda i, j: (0, j)),
                  pl.BlockSpec((tm, LANES), lambda i, j: (i, 0)),
                  pl.BlockSpec((tm, LANES), lambda i, j: (i, 0))],
        out_specs=pl.BlockSpec((tm, SWA_TN), lambda i, j: (i, j)),
        out_shape=jax.ShapeDtypeStruct((t, SWA_COLS), BF16),
        compiler_params=_params(("parallel", "parallel"), 48),
        name="swa_proj",
    )(xb, w_sh, cos_b, sin_b)


MLA_HEADS_PER_STEP = 2


def _mla_kernel(q_ref, k_ref, v_ref, o_ref):
    for h in range(MLA_HEADS_PER_STEP):
        q = q_ref[:, h * 2 * LANES:(h + 1) * 2 * LANES]
        k = k_ref[:, h * 2 * LANES:(h + 1) * 2 * LANES]
        s = lax.dot_general(q, k, (((1,), (1,)), ((), ())), preferred_element_type=F32)
        m = jnp.max(s, axis=-1, keepdims=True)
        p = jnp.exp2(s - m)
        l = jnp.sum(p, axis=-1, keepdims=True)
        o = _dot(p.astype(BF16), v_ref[:, h * MLA_V:(h + 1) * MLA_V])
        o_ref[:, h * MLA_V:(h + 1) * MLA_V] = (o / l).astype(BF16)


def _mla_attention(qf, kf, v, batch, seq):
    t = qf.shape[0]
    tq = 1024
    nq = seq // tq
    hp = MLA_HEADS_PER_STEP
    return pl.pallas_call(
        _mla_kernel,
        grid=(batch, MLA_HEADS // hp, nq),
        in_specs=[pl.BlockSpec((tq, hp * 2 * LANES), lambda b, h, i: (b * nq + i, h)),
                  pl.BlockSpec((seq, hp * 2 * LANES), lambda b, h, i: (b, h)),
                  pl.BlockSpec((seq, hp * MLA_V), lambda b, h, i: (b, h))],
        out_specs=pl.BlockSpec((tq, hp * MLA_V), lambda b, h, i: (b * nq + i, h)),
        out_shape=jax.ShapeDtypeStruct((t, MLA_HEADS * MLA_V), BF16),
        compiler_params=_params(("parallel", "parallel", "parallel"), 52),
        name="mla_attention",
    )(qf, kf, v)


SWA_BLOCK = 128
SWA_TQ = 512
SWA_WIN = SWA_TQ + 2 * WINDOW


def _swa_kernel(q_ref, k_ref, v_ref, sink_ref, bias_ref, o_ref, *, seq):
    step = pl.program_id(2)
    q0 = step * SWA_TQ
    ks = pl.multiple_of(jnp.clip(q0 - WINDOW, 0, seq - SWA_WIN), WINDOW)
    bias = bias_ref[lax.div(q0 - ks, WINDOW)]
    kw = k_ref[pl.ds(ks, SWA_WIN), :]
    vw = v_ref[pl.ds(ks, SWA_WIN), :]
    for g in range(SWA_GROUP):
        q = q_ref[:, g * SWA_DIM:(g + 1) * SWA_DIM]
        s = lax.dot_general(q, kw, (((1,), (1,)), ((), ())), preferred_element_type=F32) + bias
        sink = sink_ref[g * SWA_BLOCK:g * SWA_BLOCK + 1, :1] * LOG2E
        m = jnp.maximum(jnp.max(s, axis=-1, keepdims=True), sink)
        e = jnp.exp2(s - m)
        denom = jnp.sum(e, axis=-1, keepdims=True) + jnp.exp2(sink - m)
        o = _dot(e.astype(BF16), vw) / denom
        o_ref[:, g * SWA_DIM:(g + 1) * SWA_DIM] = o.astype(BF16)


def _swa_band_bias():
    r = np.arange(SWA_TQ)[:, None]
    c = np.arange(SWA_WIN)[None, :]
    variants = [np.where(np.abs(v * WINDOW + r - c) <= WINDOW, 0.0, -np.inf) for v in range(3)]
    return jnp.asarray(np.stack(variants), F32)


def _swa_attention(qkv, sink_rows, batch, seq):
    t = qkv.shape[0]
    nq = seq // SWA_TQ
    gw = SWA_GROUP * SWA_DIM
    k_col0 = SWA_HEADS
    v_col0 = SWA_HEADS + SWA_KV_HEADS
    return pl.pallas_call(
        functools.partial(_swa_kernel, seq=seq),
        grid=(batch, SWA_KV_HEADS, nq),
        in_specs=[pl.BlockSpec((SWA_TQ, gw), lambda b, h, n: (b * nq + n, h)),
                  pl.BlockSpec((seq, SWA_DIM), lambda b, h, n: (b, k_col0 + h)),
                  pl.BlockSpec((seq, SWA_DIM), lambda b, h, n: (b, v_col0 + h)),
                  pl.BlockSpec((None, SWA_GROUP * SWA_BLOCK, LANES), lambda b, h, n: (h, 0, 0)),
                  pl.BlockSpec((3, SWA_TQ, SWA_WIN), lambda b, h, n: (0, 0, 0))],
        out_specs=pl.BlockSpec((SWA_TQ, gw), lambda b, h, n: (b * nq + n, h)),
        out_shape=jax.ShapeDtypeStruct((t, SWA_HEADS * SWA_DIM), BF16),
        compiler_params=_params(("parallel", "parallel", "parallel"), 48),
        name="swa_attention",
    )(qkv, qkv, qkv, sink_rows, _swa_band_bias())


MERGE_TN = 256
GATE_A_BLOCK0 = SWA_COLS // MERGE_TN
GATE_B_BLOCK0 = (SWA_COLS + D_MODEL) // MERGE_TN


def _merge_kernel(xb_ref, oa_ref, ob_ref, wga_ref, wgb_ref, wa_ref, wb_ref, o_ref):
    xb = xb_ref[...]
    ga = _sigmoid(_dot(xb, wga_ref[...]))
    ya = _dot(oa_ref[...], wa_ref[...].astype(BF16))
    acc = ga * ya
    gb = _sigmoid(_dot(xb, wgb_ref[...]))
    yb = _dot(ob_ref[...], wb_ref[...].astype(BF16))
    o_ref[...] = (acc + gb * yb).astype(BF16)


def _branch_merge(xb, o_a, o_b, w_sh, w_a, w_b, layer):
    t = xb.shape[0]
    tm, tn = 512, MERGE_TN
    ka = o_a.shape[1]
    row = lambda k: pl.BlockSpec((tm, k), lambda i, j: (i, 0))
    return pl.pallas_call(
        _merge_kernel,
        grid=(t // tm, D_MODEL // tn),
        in_specs=[row(D_MODEL), row(ka), row(ka),
                  pl.BlockSpec((D_MODEL, tn), lambda i, j: (0, GATE_A_BLOCK0 + j)),
                  pl.BlockSpec((D_MODEL, tn), lambda i, j: (0, GATE_B_BLOCK0 + j)),
                  pl.BlockSpec((None, ka, tn), lambda i, j: (layer, 0, j)),
                  pl.BlockSpec((None, ka, tn), lambda i, j: (layer, 0, j))],
        out_specs=pl.BlockSpec((tm, tn), lambda i, j: (i, j)),
        out_shape=jax.ShapeDtypeStruct((t, D_MODEL), BF16),
        compiler_params=_params(("parallel", "parallel"), 48),
        name="branch_merge",
    )(xb, o_a, o_b, w_sh, w_sh, w_a, w_b)


def _outproj_kernel(m_ref, w_ref, x_ref, o_ref):
    o_ref[...] = ALPHA * x_ref[...] + _dot(m_ref[...], w_ref[...].astype(BF16))


def _out_proj(merged, w_out, layer, x):
    t = x.shape[0]
    tm, tn = 1024, 512
    return pl.pallas_call(
        _outproj_kernel,
        grid=(t // tm, D_MODEL // tn),
        in_specs=[pl.BlockSpec((tm, D_MODEL), lambda i, j: (i, 0)),
                  pl.BlockSpec((None, D_MODEL, tn), lambda i, j: (layer, 0, j)),
                  pl.BlockSpec((tm, tn), lambda i, j: (i, j))],
        out_specs=pl.BlockSpec((tm, tn), lambda i, j: (i, j)),
        out_shape=jax.ShapeDtypeStruct((t, D_MODEL), F32),
        compiler_params=_params(("parallel", "parallel"), 52),
        name="out_proj",
    )(merged, w_out, x)


def _layer_norm(v, w, b):
    mu = jnp.mean(v, axis=-1, keepdims=True)
    c = v - mu
    var = jnp.mean(c * c, axis=-1, keepdims=True)
    return c * lax.rsqrt(var + LN_EPS) * w + b


def _split_bf16(v):
    hi = v.astype(BF16)
    lo = (v - hi.astype(F32)).astype(BF16)
    return hi, lo


def _ln_route_kernel(pre_ref, lnw_ref, lnb_ref, wr_ref, br_ref,
                     x1_ref, x1b_ref, x1lin_ref, route_ref, route_t_ref, cnt_ref, carry):
    i = pl.program_id(0)

    @pl.when(i == 0)
    def _():
        carry[...] = jnp.zeros_like(carry)

    x1 = _layer_norm(pre_ref[...], lnw_ref[...], lnb_ref[...])
    tm = x1.shape[0]
    x1_ref[...] = x1
    x1b_ref[...] = x1.astype(BF16)

    def to_row_linear(a, carry):
        r0 = pl.multiple_of(a * SUBLANES, SUBLANES)
        pack = 2 * SUBLANES
        for qq in range(ROW_CHUNKS // pack):
            parts = []
            for q in (2 * qq, 2 * qq + 1):
                tiles = [x1_ref[pl.ds(r0, SUBLANES),
                                (q * SUBLANES + k) * LANES:(q * SUBLANES + k + 1) * LANES]
                         for k in range(SUBLANES)]
                parts.append(_transpose8(tiles))
            for b in range(SUBLANES):
                dst = pl.multiple_of((r0 + b) * ROW_CHUNKS + qq * pack, pack)
                x1lin_ref[pl.ds(dst, pack), :] = jnp.concatenate(
                    [parts[0][b], parts[1][b]], axis=0).astype(BF16)
        return carry

    lax.fori_loop(0, tm // SUBLANES, to_row_linear, 0)

    xh, xl = _split_bf16(x1)
    wh, wl = _split_bf16(wr_ref[...])
    logits = _dot(xh, wh) + (_dot(xl, wh) + _dot(xh, wl)) + br_ref[...]

    lane = lax.broadcasted_iota(jnp.int32, logits.shape, 1)
    neg = -jnp.inf
    gl = jnp.where(lane < N_GROUPS, logits, neg)
    gmax = jnp.max(gl, axis=-1, keepdims=True)
    gsum = jnp.sum(jnp.exp(gl - gmax), axis=-1, keepdims=True)
    g_w = 1.0 / gsum
    g_idx = jnp.min(jnp.where(gl == gmax, lane, ROUTE_LANES), axis=-1, keepdims=True)
    d = lane - (EXPERT_LANE0 + g_idx * EXPERTS_PER_GROUP)
    el = jnp.where((d >= 0) & (d < EXPERTS_PER_GROUP), logits, neg)
    m1 = jnp.max(el, axis=-1, keepdims=True)
    i1 = jnp.min(jnp.where(el == m1, lane, ROUTE_LANES), axis=-1, keepdims=True)
    el2 = jnp.where(lane == i1, neg, el)
    m2 = jnp.max(el2, axis=-1, keepdims=True)
    i2 = jnp.min(jnp.where(el2 == m2, lane, ROUTE_LANES), axis=-1, keepdims=True)
    e21 = jnp.exp(m2 - m1)
    w1 = g_w / (1.0 + e21)
    w2 = g_w * e21 / (1.0 + e21)

    oh1 = (lane == i1).astype(F32)
    oh2 = (lane == i2).astype(F32)
    oh = oh1 + oh2
    r_i = lax.broadcasted_iota(jnp.int32, (tm, tm), 0)
    c_i = lax.broadcasted_iota(jnp.int32, (tm, tm), 1)
    tri = jnp.where(c_i < r_i, 1.0, 0.0).astype(BF16)
    before = _dot(tri, oh.astype(BF16)) + carry[...]
    rank1 = jnp.sum(oh1 * before, axis=-1, keepdims=True)
    rank2 = jnp.sum(oh2 * before, axis=-1, keepdims=True)
    carry[...] = carry[...] + jnp.sum(oh, axis=0, keepdims=True)
    cnt_ref[...] = carry[...]

    slab = jnp.where(lane == 0, (i1 - EXPERT_LANE0).astype(F32), 0.0)
    slab = jnp.where(lane == 1, (i2 - EXPERT_LANE0).astype(F32), slab)
    slab = jnp.where(lane == 2, w1, slab)
    slab = jnp.where(lane == 3, w2, slab)
    slab = jnp.where(lane == 4, rank1, slab)
    slab = jnp.where(lane == 5, rank2, slab)
    slab = jnp.where(lane == 6, (i1 - EXPERT_LANE0).astype(F32) * float(1 << SLOT_SHIFT) + rank1, slab)
    slab = jnp.where(lane == 7, (i2 - EXPERT_LANE0).astype(F32) * float(1 << SLOT_SHIFT) + rank2, slab)
    route_ref[...] = slab
    route_t_ref[...] = jnp.transpose(slab)[:ROUTE_FIELDS, :]


def _ln_route(pre1, ln_w, ln_b, w_r, b_r):
    t = pre1.shape[0]
    tm = 256
    row = lambda n: pl.BlockSpec((tm, n), lambda i: (i, 0))
    const = lambda r, n: pl.BlockSpec((r, n), lambda i: (0, 0))
    return pl.pallas_call(
        _ln_route_kernel,
        grid=(t // tm,),
        in_specs=[row(D_MODEL), const(1, D_MODEL), const(1, D_MODEL),
                  const(D_MODEL, ROUTE_LANES), const(1, ROUTE_LANES)],
        out_specs=[row(D_MODEL), row(D_MODEL),
                   pl.BlockSpec((tm * ROW_CHUNKS, LANES), lambda i: (i, 0)),
                   row(ROUTE_LANES),
                   pl.BlockSpec((ROUTE_FIELDS, tm), lambda i: (0, i)),
                   const(1, ROUTE_LANES)],
        out_shape=[jax.ShapeDtypeStruct((t, D_MODEL), F32),
                   jax.ShapeDtypeStruct((t, D_MODEL), BF16),
                   jax.ShapeDtypeStruct((t * ROW_CHUNKS, LANES), BF16),
                   jax.ShapeDtypeStruct((t, ROUTE_LANES), F32),
                   jax.ShapeDtypeStruct((ROUTE_FIELDS, t), F32),
                   jax.ShapeDtypeStruct((1, ROUTE_LANES), F32)],
        scratch_shapes=[pltpu.VMEM((1, ROUTE_LANES), F32)],
        compiler_params=_params(("arbitrary",), 52),
        name="ln1_route",
    )(pre1, ln_w, ln_b, w_r, b_r)


SLOT_SHIFT = 16
SLOT_MASK = (1 << SLOT_SHIFT) - 1


def _plan_kernel(c1_ref, c2_ref, cnt_ref,
                 pos1_ref, pos2_ref, src_ref, te_ref, nv_ref, tstart, *, n_tokens, n_tiles):
    def per_expert(e, first_tile):
        tstart[e] = first_tile
        c = cnt_ref[e]
        nt = lax.div(c + (MOE_TILE - 1), MOE_TILE)

        def set_te(k, carry):
            te_ref[first_tile + k] = e
            return carry

        lax.fori_loop(0, nt, set_te, 0)

        def pad_src(k, carry):
            src_ref[first_tile * MOE_TILE + k] = 0
            return carry

        lax.fori_loop(c, nt * MOE_TILE, pad_src, 0)
        return first_tile + nt

    nv = lax.fori_loop(0, N_EXPERTS, per_expert, 0)
    nv_ref[0] = nv
    last = te_ref[nv - 1]

    def tail_te(k, carry):
        te_ref[k] = last
        return carry

    lax.fori_loop(nv, n_tiles, tail_te, 0)

    def per_token(t, carry):
        c1 = c1_ref[t]
        c2 = c2_ref[t]
        p1 = tstart[lax.shift_right_logical(c1, SLOT_SHIFT)] * MOE_TILE + (c1 & SLOT_MASK)
        p2 = tstart[lax.shift_right_logical(c2, SLOT_SHIFT)] * MOE_TILE + (c2 & SLOT_MASK)
        pos1_ref[t] = p1
        pos2_ref[t] = p2
        src_ref[p1] = t
        src_ref[p2] = t
        return carry

    lax.fori_loop(0, n_tokens, per_token, 0, unroll=8)


def _dispatch_plan(c1, c2, cnt, n_tiles):
    t = c1.shape[0]
    assert 2 * t <= SLOT_MASK + 1
    smem = pl.BlockSpec(memory_space=pltpu.SMEM)
    grid_spec = pltpu.PrefetchScalarGridSpec(
        num_scalar_prefetch=3,
        grid=(1,),
        in_specs=[],
        out_specs=[smem, smem, smem, smem, smem],
        scratch_shapes=[pltpu.SMEM((N_EXPERTS,), jnp.int32)],
    )
    i32 = lambda n: jax.ShapeDtypeStruct((n,), jnp.int32)
    return pl.pallas_call(
        functools.partial(_plan_kernel, n_tokens=t, n_tiles=n_tiles),
        grid_spec=grid_spec,
        out_shape=[i32(t), i32(t), i32(n_tiles * MOE_TILE), i32(n_tiles), i32(1)],
        compiler_params=_params(("arbitrary",), 32),
        name="dispatch_plan",
    )(c1, c2, cnt)


def _moe_kernel(te_ref, nv_ref, src_ref, x_hbm, wg_hbm, wu_hbm, wd_hbm, ys_ref,
                stage, xb, hg, hu, act, ytmp, ring, sem, wsem, *, layer):
    i = pl.program_id(0)
    nv = nv_ref[0]
    slot = lax.rem(i, 2)
    tile_rows = MOE_TILE * ROW_CHUNKS
    half_ff = EXPERT_FF // 2
    out_per = MOE_NC // LANES

    def issue_chunk(tile, step):
        sl = step % MOE_RING
        e = te_ref[tile]
        if step < MOE_NA:
            rows = pl.ds(step * MOE_KC, MOE_KC)
            pltpu.make_async_copy(wg_hbm.at[layer, e, rows, :],
                                  ring.at[sl, pl.ds(0, MOE_KC), :], wsem.at[sl]).start(priority=0)
            pltpu.make_async_copy(wu_hbm.at[layer, e, rows, :],
                                  ring.at[sl, pl.ds(MOE_KC, MOE_KC), :], wsem.at[sl]).start(priority=1)
        else:
            cols = pl.ds((step - MOE_NA) * MOE_NC, MOE_NC)
            for k in range(2):
                rows = pl.ds(k * half_ff, half_ff)
                pltpu.make_async_copy(wd_hbm.at[layer, e, rows, cols],
                                      ring.at[sl, rows, :], wsem.at[sl]).start(priority=k)

    def wait_chunk(step):
        sl = step % MOE_RING
        pltpu.make_async_copy(wd_hbm.at[layer, 0, :, pl.ds(0, MOE_NC)], ring.at[sl],
                              wsem.at[sl]).wait()

    def row_copy(tile, sl, r):
        tok = src_ref[tile * MOE_TILE + r]
        return pltpu.make_async_copy(x_hbm.at[pl.ds(tok * ROW_CHUNKS, ROW_CHUNKS)],
                                     stage.at[sl, pl.ds(r * ROW_CHUNKS, ROW_CHUNKS)], sem.at[sl])

    def issue_rows(tile, sl, r0, n):
        def body(k, carry):
            row_copy(tile, sl, r0 + 2 * k).start(priority=0)
            row_copy(tile, sl, r0 + 2 * k + 1).start(priority=1)
            return carry

        lax.fori_loop(0, n // 2, body, 0)

    def to_tiled(a, carry):
        per = MOE_KC // LANES
        pair = 2 * SUBLANES
        t0 = pl.multiple_of(a * pair, pair)
        for qq in range(ROW_CHUNKS // pair):
            halves = []
            for h in range(2):
                blks = [stage[slot, pl.ds(pl.multiple_of(
                    (t0 + h * SUBLANES + b) * ROW_CHUNKS + qq * pair, pair), pair), :].astype(F32)
                    for b in range(SUBLANES)]
                halves.append(_transpose8([v[:SUBLANES] for v in blks])
                              + _transpose8([v[SUBLANES:] for v in blks]))
            for k in range(pair):
                s = qq * pair + k
                blk = jnp.concatenate([halves[0][k], halves[1][k]], axis=0).astype(BF16)
                xb[s // per, pl.ds(t0, pair), (s % per) * LANES:(s % per + 1) * LANES] = blk
        return carry

    def up_gate_step(step):
        sl = step % MOE_RING
        xk = xb[step]
        pg = _dot(xk, ring[sl, 0:MOE_KC, :].astype(BF16))
        pu = _dot(xk, ring[sl, MOE_KC:2 * MOE_KC, :].astype(BF16))
        if step == 0:
            hg[...] = pg
            hu[...] = pu
        elif step < MOE_NA - 1:
            hg[...] += pg
            hu[...] += pu
        else:
            g = hg[...] + pg
            act[...] = (g * _sigmoid(g) * (hu[...] + pu)).astype(BF16)

    def down_step(step):
        sl = step % MOE_RING
        y = _dot(act[...], ring[sl].astype(BF16))
        ys2 = ys_ref.reshape(tile_rows, LANES)
        down = step - MOE_NA
        pack = 2 * out_per
        for a in range(MOE_TILE // SUBLANES):
            tiles = [y[a * SUBLANES:(a + 1) * SUBLANES, k * LANES:(k + 1) * LANES]
                     for k in range(out_per)]
            rows = _transpose8(tiles)
            for b in range(SUBLANES):
                r = a * SUBLANES + b
                if down % 2 == 0:
                    ytmp[r * out_per:(r + 1) * out_per, :] = rows[b]
                else:
                    r0 = r * ROW_CHUNKS + (down // 2) * pack
                    ys2[r0:r0 + pack, :] = jnp.concatenate(
                        [ytmp[r * out_per:(r + 1) * out_per, :], rows[b]], axis=0).astype(BF16)

    @pl.when(i < nv)
    def _():
        @pl.when(i == 0)
        def _():
            for c in range(MOE_RING - 1):
                issue_chunk(0, c)
            issue_rows(0, 0, 0, MOE_TILE)

        pltpu.make_async_copy(x_hbm.at[pl.ds(0, tile_rows)], stage.at[slot], sem.at[slot]).wait()
        lax.fori_loop(0, MOE_TILE // (2 * SUBLANES), to_tiled, 0)
        has_next = i + 1 < nv

        for step in range(MOE_STEPS):
            ahead = step + MOE_RING - 1
            if ahead < MOE_STEPS:
                issue_chunk(i, ahead)
            else:
                pl.when(has_next)(functools.partial(issue_chunk, i + 1, ahead - MOE_STEPS))
            pl.when(has_next)(functools.partial(
                issue_rows, i + 1, 1 - slot, step * MOE_ROWS_PER_STEP, MOE_ROWS_PER_STEP))
            wait_chunk(step)
            if step < MOE_NA:
                up_gate_step(step)
            else:
                down_step(step)


def _moe_experts(te, nv, src, x1lin, w_gate, w_up, w_down, layer, n_tiles):
    per = MOE_NC // LANES

    def out_map(i, te, nv, src):
        return (jnp.minimum(i, nv[0] - 1), 0, 0)

    assert 2 * MOE_KC == EXPERT_FF and MOE_NC == EXPERT_FF
    any_spec = pl.BlockSpec(memory_space=pl.ANY)
    grid_spec = pltpu.PrefetchScalarGridSpec(
        num_scalar_prefetch=3,
        grid=(n_tiles,),
        in_specs=[any_spec, any_spec, any_spec, any_spec],
        out_specs=pl.BlockSpec((MOE_TILE, ROW_CHUNKS, LANES), out_map),
        scratch_shapes=[pltpu.VMEM((2, MOE_TILE * ROW_CHUNKS, LANES), BF16),
                        pltpu.VMEM((MOE_NA, MOE_TILE, MOE_KC), BF16),
                        pltpu.VMEM((MOE_TILE, EXPERT_FF), F32),
                        pltpu.VMEM((MOE_TILE, EXPERT_FF), F32),
                        pltpu.VMEM((MOE_TILE, EXPERT_FF), BF16),
                        pltpu.VMEM((MOE_TILE * (MOE_NC // LANES), LANES), F32),
                        pltpu.VMEM((MOE_RING, EXPERT_FF, MOE_NC), F32),
                        pltpu.SemaphoreType.DMA((2,)),
                        pltpu.SemaphoreType.DMA((MOE_RING,))],
    )
    ys = pl.pallas_call(
        functools.partial(_moe_kernel, layer=layer),
        grid_spec=grid_spec,
        out_shape=jax.ShapeDtypeStruct((n_tiles * MOE_TILE, ROW_CHUNKS, LANES), BF16),
        compiler_params=_params(("arbitrary",), 56),
        name="moe_experts",
    )(te, nv, src, x1lin,
      w_gate.reshape(DEPTH, N_EXPERTS, D_MODEL, EXPERT_FF),
      w_up.reshape(DEPTH, N_EXPERTS, D_MODEL, EXPERT_FF),
      w_down.reshape(DEPTH, N_EXPERTS, EXPERT_FF, D_MODEL))
    return ys


def _ple_kernel(x1b_ref, wpg_ref, p_ref, wpu_ref, x1_ref, o_ref):
    gate = _sigmoid(_dot(x1b_ref[...], wpg_ref[...].astype(BF16)))
    up = _dot(p_ref[...].astype(BF16), wpu_ref[...].astype(BF16))
    o_ref[...] = ALPHA * x1_ref[...] + gate * up


def _ple(x1b, w_pg, p, w_pu, layer, x1):
    t = x1.shape[0]
    tm, tn = 1024, 512
    return pl.pallas_call(
        _ple_kernel,
        grid=(t // tm, D_MODEL // tn),
        in_specs=[pl.BlockSpec((tm, D_MODEL), lambda i, j: (i, 0)),
                  pl.BlockSpec((None, D_MODEL, tn), lambda i, j: (layer, 0, j)),
                  pl.BlockSpec((None, tm, PLE_DIM), lambda i, j: (layer, i, 0)),
                  pl.BlockSpec((None, PLE_DIM, tn), lambda i, j: (layer, 0, j)),
                  pl.BlockSpec((tm, tn), lambda i, j: (i, j))],
        out_specs=pl.BlockSpec((tm, tn), lambda i, j: (i, j)),
        out_shape=jax.ShapeDtypeStruct((t, D_MODEL), F32),
        compiler_params=_params(("parallel", "parallel"), 52),
        name="ple",
    )(x1b, w_pg, p, w_pu, x1)


def _final_kernel(pos1_ref, pos2_ref, pre_ref, route_ref, lnw_ref, lnb_ref, ys_hbm,
                  o_ref, stage, vbuf, sem):
    i = pl.program_id(0)
    n = pl.num_programs(0)
    tm = pre_ref.shape[0]
    slot = lax.rem(i, 2)

    def issue(tile, sl):
        base = tile * tm

        def body(r, carry):
            dst = pl.ds(r * ROW_CHUNKS, ROW_CHUNKS)
            pltpu.make_async_copy(ys_hbm.at[pl.ds(pos1_ref[base + r] * ROW_CHUNKS, ROW_CHUNKS)],
                                  stage.at[sl, 0, dst], sem.at[sl]).start(priority=0)
            pltpu.make_async_copy(ys_hbm.at[pl.ds(pos2_ref[base + r] * ROW_CHUNKS, ROW_CHUNKS)],
                                  stage.at[sl, 1, dst], sem.at[sl]).start(priority=1)
            return carry

        lax.fori_loop(0, tm, body, 0)

    @pl.when(i == 0)
    def _():
        issue(0, 0)

    for k in range(2):
        pltpu.make_async_copy(ys_hbm.at[pl.ds(0, tm * ROW_CHUNKS)], stage.at[slot, k],
                              sem.at[slot]).wait()

    @pl.when(i + 1 < n)
    def _():
        issue(i + 1, 1 - slot)

    def combine(a, carry):
        r0 = pl.multiple_of(a * SUBLANES, SUBLANES)
        route = route_ref[pl.ds(r0, SUBLANES), :]
        w1 = jnp.broadcast_to(route[:, 2:3], (SUBLANES, LANES))
        w2 = jnp.broadcast_to(route[:, 3:4], (SUBLANES, LANES))
        pack = 2 * SUBLANES
        for qq in range(ROW_CHUNKS // pack):
            tiles = []
            for slot_half in range(2):
                blks = [stage[slot, slot_half, pl.ds(pl.multiple_of(
                    (r0 + b) * ROW_CHUNKS + qq * pack, pack), pack), :].astype(F32)
                    for b in range(SUBLANES)]
                tiles.append(_transpose8([v[:SUBLANES] for v in blks])
                             + _transpose8([v[SUBLANES:] for v in blks]))
            for k in range(pack):
                s = qq * pack + k
                vbuf[pl.ds(r0, SUBLANES), s * LANES:(s + 1) * LANES] = w1 * tiles[0][k] + w2 * tiles[1][k]
        return carry

    lax.fori_loop(0, tm // SUBLANES, combine, 0)
    o_ref[...] = _layer_norm(pre_ref[...] + vbuf[...], lnw_ref[...], lnb_ref[...])


def _final(pos1, pos2, pre2, route, ln_w, ln_b, ys_lin):
    t, d = pre2.shape
    tm = 256
    grid_spec = pltpu.PrefetchScalarGridSpec(
        num_scalar_prefetch=2,
        grid=(t // tm,),
        in_specs=[pl.BlockSpec((tm, d), lambda i, p1, p2: (i, 0)),
                  pl.BlockSpec((tm, ROUTE_LANES), lambda i, p1, p2: (i, 0)),
                  pl.BlockSpec((1, d), lambda i, p1, p2: (0, 0)),
                  pl.BlockSpec((1, d), lambda i, p1, p2: (0, 0)),
                  pl.BlockSpec(memory_space=pl.ANY)],
        out_specs=pl.BlockSpec((tm, d), lambda i, p1, p2: (i, 0)),
        scratch_shapes=[pltpu.VMEM((2, 2, tm * ROW_CHUNKS, LANES), BF16),
                        pltpu.VMEM((tm, d), F32),
                        pltpu.SemaphoreType.DMA((2,))],
    )
    return pl.pallas_call(
        _final_kernel,
        grid_spec=grid_spec,
        out_shape=jax.ShapeDtypeStruct((t, d), F32),
        compiler_params=_params(("arbitrary",), 56),
        name="combine_ln2",
    )(pos1, pos2, pre2, route, ln_w, ln_b, ys_lin)


def _prep_q_weight(w_q_up):
    r = w_q_up.shape[0]
    wq = w_q_up.reshape(r, MLA_HEADS, MLA_NOPE + MLA_ROPE)
    z = jnp.zeros((r, MLA_HEADS, LANES - MLA_ROPE), w_q_up.dtype)
    return jnp.concatenate([wq, z], axis=-1).reshape(r, MLA_HEADS * 2 * LANES).astype(BF16)


def _layer(layer, x, p, tabs, batch, seq, w_in, q_norm, kv_norm, w_q_up, w_kv_up, sink, w_branch_a,
           w_branch_b, w_out, ln1_w, ln1_b, w_group, b_group, w_expert_router, b_expert,
           w_gate, w_up, w_down, w_ple_up, w_ple_gate, ln2_w, ln2_b):
    t = x.shape[0]
    cos_a, sin_a, cos_b, sin_b = tabs

    w_lat, w_sh = _prep_input_weight(w_in, layer)
    xb, cq, ckv, kr = _latent_proj(x, w_lat, q_norm[layer][None, :], kv_norm[layer][None, :],
                                   cos_a, sin_a)
    qf = _q_up(cq, _prep_q_weight(w_q_up[layer]), cos_a, sin_a)
    kf, v_a = _kv_up(ckv, w_kv_up, layer, kr)
    o_a = _mla_attention(qf, kf, v_a, batch, seq)

    qkv_b = _swa_proj(xb, w_sh, cos_b, sin_b)
    sink_rows = jnp.broadcast_to(
        jnp.repeat(sink[layer].astype(F32).reshape(SWA_KV_HEADS, SWA_GROUP), SWA_BLOCK, axis=1)[:, :, None],
        (SWA_KV_HEADS, SWA_GROUP * SWA_BLOCK, LANES))
    o_b = _swa_attention(qkv_b, sink_rows, batch, seq)

    merged = _branch_merge(xb, o_a, o_b, w_sh, w_branch_a, w_branch_b, layer)
    pre1 = _out_proj(merged, w_out, layer, x)

    pad = ROUTE_LANES - N_GROUPS - N_EXPERTS
    w_r = jnp.concatenate([w_group[layer], w_expert_router[layer], jnp.zeros((D_MODEL, pad), F32)],
                          axis=1)
    b_r = jnp.concatenate([b_group[layer], b_expert[layer], jnp.zeros((pad,), F32)])[None, :]
    x1, x1b, x1lin, route, route_t, counts = _ln_route(pre1, ln1_w[layer][None, :],
                                                       ln1_b[layer][None, :], w_r, b_r)

    n_tiles = (2 * t) // MOE_TILE + N_EXPERTS
    ri = route_t.astype(jnp.int32)
    cnt = counts[0, EXPERT_LANE0:EXPERT_LANE0 + N_EXPERTS].astype(jnp.int32)
    pos1, pos2, src, te, nv = _dispatch_plan(ri[6], ri[7], cnt, n_tiles)

    ys = _moe_experts(te, nv, src, x1lin, w_gate, w_up, w_down, layer, n_tiles)
    ys_lin = ys.reshape(n_tiles * MOE_TILE * ROW_CHUNKS, LANES)

    pre2 = _ple(x1b, w_ple_gate, p, w_ple_up, layer, x1)
    return _final(pos1, pos2, pre2, route, ln2_w[layer][None, :], ln2_b[layer][None, :], ys_lin)


def kernel(x, p, positions, w_in, q_norm, kv_norm, w_q_up, w_kv_up, sink, w_branch_a, w_branch_b,
           w_out, ln1_w, ln1_b, w_group, b_group, w_expert_router, b_expert, w_gate, w_up, w_down,
           w_ple_up, w_ple_gate, ln2_w, ln2_b):
    batch, seq, d = x.shape
    t = batch * seq
    h = x.reshape(t, d)
    tabs = _rope_tables(positions.reshape(t, 1))
    p2 = p.reshape(p.shape[0], t, PLE_DIM)
    for layer in range(w_in.shape[0]):
        h = _layer(layer, h, p2, tabs, batch, seq, w_in, q_norm, kv_norm, w_q_up, w_kv_up, sink,
                   w_branch_a, w_branch_b, w_out, ln1_w, ln1_b, w_group, b_group, w_expert_router,
                   b_expert, w_gate, w_up, w_down, w_ple_up, w_ple_gate, ln2_w, ln2_b)
    return h.reshape(batch, seq, d)
```

```python
import functools
import math

import numpy as np
import jax
import jax.numpy as jnp
from jax import lax
from jax.experimental import pallas as pl
from jax.experimental.pallas import tpu as pltpu

F32 = jnp.float32
BF16 = jnp.bfloat16

D_MODEL = 4096
MLA_HEADS = 16
MLA_NOPE = 128
MLA_ROPE = 64
MLA_V = 128
Q_RANK = 768
KV_RANK = 512
SWA_HEADS = 16
SWA_KV_HEADS = 4
SWA_GROUP = SWA_HEADS // SWA_KV_HEADS
SWA_DIM = 128
WINDOW = 128
ROPE_THETA = 10000.0
N_GROUPS = 8
EXPERTS_PER_GROUP = 8
N_EXPERTS = N_GROUPS * EXPERTS_PER_GROUP
EXPERT_FF = 1024
PLE_DIM = 256
LN_EPS = 1e-5
RMS_EPS = 1e-6
DEPTH = 1
ALPHA = (2.0 * DEPTH) ** 0.25
LOG2E = math.log2(math.e)

OFF_KR = Q_RANK + KV_RANK
OFF_QB = OFF_KR + MLA_ROPE
SWA_COLS = (SWA_HEADS + 2 * SWA_KV_HEADS) * SWA_DIM
IN_WIDTH = OFF_QB + SWA_COLS + 2 * D_MODEL

LANES = 128
MIB = 1024 * 1024

ROW_CHUNKS = D_MODEL // LANES

MOE_TILE = 336
MOE_KC = 512
MOE_NA = D_MODEL // MOE_KC
MOE_NC = 1024
MOE_NB = D_MODEL // MOE_NC
MOE_STEPS = MOE_NA + MOE_NB
MOE_ROWS_PER_STEP = MOE_TILE // MOE_STEPS
MOE_RING = 4
assert MOE_STEPS % MOE_RING == 0
ROUTE_LANES = LANES
EXPERT_LANE0 = N_GROUPS
ROUTE_FIELDS = 8


def _params(semantics, vmem_mib):
    return pltpu.CompilerParams(dimension_semantics=semantics,
                                vmem_limit_bytes=vmem_mib * MIB)


def _sigmoid(v):
    return 1.0 / (1.0 + jnp.exp(-v))


def _dot(a, b):
    return jnp.dot(a, b, preferred_element_type=F32)


SUBLANES = 8


def _transpose8(vs):
    sub = lax.broadcasted_iota(jnp.int32, (SUBLANES, LANES), 0)
    vs = list(vs)
    for d in (4, 2, 1):
        low = (sub & d) == 0
        for i in range(SUBLANES):
            if i & d:
                continue
            a, b = vs[i], vs[i + d]
            vs[i] = jnp.where(low, a, pltpu.roll(b, d, 0))
            vs[i + d] = jnp.where(low, pltpu.roll(a, SUBLANES - d, 0), b)
    return vs


def _rot_half_64(v):
    lane = lax.broadcasted_iota(jnp.int32, v.shape, 1)
    half = MLA_ROPE // 2
    fwd = pltpu.roll(v, half, 1)
    bwd = pltpu.roll(v, LANES - half, 1)
    return jnp.where(lane < half, -bwd, jnp.where(lane < MLA_ROPE, fwd, 0.0))


def _tables_kernel(pos_ref, inv_a_ref, inv_b_ref, sign_b_ref, cos_a, sin_a, cos_b, sin_b):
    pos = pos_ref[...].astype(F32)
    ang_a = pos * inv_a_ref[...]
    ang_b = pos * inv_b_ref[...]
    cos_a[...] = jnp.cos(ang_a)
    sin_a[...] = jnp.sin(ang_a)
    cos_b[...] = jnp.cos(ang_b)
    sin_b[...] = jnp.sin(ang_b) * sign_b_ref[...]


def _rope_tables(positions):
    t = positions.shape[0]
    tm = 1024
    inv64 = ROPE_THETA ** (-np.arange(0, MLA_ROPE, 2, dtype=np.float32) / MLA_ROPE)
    inv128 = ROPE_THETA ** (-np.arange(0, SWA_DIM, 2, dtype=np.float32) / SWA_DIM)
    inv_a = jnp.asarray(np.tile(inv64, 4)[None, :], F32)
    inv_b = jnp.asarray(np.tile(inv128, 2)[None, :], F32)
    sign_b = jnp.asarray(np.concatenate([-np.ones(64), np.ones(64)])[None, :], F32)
    row = pl.BlockSpec((tm, LANES), lambda i: (i, 0))
    const = pl.BlockSpec((1, LANES), lambda i: (0, 0))
    out = jax.ShapeDtypeStruct((t, LANES), F32)
    return pl.pallas_call(
        _tables_kernel,
        grid=(t // tm,),
        in_specs=[pl.BlockSpec((tm, 1), lambda i: (i, 0)), const, const, const],
        out_specs=[row, row, row, row],
        out_shape=[out, out, out, out],
        compiler_params=_params(("parallel",), 32),
        name="rope_tables",
    )(positions, inv_a, inv_b, sign_b)


LAT_COLS = OFF_KR + LANES
SHIFT_COLS = IN_WIDTH - OFF_QB
CAST_TILE = 1024


def _cast_transpose_kernel(w_ref, o_ref):
    o_ref[...] = jnp.transpose(w_ref[...]).astype(BF16)


def _prep_input_weight(w_in, layer):
    d = w_in.shape[1]
    w_t = jnp.swapaxes(w_in, 1, 2)
    tk = 512
    w_lat = pl.pallas_call(
        _cast_transpose_kernel,
        grid=(d // tk,),
        in_specs=[pl.BlockSpec((None, LAT_COLS, tk), lambda i: (layer, 0, i))],
        out_specs=pl.BlockSpec((tk, LAT_COLS), lambda i: (i, 0)),
        out_shape=jax.ShapeDtypeStruct((d, LAT_COLS), BF16),
        compiler_params=_params(("parallel",), 32),
        name="cast_latent_w",
    )(w_t)
    w_sh = pl.pallas_call(
        _cast_transpose_kernel,
        grid=(d // CAST_TILE, SHIFT_COLS // CAST_TILE),
        in_specs=[pl.BlockSpec((pl.Element(CAST_TILE), pl.Element(CAST_TILE)),
                               lambda i, j: (pl.multiple_of(OFF_QB + CAST_TILE * j, LANES // 2),
                                             pl.multiple_of(CAST_TILE * i, CAST_TILE)))],
        out_specs=pl.BlockSpec((CAST_TILE, CAST_TILE), lambda i, j: (i, j)),
        out_shape=jax.ShapeDtypeStruct((d, SHIFT_COLS), BF16),
        compiler_params=_params(("parallel", "parallel"), 32),
        name="cast_shift_w",
    )(w_t[layer])
    return w_lat, w_sh


def _latent_kernel(x_ref, w_ref, qn_ref, kvn_ref, cos_ref, sin_ref,
                   xb_ref, cq_ref, ckv_ref, kr_ref):
    xb = x_ref[...].astype(BF16)
    xb_ref[...] = xb
    acc = _dot(xb, w_ref[...])
    cq = acc[:, :Q_RANK]
    ckv = acc[:, Q_RANK:OFF_KR]
    cq = cq * lax.rsqrt(jnp.mean(cq * cq, axis=-1, keepdims=True) + RMS_EPS) * qn_ref[...]
    ckv = ckv * lax.rsqrt(jnp.mean(ckv * ckv, axis=-1, keepdims=True) + RMS_EPS) * kvn_ref[...]
    cq_ref[...] = cq.astype(BF16)
    ckv_ref[...] = ckv.astype(BF16)
    k = acc[:, OFF_KR:]
    lane = lax.broadcasted_iota(jnp.int32, k.shape, 1)
    kr = jnp.where(lane < MLA_ROPE, k * cos_ref[...], 0.0) + _rot_half_64(k) * sin_ref[...]
    kr_ref[...] = kr.astype(BF16)


def _latent_proj(x, w_lat, q_norm, kv_norm, cos_a, sin_a):
    t = x.shape[0]
    tm = 512
    row = lambda n: pl.BlockSpec((tm, n), lambda i: (i, 0))
    const = lambda r, n: pl.BlockSpec((r, n), lambda i: (0, 0))
    resident_w = pl.BlockSpec((D_MODEL, LAT_COLS), lambda i: (0, 0), pipeline_mode=pl.Buffered(1))
    return pl.pallas_call(
        _latent_kernel,
        grid=(t // tm,),
        in_specs=[row(D_MODEL), resident_w, const(1, Q_RANK), const(1, KV_RANK),
                  row(LANES), row(LANES)],
        out_specs=[row(D_MODEL), row(Q_RANK), row(KV_RANK), row(LANES)],
        out_shape=[jax.ShapeDtypeStruct((t, D_MODEL), BF16),
                   jax.ShapeDtypeStruct((t, Q_RANK), BF16),
                   jax.ShapeDtypeStruct((t, KV_RANK), BF16),
                   jax.ShapeDtypeStruct((t, LANES), BF16)],
        compiler_params=_params(("parallel",), 52),
        name="latent_proj",
    )(x, w_lat, q_norm, kv_norm, cos_a, sin_a)


UP_HEADS_PER_STEP = 4


def _qup_kernel(cq_ref, wa_ref, cos_ref, sin_ref, q_ref, *, scale):
    a = _dot(cq_ref[...], wa_ref[...])
    cos = cos_ref[...] * scale
    sin = sin_ref[...] * scale
    for h in range(UP_HEADS_PER_STEP):
        c0 = h * 2 * LANES
        r = a[:, c0 + LANES:c0 + 2 * LANES]
        q_ref[:, c0:c0 + LANES] = (a[:, c0:c0 + LANES] * scale).astype(BF16)
        q_ref[:, c0 + LANES:c0 + 2 * LANES] = (r * cos + _rot_half_64(r) * sin).astype(BF16)


def _q_up(cq, w_qa, cos_a, sin_a):
    t = cq.shape[0]
    tm = 1024
    tn = UP_HEADS_PER_STEP * 2 * LANES
    scale = float((MLA_NOPE + MLA_ROPE) ** -0.5 * LOG2E)
    return pl.pallas_call(
        functools.partial(_qup_kernel, scale=scale),
        grid=(t // tm, MLA_HEADS // UP_HEADS_PER_STEP),
        in_specs=[pl.BlockSpec((tm, Q_RANK), lambda i, j: (i, 0)),
                  pl.BlockSpec((Q_RANK, tn), lambda i, j: (0, j)),
                  pl.BlockSpec((tm, LANES), lambda i, j: (i, 0)),
                  pl.BlockSpec((tm, LANES), lambda i, j: (i, 0))],
        out_specs=pl.BlockSpec((tm, tn), lambda i, j: (i, j)),
        out_shape=jax.ShapeDtypeStruct((t, MLA_HEADS * 2 * LANES), BF16),
        compiler_params=_params(("parallel", "parallel"), 32),
        name="q_up",
    )(cq, w_qa, cos_a, sin_a)


def _kvup_kernel(ckv_ref, w_ref, kr_ref, k_ref, v_ref):
    acc = _dot(ckv_ref[...], w_ref[...].astype(BF16))
    kr = kr_ref[...]
    for h in range(UP_HEADS_PER_STEP):
        c0 = h * 2 * LANES
        k_ref[:, c0:c0 + LANES] = acc[:, c0:c0 + LANES].astype(BF16)
        k_ref[:, c0 + LANES:c0 + 2 * LANES] = kr
        v_ref[:, h * MLA_V:(h + 1) * MLA_V] = acc[:, c0 + LANES:c0 + 2 * LANES].astype(BF16)


def _kv_up(ckv, w_kv_up, layer, kr):
    t = ckv.shape[0]
    tm = 1024
    tn = UP_HEADS_PER_STEP * 2 * LANES
    return pl.pallas_call(
        _kvup_kernel,
        grid=(t // tm, MLA_HEADS // UP_HEADS_PER_STEP),
        in_specs=[pl.BlockSpec((tm, KV_RANK), lambda i, j: (i, 0)),
                  pl.BlockSpec((None, KV_RANK, tn), lambda i, j: (layer, 0, j)),
                  pl.BlockSpec((tm, LANES), lambda i, j: (i, 0))],
        out_specs=[pl.BlockSpec((tm, tn), lambda i, j: (i, j)),
                   pl.BlockSpec((tm, UP_HEADS_PER_STEP * MLA_V), lambda i, j: (i, j))],
        out_shape=[jax.ShapeDtypeStruct((t, MLA_HEADS * 2 * LANES), BF16),
                   jax.ShapeDtypeStruct((t, MLA_HEADS * MLA_V), BF16)],
        compiler_params=_params(("parallel", "parallel"), 32),
        name="kv_up",
    )(ckv, w_kv_up, kr)


SWA_TN = 512
SWA_Q_TILES = SWA_HEADS * SWA_DIM // SWA_TN
SWA_ROPE_TILES = (SWA_HEADS + SWA_KV_HEADS) * SWA_DIM // SWA_TN


def _swaproj_kernel(xb_ref, w_ref, cos_ref, sin_ref, o_ref, *, scale):
    j = pl.program_id(1)
    acc = _dot(xb_ref[...], w_ref[...])

    @pl.when(j < SWA_ROPE_TILES)
    def _():
        sc = jnp.where(j < SWA_Q_TILES, scale, 1.0).astype(F32)
        cos = cos_ref[...] * sc
        sin = sin_ref[...] * sc
        for c in range(SWA_TN // LANES):
            a = acc[:, c * LANES:(c + 1) * LANES]
            r = a * cos + pltpu.roll(a, SWA_DIM // 2, 1) * sin
            o_ref[:, c * LANES:(c + 1) * LANES] = r.astype(BF16)

    @pl.when(j >= SWA_ROPE_TILES)
    def _():
        o_ref[...] = acc.astype(BF16)


def _swa_proj(xb, w_sh, cos_b, sin_b):
    t = xb.shape[0]
    tm = 1024
    return pl.pallas_call(
        functools.partial(_swaproj_kernel, scale=float(SWA_DIM ** -0.5 * LOG2E)),
        grid=(t // tm, SWA_COLS // SWA_TN),
        in_specs=[pl.BlockSpec((tm, D_MODEL), lambda i, j: (i, 0)),
                  pl.BlockSpec((D_MODEL, SWA_TN), lambda i, j: (0, j)),
                  pl.BlockSpec((tm, LANES), lambda i, j: (i, 0)),
                  pl.BlockSpec((tm, LANES), lambda i, j: (i, 0))],
        out_specs=pl.BlockSpec((tm, SWA_TN), lambda i, j: (i, j)),
        out_shape=jax.ShapeDtypeStruct((t, SWA_COLS), BF16),
        compiler_params=_params(("parallel", "parallel"), 48),
        name="swa_proj",
    )(xb, w_sh, cos_b, sin_b)


MLA_HEADS_PER_STEP = 4


def _mla_kernel(q_ref, k_ref, v_ref, o_ref):
    for h in range(MLA_HEADS_PER_STEP):
        q = q_ref[:, h * 2 * LANES:(h + 1) * 2 * LANES]
        k = k_ref[:, h * 2 * LANES:(h + 1) * 2 * LANES]
        s = lax.dot_general(q, k, (((1,), (1,)), ((), ())), preferred_element_type=F32)
        m = jnp.max(s, axis=-1, keepdims=True)
        p = jnp.exp2(s - m)
        l = jnp.sum(p, axis=-1, keepdims=True)
        o = _dot(p.astype(BF16), v_ref[:, h * MLA_V:(h + 1) * MLA_V])
        o_ref[:, h * MLA_V:(h + 1) * MLA_V] = (o / l).astype(BF16)


def _mla_attention(qf, kf, v, batch, seq):
    t = qf.shape[0]
    tq = 512
    nq = seq // tq
    hp = MLA_HEADS_PER_STEP
    return pl.pallas_call(
        _mla_kernel,
        grid=(batch, MLA_HEADS // hp, nq),
        in_specs=[pl.BlockSpec((tq, hp * 2 * LANES), lambda b, h, i: (b * nq + i, h)),
                  pl.BlockSpec((seq, hp * 2 * LANES), lambda b, h, i: (b, h)),
                  pl.BlockSpec((seq, hp * MLA_V), lambda b, h, i: (b, h))],
        out_specs=pl.BlockSpec((tq, hp * MLA_V), lambda b, h, i: (b * nq + i, h)),
        out_shape=jax.ShapeDtypeStruct((t, MLA_HEADS * MLA_V), BF16),
        compiler_params=_params(("parallel", "parallel", "parallel"), 52),
        name="mla_attention",
    )(qf, kf, v)


SWA_BLOCK = 128
SWA_TQ = 512
SWA_WIN = SWA_TQ + 2 * WINDOW


def _swa_kernel(q_ref, k_ref, v_ref, sink_ref, bias_ref, o_ref, *, seq):
    step = pl.program_id(2)
    q0 = step * SWA_TQ
    ks = pl.multiple_of(jnp.clip(q0 - WINDOW, 0, seq - SWA_WIN), WINDOW)
    bias = bias_ref[lax.div(q0 - ks, WINDOW)]
    kw = k_ref[pl.ds(ks, SWA_WIN), :]
    vw = v_ref[pl.ds(ks, SWA_WIN), :]
    for g in range(SWA_GROUP):
        q = q_ref[:, g * SWA_DIM:(g + 1) * SWA_DIM]
        s = lax.dot_general(q, kw, (((1,), (1,)), ((), ())), preferred_element_type=F32) + bias
        sink = sink_ref[g * SWA_BLOCK:g * SWA_BLOCK + 1, :1] * LOG2E
        m = jnp.maximum(jnp.max(s, axis=-1, keepdims=True), sink)
        e = jnp.exp2(s - m)
        denom = jnp.sum(e, axis=-1, keepdims=True) + jnp.exp2(sink - m)
        o = _dot(e.astype(BF16), vw) / denom
        o_ref[:, g * SWA_DIM:(g + 1) * SWA_DIM] = o.astype(BF16)


def _swa_band_bias():
    r = np.arange(SWA_TQ)[:, None]
    c = np.arange(SWA_WIN)[None, :]
    variants = [np.where(np.abs(v * WINDOW + r - c) <= WINDOW, 0.0, -np.inf) for v in range(3)]
    return jnp.asarray(np.stack(variants), F32)


def _swa_attention(qkv, sink_rows, batch, seq):
    t = qkv.shape[0]
    nq = seq // SWA_TQ
    gw = SWA_GROUP * SWA_DIM
    k_col0 = SWA_HEADS
    v_col0 = SWA_HEADS + SWA_KV_HEADS
    return pl.pallas_call(
        functools.partial(_swa_kernel, seq=seq),
        grid=(batch, SWA_KV_HEADS, nq),
        in_specs=[pl.BlockSpec((SWA_TQ, gw), lambda b, h, n: (b * nq + n, h)),
                  pl.BlockSpec((seq, SWA_DIM), lambda b, h, n: (b, k_col0 + h)),
                  pl.BlockSpec((seq, SWA_DIM), lambda b, h, n: (b, v_col0 + h)),
                  pl.BlockSpec((None, SWA_GROUP * SWA_BLOCK, LANES), lambda b, h, n: (h, 0, 0)),
                  pl.BlockSpec((3, SWA_TQ, SWA_WIN), lambda b, h, n: (0, 0, 0))],
        out_specs=pl.BlockSpec((SWA_TQ, gw), lambda b, h, n: (b * nq + n, h)),
        out_shape=jax.ShapeDtypeStruct((t, SWA_HEADS * SWA_DIM), BF16),
        compiler_params=_params(("parallel", "parallel", "parallel"), 48),
        name="swa_attention",
    )(qkv, qkv, qkv, sink_rows, _swa_band_bias())


MERGE_TN = 256
GATE_A_BLOCK0 = SWA_COLS // MERGE_TN
GATE_B_BLOCK0 = (SWA_COLS + D_MODEL) // MERGE_TN


def _merge_kernel(xb_ref, oa_ref, ob_ref, wga_ref, wgb_ref, wa_ref, wb_ref, o_ref):
    xb = xb_ref[...]
    ga = _sigmoid(_dot(xb, wga_ref[...]))
    ya = _dot(oa_ref[...], wa_ref[...].astype(BF16))
    acc = ga * ya
    gb = _sigmoid(_dot(xb, wgb_ref[...]))
    yb = _dot(ob_ref[...], wb_ref[...].astype(BF16))
    o_ref[...] = (acc + gb * yb).astype(BF16)


def _branch_merge(xb, o_a, o_b, w_sh, w_a, w_b, layer):
    t = xb.shape[0]
    tm, tn = 512, MERGE_TN
    ka = o_a.shape[1]
    row = lambda k: pl.BlockSpec((tm, k), lambda i, j: (i, 0))
    return pl.pallas_call(
        _merge_kernel,
        grid=(t // tm, D_MODEL // tn),
        in_specs=[row(D_MODEL), row(ka), row(ka),
                  pl.BlockSpec((D_MODEL, tn), lambda i, j: (0, GATE_A_BLOCK0 + j)),
                  pl.BlockSpec((D_MODEL, tn), lambda i, j: (0, GATE_B_BLOCK0 + j)),
                  pl.BlockSpec((None, ka, tn), lambda i, j: (layer, 0, j)),
                  pl.BlockSpec((None, ka, tn), lambda i, j: (layer, 0, j))],
        out_specs=pl.BlockSpec((tm, tn), lambda i, j: (i, j)),
        out_shape=jax.ShapeDtypeStruct((t, D_MODEL), BF16),
        compiler_params=_params(("parallel", "parallel"), 48),
        name="branch_merge",
    )(xb, o_a, o_b, w_sh, w_sh, w_a, w_b)


def _outproj_kernel(m_ref, w_ref, x_ref, o_ref):
    o_ref[...] = ALPHA * x_ref[...] + _dot(m_ref[...], w_ref[...].astype(BF16))


def _out_proj(merged, w_out, layer, x):
    t = x.shape[0]
    tm, tn = 1024, 512
    return pl.pallas_call(
        _outproj_kernel,
        grid=(t // tm, D_MODEL // tn),
        in_specs=[pl.BlockSpec((tm, D_MODEL), lambda i, j: (i, 0)),
                  pl.BlockSpec((None, D_MODEL, tn), lambda i, j: (layer, 0, j)),
                  pl.BlockSpec((tm, tn), lambda i, j: (i, j))],
        out_specs=pl.BlockSpec((tm, tn), lambda i, j: (i, j)),
        out_shape=jax.ShapeDtypeStruct((t, D_MODEL), F32),
        compiler_params=_params(("parallel", "parallel"), 52),
        name="out_proj",
    )(merged, w_out, x)


def _layer_norm(v, w, b):
    mu = jnp.mean(v, axis=-1, keepdims=True)
    c = v - mu
    var = jnp.mean(c * c, axis=-1, keepdims=True)
    return c * lax.rsqrt(var + LN_EPS) * w + b


def _split_bf16(v):
    hi = v.astype(BF16)
    lo = (v - hi.astype(F32)).astype(BF16)
    return hi, lo


def _ln_route_kernel(pre_ref, lnw_ref, lnb_ref, wr_ref, br_ref,
                     x1_ref, x1b_ref, x1lin_ref, route_ref, route_t_ref, cnt_ref, carry):
    i = pl.program_id(0)

    @pl.when(i == 0)
    def _():
        carry[...] = jnp.zeros_like(carry)

    x1 = _layer_norm(pre_ref[...], lnw_ref[...], lnb_ref[...])
    tm = x1.shape[0]
    x1_ref[...] = x1
    x1b_ref[...] = x1.astype(BF16)

    def to_row_linear(a, carry):
        r0 = pl.multiple_of(a * SUBLANES, SUBLANES)
        pack = 2 * SUBLANES
        for qq in range(ROW_CHUNKS // pack):
            parts = []
            for q in (2 * qq, 2 * qq + 1):
                tiles = [x1_ref[pl.ds(r0, SUBLANES),
                                (q * SUBLANES + k) * LANES:(q * SUBLANES + k + 1) * LANES]
                         for k in range(SUBLANES)]
                parts.append(_transpose8(tiles))
            for b in range(SUBLANES):
                dst = pl.multiple_of((r0 + b) * ROW_CHUNKS + qq * pack, pack)
                x1lin_ref[pl.ds(dst, pack), :] = jnp.concatenate(
                    [parts[0][b], parts[1][b]], axis=0).astype(BF16)
        return carry

    lax.fori_loop(0, tm // SUBLANES, to_row_linear, 0)

    xh, xl = _split_bf16(x1)
    wh, wl = _split_bf16(wr_ref[...])
    logits = _dot(xh, wh) + (_dot(xl, wh) + _dot(xh, wl)) + br_ref[...]

    lane = lax.broadcasted_iota(jnp.int32, logits.shape, 1)
    neg = -jnp.inf
    gl = jnp.where(lane < N_GROUPS, logits, neg)
    gmax = jnp.max(gl, axis=-1, keepdims=True)
    gsum = jnp.sum(jnp.exp(gl - gmax), axis=-1, keepdims=True)
    g_w = 1.0 / gsum
    g_idx = jnp.min(jnp.where(gl == gmax, lane, ROUTE_LANES), axis=-1, keepdims=True)
    d = lane - (EXPERT_LANE0 + g_idx * EXPERTS_PER_GROUP)
    el = jnp.where((d >= 0) & (d < EXPERTS_PER_GROUP), logits, neg)
    m1 = jnp.max(el, axis=-1, keepdims=True)
    i1 = jnp.min(jnp.where(el == m1, lane, ROUTE_LANES), axis=-1, keepdims=True)
    el2 = jnp.where(lane == i1, neg, el)
    m2 = jnp.max(el2, axis=-1, keepdims=True)
    i2 = jnp.min(jnp.where(el2 == m2, lane, ROUTE_LANES), axis=-1, keepdims=True)
    e21 = jnp.exp(m2 - m1)
    w1 = g_w / (1.0 + e21)
    w2 = g_w * e21 / (1.0 + e21)

    oh1 = (lane == i1).astype(F32)
    oh2 = (lane == i2).astype(F32)
    oh = oh1 + oh2
    r_i = lax.broadcasted_iota(jnp.int32, (tm, tm), 0)
    c_i = lax.broadcasted_iota(jnp.int32, (tm, tm), 1)
    tri = jnp.where(c_i < r_i, 1.0, 0.0).astype(BF16)
    before = _dot(tri, oh.astype(BF16)) + carry[...]
    rank1 = jnp.sum(oh1 * before, axis=-1, keepdims=True)
    rank2 = jnp.sum(oh2 * before, axis=-1, keepdims=True)
    carry[...] = carry[...] + jnp.sum(oh, axis=0, keepdims=True)
    cnt_ref[...] = carry[...]

    slab = jnp.where(lane == 0, (i1 - EXPERT_LANE0).astype(F32), 0.0)
    slab = jnp.where(lane == 1, (i2 - EXPERT_LANE0).astype(F32), slab)
    slab = jnp.where(lane == 2, w1, slab)
    slab = jnp.where(lane == 3, w2, slab)
    slab = jnp.where(lane == 4, rank1, slab)
    slab = jnp.where(lane == 5, rank2, slab)
    slab = jnp.where(lane == 6, (i1 - EXPERT_LANE0).astype(F32) * float(1 << SLOT_SHIFT) + rank1, slab)
    slab = jnp.where(lane == 7, (i2 - EXPERT_LANE0).astype(F32) * float(1 << SLOT_SHIFT) + rank2, slab)
    route_ref[...] = slab
    route_t_ref[...] = jnp.transpose(slab)[:ROUTE_FIELDS, :]


def _ln_route(pre1, ln_w, ln_b, w_r, b_r):
    t = pre1.shape[0]
    tm = 256
    row = lambda n: pl.BlockSpec((tm, n), lambda i: (i, 0))
    const = lambda r, n: pl.BlockSpec((r, n), lambda i: (0, 0))
    return pl.pallas_call(
        _ln_route_kernel,
        grid=(t // tm,),
        in_specs=[row(D_MODEL), const(1, D_MODEL), const(1, D_MODEL),
                  const(D_MODEL, ROUTE_LANES), const(1, ROUTE_LANES)],
        out_specs=[row(D_MODEL), row(D_MODEL),
                   pl.BlockSpec((tm * ROW_CHUNKS, LANES), lambda i: (i, 0)),
                   row(ROUTE_LANES),
                   pl.BlockSpec((ROUTE_FIELDS, tm), lambda i: (0, i)),
                   const(1, ROUTE_LANES)],
        out_shape=[jax.ShapeDtypeStruct((t, D_MODEL), F32),
                   jax.ShapeDtypeStruct((t, D_MODEL), BF16),
                   jax.ShapeDtypeStruct((t * ROW_CHUNKS, LANES), BF16),
                   jax.ShapeDtypeStruct((t, ROUTE_LANES), F32),
                   jax.ShapeDtypeStruct((ROUTE_FIELDS, t), F32),
                   jax.ShapeDtypeStruct((1, ROUTE_LANES), F32)],
        scratch_shapes=[pltpu.VMEM((1, ROUTE_LANES), F32)],
        compiler_params=_params(("arbitrary",), 52),
        name="ln1_route",
    )(pre1, ln_w, ln_b, w_r, b_r)


SLOT_SHIFT = 16
SLOT_MASK = (1 << SLOT_SHIFT) - 1


def _plan_kernel(c1_ref, c2_ref, cnt_ref,
                 pos1_ref, pos2_ref, src_ref, te_ref, nv_ref, tstart, *, n_tokens, n_tiles):
    def per_expert(e, first_tile):
        tstart[e] = first_tile
        c = cnt_ref[e]
        nt = lax.div(c + (MOE_TILE - 1), MOE_TILE)

        def set_te(k, carry):
            te_ref[first_tile + k] = e
            return carry

        lax.fori_loop(0, nt, set_te, 0)

        def pad_src(k, carry):
            src_ref[first_tile * MOE_TILE + k] = 0
            return carry

        lax.fori_loop(c, nt * MOE_TILE, pad_src, 0)
        return first_tile + nt

    nv = lax.fori_loop(0, N_EXPERTS, per_expert, 0)
    nv_ref[0] = nv
    last = te_ref[nv - 1]

    def tail_te(k, carry):
        te_ref[k] = last
        return carry

    lax.fori_loop(nv, n_tiles, tail_te, 0)

    def per_token(t, carry):
        c1 = c1_ref[t]
        c2 = c2_ref[t]
        p1 = tstart[lax.shift_right_logical(c1, SLOT_SHIFT)] * MOE_TILE + (c1 & SLOT_MASK)
        p2 = tstart[lax.shift_right_logical(c2, SLOT_SHIFT)] * MOE_TILE + (c2 & SLOT_MASK)
        pos1_ref[t] = p1
        pos2_ref[t] = p2
        src_ref[p1] = t
        src_ref[p2] = t
        return carry

    lax.fori_loop(0, n_tokens, per_token, 0, unroll=8)


def _dispatch_plan(c1, c2, cnt, n_tiles):
    t = c1.shape[0]
    assert 2 * t <= SLOT_MASK + 1
    smem = pl.BlockSpec(memory_space=pltpu.SMEM)
    grid_spec = pltpu.PrefetchScalarGridSpec(
        num_scalar_prefetch=3,
        grid=(1,),
        in_specs=[],
        out_specs=[smem, smem, smem, smem, smem],
        scratch_shapes=[pltpu.SMEM((N_EXPERTS,), jnp.int32)],
    )
    i32 = lambda n: jax.ShapeDtypeStruct((n,), jnp.int32)
    return pl.pallas_call(
        functools.partial(_plan_kernel, n_tokens=t, n_tiles=n_tiles),
        grid_spec=grid_spec,
        out_shape=[i32(t), i32(t), i32(n_tiles * MOE_TILE), i32(n_tiles), i32(1)],
        compiler_params=_params(("arbitrary",), 32),
        name="dispatch_plan",
    )(c1, c2, cnt)


def _moe_kernel(te_ref, nv_ref, src_ref, x_hbm, wg_hbm, wu_hbm, wd_hbm, ys_ref,
                stage, xb, hg, hu, act, ytmp, ring, sem, wsem, *, layer):
    i = pl.program_id(0)
    nv = nv_ref[0]
    slot = lax.rem(i, 2)
    tile_rows = MOE_TILE * ROW_CHUNKS
    half_ff = EXPERT_FF // 2
    out_per = MOE_NC // LANES

    def issue_chunk(tile, step):
        sl = step % MOE_RING
        e = te_ref[tile]
        if step < MOE_NA:
            rows = pl.ds(step * MOE_KC, MOE_KC)
            pltpu.make_async_copy(wg_hbm.at[layer, e, rows, :],
                                  ring.at[sl, pl.ds(0, MOE_KC), :], wsem.at[sl]).start(priority=0)
            pltpu.make_async_copy(wu_hbm.at[layer, e, rows, :],
                                  ring.at[sl, pl.ds(MOE_KC, MOE_KC), :], wsem.at[sl]).start(priority=1)
        else:
            cols = pl.ds((step - MOE_NA) * MOE_NC, MOE_NC)
            for k in range(2):
                rows = pl.ds(k * half_ff, half_ff)
                pltpu.make_async_copy(wd_hbm.at[layer, e, rows, cols],
                                      ring.at[sl, rows, :], wsem.at[sl]).start(priority=k)

    def wait_chunk(step):
        sl = step % MOE_RING
        pltpu.make_async_copy(wd_hbm.at[layer, 0, :, pl.ds(0, MOE_NC)], ring.at[sl],
                              wsem.at[sl]).wait()

    def row_copy(tile, sl, r):
        tok = src_ref[tile * MOE_TILE + r]
        return pltpu.make_async_copy(x_hbm.at[pl.ds(tok * ROW_CHUNKS, ROW_CHUNKS)],
                                     stage.at[sl, pl.ds(r * ROW_CHUNKS, ROW_CHUNKS)], sem.at[sl])

    def issue_rows(tile, sl, r0, n):
        def body(k, carry):
            row_copy(tile, sl, r0 + 2 * k).start(priority=0)
            row_copy(tile, sl, r0 + 2 * k + 1).start(priority=1)
            return carry

        lax.fori_loop(0, n // 2, body, 0)

    def to_tiled(a, carry):
        per = MOE_KC // LANES
        pair = 2 * SUBLANES
        t0 = pl.multiple_of(a * pair, pair)
        for qq in range(ROW_CHUNKS // pair):
            halves = []
            for h in range(2):
                blks = [stage[slot, pl.ds(pl.multiple_of(
                    (t0 + h * SUBLANES + b) * ROW_CHUNKS + qq * pair, pair), pair), :].astype(F32)
                    for b in range(SUBLANES)]
                halves.append(_transpose8([v[:SUBLANES] for v in blks])
                              + _transpose8([v[SUBLANES:] for v in blks]))
            for k in range(pair):
                s = qq * pair + k
                blk = jnp.concatenate([halves[0][k], halves[1][k]], axis=0).astype(BF16)
                xb[s // per, pl.ds(t0, pair), (s % per) * LANES:(s % per + 1) * LANES] = blk
        return carry

    def up_gate_step(step):
        sl = step % MOE_RING
        xk = xb[step]
        pg = _dot(xk, ring[sl, 0:MOE_KC, :].astype(BF16))
        pu = _dot(xk, ring[sl, MOE_KC:2 * MOE_KC, :].astype(BF16))
        if step == 0:
            hg[...] = pg
            hu[...] = pu
        elif step < MOE_NA - 1:
            hg[...] += pg
            hu[...] += pu
        else:
            g = hg[...] + pg
            act[...] = (g * _sigmoid(g) * (hu[...] + pu)).astype(BF16)

    def down_step(step):
        sl = step % MOE_RING
        y = _dot(act[...], ring[sl].astype(BF16))
        ys2 = ys_ref.reshape(tile_rows, LANES)
        down = step - MOE_NA
        pack = 2 * out_per
        for a in range(MOE_TILE // SUBLANES):
            tiles = [y[a * SUBLANES:(a + 1) * SUBLANES, k * LANES:(k + 1) * LANES]
                     for k in range(out_per)]
            rows = _transpose8(tiles)
            for b in range(SUBLANES):
                r = a * SUBLANES + b
                if down % 2 == 0:
                    ytmp[r * out_per:(r + 1) * out_per, :] = rows[b]
                else:
                    r0 = r * ROW_CHUNKS + (down // 2) * pack
                    ys2[r0:r0 + pack, :] = jnp.concatenate(
                        [ytmp[r * out_per:(r + 1) * out_per, :], rows[b]], axis=0).astype(BF16)

    @pl.when(i < nv)
    def _():
        @pl.when(i == 0)
        def _():
            for c in range(MOE_RING - 1):
                issue_chunk(0, c)
            issue_rows(0, 0, 0, MOE_TILE)

        pltpu.make_async_copy(x_hbm.at[pl.ds(0, tile_rows)], stage.at[slot], sem.at[slot]).wait()
        lax.fori_loop(0, MOE_TILE // (2 * SUBLANES), to_tiled, 0)
        has_next = i + 1 < nv

        for step in range(MOE_STEPS):
            ahead = step + MOE_RING - 1
            if ahead < MOE_STEPS:
                issue_chunk(i, ahead)
            else:
                pl.when(has_next)(functools.partial(issue_chunk, i + 1, ahead - MOE_STEPS))
            pl.when(has_next)(functools.partial(
                issue_rows, i + 1, 1 - slot, step * MOE_ROWS_PER_STEP, MOE_ROWS_PER_STEP))
            wait_chunk(step)
            if step < MOE_NA:
                up_gate_step(step)
            else:
                down_step(step)


def _moe_experts(te, nv, src, x1lin, w_gate, w_up, w_down, layer, n_tiles):
    per = MOE_NC // LANES

    def out_map(i, te, nv, src):
        return (jnp.minimum(i, nv[0] - 1), 0, 0)

    assert 2 * MOE_KC == EXPERT_FF and MOE_NC == EXPERT_FF
    any_spec = pl.BlockSpec(memory_space=pl.ANY)
    grid_spec = pltpu.PrefetchScalarGridSpec(
        num_scalar_prefetch=3,
        grid=(n_tiles,),
        in_specs=[any_spec, any_spec, any_spec, any_spec],
        out_specs=pl.BlockSpec((MOE_TILE, ROW_CHUNKS, LANES), out_map),
        scratch_shapes=[pltpu.VMEM((2, MOE_TILE * ROW_CHUNKS, LANES), BF16),
                        pltpu.VMEM((MOE_NA, MOE_TILE, MOE_KC), BF16),
                        pltpu.VMEM((MOE_TILE, EXPERT_FF), F32),
                        pltpu.VMEM((MOE_TILE, EXPERT_FF), F32),
                        pltpu.VMEM((MOE_TILE, EXPERT_FF), BF16),
                        pltpu.VMEM((MOE_TILE * (MOE_NC // LANES), LANES), F32),
                        pltpu.VMEM((MOE_RING, EXPERT_FF, MOE_NC), F32),
                        pltpu.SemaphoreType.DMA((2,)),
                        pltpu.SemaphoreType.DMA((MOE_RING,))],
    )
    ys = pl.pallas_call(
        functools.partial(_moe_kernel, layer=layer),
        grid_spec=grid_spec,
        out_shape=jax.ShapeDtypeStruct((n_tiles * MOE_TILE, ROW_CHUNKS, LANES), BF16),
        compiler_params=_params(("arbitrary",), 56),
        name="moe_experts",
    )(te, nv, src, x1lin,
      w_gate.reshape(DEPTH, N_EXPERTS, D_MODEL, EXPERT_FF),
      w_up.reshape(DEPTH, N_EXPERTS, D_MODEL, EXPERT_FF),
      w_down.reshape(DEPTH, N_EXPERTS, EXPERT_FF, D_MODEL))
    return ys


def _ple_kernel(x1b_ref, wpg_ref, p_ref, wpu_ref, x1_ref, o_ref):
    gate = _sigmoid(_dot(x1b_ref[...], wpg_ref[...].astype(BF16)))
    up = _dot(p_ref[...].astype(BF16), wpu_ref[...].astype(BF16))
    o_ref[...] = ALPHA * x1_ref[...] + gate * up


def _ple(x1b, w_pg, p, w_pu, layer, x1):
    t = x1.shape[0]
    tm, tn = 1024, 512
    return pl.pallas_call(
        _ple_kernel,
        grid=(t // tm, D_MODEL // tn),
        in_specs=[pl.BlockSpec((tm, D_MODEL), lambda i, j: (i, 0)),
                  pl.BlockSpec((None, D_MODEL, tn), lambda i, j: (layer, 0, j)),
                  pl.BlockSpec((None, tm, PLE_DIM), lambda i, j: (layer, i, 0)),
                  pl.BlockSpec((None, PLE_DIM, tn), lambda i, j: (layer, 0, j)),
                  pl.BlockSpec((tm, tn), lambda i, j: (i, j))],
        out_specs=pl.BlockSpec((tm, tn), lambda i, j: (i, j)),
        out_shape=jax.ShapeDtypeStruct((t, D_MODEL), F32),
        compiler_params=_params(("parallel", "parallel"), 52),
        name="ple",
    )(x1b, w_pg, p, w_pu, x1)


def _final_kernel(pos1_ref, pos2_ref, pre_ref, route_ref, lnw_ref, lnb_ref, ys_hbm,
                  o_ref, stage, vbuf, sem):
    i = pl.program_id(0)
    n = pl.num_programs(0)
    tm = pre_ref.shape[0]
    slot = lax.rem(i, 2)

    def issue(tile, sl):
        base = tile * tm

        def body(r, carry):
            dst = pl.ds(r * ROW_CHUNKS, ROW_CHUNKS)
            pltpu.make_async_copy(ys_hbm.at[pl.ds(pos1_ref[base + r] * ROW_CHUNKS, ROW_CHUNKS)],
                                  stage.at[sl, 0, dst], sem.at[sl]).start(priority=0)
            pltpu.make_async_copy(ys_hbm.at[pl.ds(pos2_ref[base + r] * ROW_CHUNKS, ROW_CHUNKS)],
                                  stage.at[sl, 1, dst], sem.at[sl]).start(priority=1)
            return carry

        lax.fori_loop(0, tm, body, 0)

    @pl.when(i == 0)
    def _():
        issue(0, 0)

    for k in range(2):
        pltpu.make_async_copy(ys_hbm.at[pl.ds(0, tm * ROW_CHUNKS)], stage.at[slot, k],
                              sem.at[slot]).wait()

    @pl.when(i + 1 < n)
    def _():
        issue(i + 1, 1 - slot)

    def combine(a, carry):
        r0 = pl.multiple_of(a * SUBLANES, SUBLANES)
        route = route_ref[pl.ds(r0, SUBLANES), :]
        w1 = jnp.broadcast_to(route[:, 2:3], (SUBLANES, LANES))
        w2 = jnp.broadcast_to(route[:, 3:4], (SUBLANES, LANES))
        pack = 2 * SUBLANES
        for qq in range(ROW_CHUNKS // pack):
            tiles = []
            for slot_half in range(2):
                blks = [stage[slot, slot_half, pl.ds(pl.multiple_of(
                    (r0 + b) * ROW_CHUNKS + qq * pack, pack), pack), :].astype(F32)
                    for b in range(SUBLANES)]
                tiles.append(_transpose8([v[:SUBLANES] for v in blks])
                             + _transpose8([v[SUBLANES:] for v in blks]))
            for k in range(pack):
                s = qq * pack + k
                vbuf[pl.ds(r0, SUBLANES), s * LANES:(s + 1) * LANES] = w1 * tiles[0][k] + w2 * tiles[1][k]
        return carry

    lax.fori_loop(0, tm // SUBLANES, combine, 0)
    o_ref[...] = _layer_norm(pre_ref[...] + vbuf[...], lnw_ref[...], lnb_ref[...])


def _final(pos1, pos2, pre2, route, ln_w, ln_b, ys_lin):
    t, d = pre2.shape
    tm = 256
    grid_spec = pltpu.PrefetchScalarGridSpec(
        num_scalar_prefetch=2,
        grid=(t // tm,),
        in_specs=[pl.BlockSpec((tm, d), lambda i, p1, p2: (i, 0)),
                  pl.BlockSpec((tm, ROUTE_LANES), lambda i, p1, p2: (i, 0)),
                  pl.BlockSpec((1, d), lambda i, p1, p2: (0, 0)),
                  pl.BlockSpec((1, d), lambda i, p1, p2: (0, 0)),
                  pl.BlockSpec(memory_space=pl.ANY)],
        out_specs=pl.BlockSpec((tm, d), lambda i, p1, p2: (i, 0)),
        scratch_shapes=[pltpu.VMEM((2, 2, tm * ROW_CHUNKS, LANES), BF16),
                        pltpu.VMEM((tm, d), F32),
                        pltpu.SemaphoreType.DMA((2,))],
    )
    return pl.pallas_call(
        _final_kernel,
        grid_spec=grid_spec,
        out_shape=jax.ShapeDtypeStruct((t, d), F32),
        compiler_params=_params(("arbitrary",), 56),
        name="combine_ln2",
    )(pos1, pos2, pre2, route, ln_w, ln_b, ys_lin)


def _prep_q_weight(w_q_up):
    r = w_q_up.shape[0]
    wq = w_q_up.reshape(r, MLA_HEADS, MLA_NOPE + MLA_ROPE)
    z = jnp.zeros((r, MLA_HEADS, LANES - MLA_ROPE), w_q_up.dtype)
    return jnp.concatenate([wq, z], axis=-1).reshape(r, MLA_HEADS * 2 * LANES).astype(BF16)


def _layer(layer, x, p, tabs, batch, seq, w_in, q_norm, kv_norm, w_q_up, w_kv_up, sink, w_branch_a,
           w_branch_b, w_out, ln1_w, ln1_b, w_group, b_group, w_expert_router, b_expert,
           w_gate, w_up, w_down, w_ple_up, w_ple_gate, ln2_w, ln2_b):
    t = x.shape[0]
    cos_a, sin_a, cos_b, sin_b = tabs

    w_lat, w_sh = _prep_input_weight(w_in, layer)
    xb, cq, ckv, kr = _latent_proj(x, w_lat, q_norm[layer][None, :], kv_norm[layer][None, :],
                                   cos_a, sin_a)
    qf = _q_up(cq, _prep_q_weight(w_q_up[layer]), cos_a, sin_a)
    kf, v_a = _kv_up(ckv, w_kv_up, layer, kr)
    o_a = _mla_attention(qf, kf, v_a, batch, seq)

    qkv_b = _swa_proj(xb, w_sh, cos_b, sin_b)
    sink_rows = jnp.broadcast_to(
        jnp.repeat(sink[layer].astype(F32).reshape(SWA_KV_HEADS, SWA_GROUP), SWA_BLOCK, axis=1)[:, :, None],
        (SWA_KV_HEADS, SWA_GROUP * SWA_BLOCK, LANES))
    o_b = _swa_attention(qkv_b, sink_rows, batch, seq)

    merged = _branch_merge(xb, o_a, o_b, w_sh, w_branch_a, w_branch_b, layer)
    pre1 = _out_proj(merged, w_out, layer, x)

    pad = ROUTE_LANES - N_GROUPS - N_EXPERTS
    w_r = jnp.concatenate([w_group[layer], w_expert_router[layer], jnp.zeros((D_MODEL, pad), F32)],
                          axis=1)
    b_r = jnp.concatenate([b_group[layer], b_expert[layer], jnp.zeros((pad,), F32)])[None, :]
    x1, x1b, x1lin, route, route_t, counts = _ln_route(pre1, ln1_w[layer][None, :],
                                                       ln1_b[layer][None, :], w_r, b_r)

    n_tiles = (2 * t) // MOE_TILE + N_EXPERTS
    ri = route_t.astype(jnp.int32)
    cnt = counts[0, EXPERT_LANE0:EXPERT_LANE0 + N_EXPERTS].astype(jnp.int32)
    pos1, pos2, src, te, nv = _dispatch_plan(ri[6], ri[7], cnt, n_tiles)

    ys = _moe_experts(te, nv, src, x1lin, w_gate, w_up, w_down, layer, n_tiles)
    ys_lin = ys.reshape(n_tiles * MOE_TILE * ROW_CHUNKS, LANES)

    pre2 = _ple(x1b, w_ple_gate, p, w_ple_up, layer, x1)
    return _final(pos1, pos2, pre2, route, ln2_w[layer][None, :], ln2_b[layer][None, :], ys_lin)


def kernel(x, p, positions, w_in, q_norm, kv_norm, w_q_up, w_kv_up, sink, w_branch_a, w_branch_b,
           w_out, ln1_w, ln1_b, w_group, b_group, w_expert_router, b_expert, w_gate, w_up, w_down,
           w_ple_up, w_ple_gate, ln2_w, ln2_b):
    batch, seq, d = x.shape
    t = batch * seq
    h = x.reshape(t, d)
    tabs = _rope_tables(positions.reshape(t, 1))
    p2 = p.reshape(p.shape[0], t, PLE_DIM)
    for layer in range(w_in.shape[0]):
        h = _layer(layer, h, p2, tabs, batch, seq, w_in, q_norm, kv_norm, w_q_up, w_kv_up, sink,
                   w_branch_a, w_branch_b, w_out, ln1_w, ln1_b, w_group, b_group, w_expert_router,
                   b_expert, w_gate, w_up, w_down, w_ple_up, w_ple_gate, ln2_w, ln2_b)
    return h.reshape(batch, seq, d)
```
